```python
import math
import jax
import jax.numpy as jnp
from jax import lax
import numpy as np

D_MODEL = 4096
BATCH = 1
SEQ = 8192
DEPTH = 4

GRID_W = 64
CTX_LEN = 256
HEAD_DIM = 128
D_ATTN = D_MODEL // 2
N_Q_HEADS = D_ATTN // HEAD_DIM
N_KV_HEADS = N_Q_HEADS // 4
KV_REP = N_Q_HEADS // N_KV_HEADS
D_KV = N_KV_HEADS * HEAD_DIM
Q_BLOCK = 128
ROPE_THETA = 10000.0
ATTN_SCALE = HEAD_DIM ** -0.5
D_HYENA = D_MODEL // 4
HYENA_ORDER = 2
HYENA_EMB = 33
HYENA_FILTER_HIDDEN = 64
HYENA_SHIFT = 0.05
HYENA_FAST_PCT = 0.3
HYENA_SLOW_PCT = 1.5
HYENA_TARGET = 1e-2
D_GMLP = D_MODEL // 4
GMLP_CHUNK = 128
GMLP_GROUP_W = 128
N_GMLP_GROUPS = D_GMLP // GMLP_GROUP_W
N_BRANCH = 3
D_MIX = D_HYENA + D_ATTN + D_GMLP
GATE_RANK = 512
MOD_RANK = 256
N_MOD = 6
N_EXPERT_GROUPS = 4
EXPERTS_PER_GROUP = 8
N_EXPERTS = N_EXPERT_GROUPS * EXPERTS_PER_GROUP
TOP_K = 2
D_EXPERT = 256
EPS = 1e-6
LN_EPS = 1e-5
Q_END = D_ATTN
K_END = Q_END + D_KV
V_END = K_END + D_KV
HY_END = V_END + (HYENA_ORDER + 1) * D_HYENA
GM_END = HY_END + 2 * D_GMLP
IN_COLS = GM_END + GATE_RANK

F32 = jnp.float32

kernel_name = 'hybrid_hyena_gqa_gmlp_hmoe_dit'


def _rms_norm(x, g):
    xf = x.astype(F32)
    y = xf * lax.rsqrt(jnp.mean(xf * xf, axis=-1, keepdims=True) + EPS)
    return (y * g.astype(F32)).astype(x.dtype)


def _layer_norm(x, g, b):
    xf = x.astype(F32)
    xc = xf - jnp.mean(xf, axis=-1, keepdims=True)
    var = jnp.mean(xc * xc, axis=-1, keepdims=True)
    return (xc * lax.rsqrt(var + LN_EPS) * g.astype(F32) + b.astype(F32)).astype(x.dtype)


def _adaln(cond, a, b, bias):
    m = (jax.nn.silu(cond) @ a) @ b + bias
    return m.reshape(cond.shape[0], N_MOD, D_MODEL)


def _modulate(x, g, shift, scale):
    return _rms_norm(x, g) * (1.0 + scale[:, None]) + shift[:, None]


def _heads(t, n_heads, g):
    bsz, length, _ = t.shape
    return _rms_norm(t.reshape(bsz, length, n_heads, HEAD_DIM), g)


def _rope_axis(xa, pos):
    n = xa.shape[-1] // 2
    inv = ROPE_THETA ** (-jnp.arange(n, dtype=F32) / n)
    ang = pos.astype(F32)[:, None] * inv[None, :]
    cos = jnp.cos(ang)[None, :, None, :]
    sin = jnp.sin(ang)[None, :, None, :]
    x1 = xa[..., :n].astype(F32)
    x2 = xa[..., n:].astype(F32)
    return jnp.concatenate([x1 * cos - x2 * sin, x2 * cos + x1 * sin], axis=-1)


def _axial_rope(x, rows, cols):
    half = HEAD_DIM // 2
    y = jnp.concatenate([_rope_axis(x[..., :half], rows), _rope_axis(x[..., half:], cols)], axis=-1)
    return y.astype(x.dtype)


def _block_attention(q, k, v):
    bsz, sq = q.shape[:2]
    nb = sq // Q_BLOCK
    qb = (q * ATTN_SCALE).reshape(bsz, nb, Q_BLOCK, N_KV_HEADS, KV_REP, HEAD_DIM)
    qb = jnp.moveaxis(qb, 1, 0)

    def one_block(qblk):
        s = jnp.einsum('bqkrd,bskd->bkrqs', qblk, k).astype(F32)
        p = jax.nn.softmax(s, axis=-1).astype(v.dtype)
        return jnp.einsum('bkrqs,bskd->bqkrd', p, v)

    o = lax.map(one_block, qb)
    return jnp.moveaxis(o, 0, 1).reshape(bsz, sq, D_ATTN)


def _short_conv(p, w, b):
    pad = jnp.pad(p, ((0, 0), (1, 1), (0, 0)))
    return pad[:, :-2] * w[0] + pad[:, 1:-1] * w[1] + pad[:, 2:] * w[2] + b


def _hyena_filters(length, w1, b1, w2, b2, w3, b3, w4, freq):
    t = jnp.linspace(0.0, 1.0, length, dtype=F32)[:, None]
    bands = (HYENA_EMB - 1) // 2
    f = jnp.linspace(1e-4, bands - 1, bands, dtype=F32)[None, :]
    w = 2.0 * math.pi * jnp.arange(length, dtype=F32)[:, None] / length
    z = jnp.concatenate([t, jnp.cos(-f * w), jnp.sin(-f * w)], axis=-1)
    fr = freq.astype(F32)
    h = jnp.sin(fr * (z @ w1.astype(F32) + b1.astype(F32)))
    h = jnp.sin(fr * (h @ w2.astype(F32) + b2.astype(F32)))
    h = jnp.sin(fr * (h @ w3.astype(F32) + b3.astype(F32)))
    h = (h @ w4.astype(F32)).reshape(length, 2, HYENA_ORDER, D_HYENA)
    max_decay = math.log(HYENA_TARGET) / HYENA_FAST_PCT
    min_decay = math.log(HYENA_TARGET) / HYENA_SLOW_PCT
    deltas = jnp.abs(jnp.linspace(min_decay, max_decay, D_HYENA, dtype=F32))
    decay = jnp.exp(-t[:, :, None, None] * deltas)
    h = h * (decay + HYENA_SHIFT)
    return h / jnp.sum(jnp.abs(h), axis=(0, 1), keepdims=True)


def _bidir_long_conv(u, k_fwd, k_bwd, skip):
    length = u.shape[1]
    n = 2 * length
    k = jnp.concatenate([k_fwd, k_bwd[::-1]], axis=0)
    uf = jnp.fft.rfft(u.astype(F32), n=n, axis=1)
    kf = jnp.fft.rfft(k.astype(F32), n=n, axis=0)
    y = jnp.fft.irfft(uf * kf[None], n=n, axis=1)[:, :length]
    return (y + u.astype(F32) * skip.astype(F32)).astype(u.dtype)


def _hyena_branch(p, filt, conv_w, conv_b, skip):
    p = _short_conv(p, conv_w, conv_b)
    v, x1, x2 = jnp.split(p, 3, axis=-1)
    z = x1 * _bidir_long_conv(v, filt[:, 0, 0], filt[:, 1, 0], skip[0])
    return x2 * _bidir_long_conv(z, filt[:, 0, 1], filt[:, 1, 1], skip[1])


def _gmlp_branch(p, ln_g, ln_b, ws, bs):
    u, v = jnp.split(jax.nn.gelu(p), 2, axis=-1)
    v = _layer_norm(v, ln_g, ln_b)
    bsz, length, _ = v.shape
    vc = v.reshape(bsz, length // GMLP_CHUNK, GMLP_CHUNK, N_GMLP_GROUPS, GMLP_GROUP_W)
    vm = jnp.einsum('gpq,bnqgc->bnpgc', ws, vc) + bs.T[None, None, :, :, None]
    return u * vm.reshape(bsz, length, D_GMLP)


def _merge(o_hy, o_at, o_gm, g_code, gate_up, gate_b, w_br, w_out):
    y_hy = o_hy @ w_br[:D_HYENA]
    y_at = o_at @ w_br[D_HYENA:D_HYENA + D_ATTN]
    y_gm = o_gm @ w_br[D_HYENA + D_ATTN:]
    merged = (jax.nn.sigmoid(g_code @ gate_up[0] + gate_b[0]) * y_hy
              + jax.nn.sigmoid(g_code @ gate_up[1] + gate_b[1]) * y_at
              + jax.nn.sigmoid(g_code @ gate_up[2] + gate_b[2]) * y_gm)
    return merged @ w_out


def _mixer(p, q, k, v, filt, conv_w, conv_b, skip, ln_g, ln_b, ws, bs, gate_up, gate_b, w_br, w_out):
    o_hy = _hyena_branch(p[..., V_END:HY_END], filt, conv_w, conv_b, skip)
    o_at = _block_attention(q, k, v)
    o_gm = _gmlp_branch(p[..., HY_END:GM_END], ln_g, ln_b, ws, bs)
    return _merge(o_hy, o_at, o_gm, p[..., GM_END:], gate_up, gate_b, w_br, w_out)


def _hier_moe(h, rg_w, rg_b, re_w, re_b, w_gate, w_up, w_down):
    lg = (h @ rg_w + rg_b).astype(F32)
    grp = jnp.argmax(lg, axis=-1)
    g_hot = jax.nn.one_hot(grp, N_EXPERT_GROUPS, dtype=F32)
    p_grp = jnp.sum(jax.nn.softmax(lg, axis=-1) * g_hot, axis=-1, keepdims=True)
    le = (jnp.einsum('bld,gde->blge', h, re_w) + re_b).astype(F32)
    le = jnp.einsum('blge,blg->ble', le, g_hot)
    top_p, top_i = lax.top_k(jax.nn.softmax(le, axis=-1), TOP_K)
    w = top_p / jnp.sum(top_p, axis=-1, keepdims=True) * p_grp
    eid = grp[..., None] * EXPERTS_PER_GROUP + top_i
    comb = jnp.sum(w[..., None] * jax.nn.one_hot(eid, N_EXPERTS, dtype=F32), axis=-2)
    hid = jax.nn.silu(jnp.einsum('bld,edf->eblf', h, w_gate)) * jnp.einsum('bld,edf->eblf', h, w_up)
    hid = hid * jnp.moveaxis(comb, -1, 0)[..., None].astype(hid.dtype)
    return jnp.einsum('eblf,efd->bld', hid, w_down).astype(h.dtype)


def setup_inputs(seed: int = 0) -> dict:
    key = jax.random.key(seed)
    keys = jax.random.split(key, 48)
    ks = [keys[i] for i in range(48)]

    def nrm(shape, scale):
        return jax.random.normal(ks.pop(), shape, jnp.float32) * scale

    L = DEPTH
    D = D_MODEL
    H = HYENA_FILTER_HIDDEN
    return {
        'x': nrm((BATCH, SEQ, D), 1.0),
        'c': nrm((BATCH, D), 1.0),
        'ctx': nrm((BATCH, CTX_LEN, D), 1.0),
        'c_ctx': nrm((D,), 1.0),
        'norm1_g': 1.0 + nrm((L, D), 0.02),
        'norm2_g': 1.0 + nrm((L, D), 0.02),
        'mod_a': nrm((L, D, MOD_RANK), D ** -0.5),
        'mod_b': nrm((L, MOD_RANK, N_MOD * D), 0.5 * MOD_RANK ** -0.5),
        'mod_bias': nrm((L, N_MOD * D), 0.02),
        'w_in': nrm((L, D, IN_COLS), D ** -0.5),
        'q_norm_g': 1.0 + nrm((L, HEAD_DIM), 0.02),
        'k_norm_g': 1.0 + nrm((L, HEAD_DIM), 0.02),
        'hy_conv_w': nrm((L, 3, (HYENA_ORDER + 1) * D_HYENA), 3 ** -0.5),
        'hy_conv_b': nrm((L, (HYENA_ORDER + 1) * D_HYENA), 0.02),
        'hy_w1': nrm((L, HYENA_EMB, H), HYENA_EMB ** -0.5),
        'hy_b1': nrm((L, H), 0.1),
        'hy_w2': nrm((L, H, H), H ** -0.5),
        'hy_b2': nrm((L, H), 0.1),
        'hy_w3': nrm((L, H, H), H ** -0.5),
        'hy_b3': nrm((L, H), 0.1),
        'hy_w4': nrm((L, H, 2 * HYENA_ORDER * D_HYENA), H ** -0.5),
        'hy_freq': 1.0 + nrm((L, H), 0.1),
        'hy_skip': nrm((L, HYENA_ORDER, D_HYENA), 0.5),
        'gm_ln_g': 1.0 + nrm((L, D_GMLP), 0.02),
        'gm_ln_b': nrm((L, D_GMLP), 0.02),
        'gm_ws': nrm((L, N_GMLP_GROUPS, GMLP_CHUNK, GMLP_CHUNK), GMLP_CHUNK ** -0.5),
        'gm_bs': 1.0 + nrm((L, N_GMLP_GROUPS, GMLP_CHUNK), 0.02),
        'gate_up': nrm((L, N_BRANCH, GATE_RANK, D), GATE_RANK ** -0.5),
        'gate_b': nrm((L, N_BRANCH, D), 0.02),
        'w_br': nrm((L, D_MIX, D), (D_MIX // N_BRANCH) ** -0.5),
        'w_out': nrm((L, D, D), D ** -0.5),
        'rt_group_w': nrm((L, D, N_EXPERT_GROUPS), D ** -0.5),
        'rt_group_b': nrm((L, N_EXPERT_GROUPS), 0.01),
        'rt_expert_w': nrm((L, N_EXPERT_GROUPS, D, EXPERTS_PER_GROUP), D ** -0.5),
        'rt_expert_b': nrm((L, N_EXPERT_GROUPS, EXPERTS_PER_GROUP), 0.01),
        'ex_w_gate': nrm((L, N_EXPERTS, D, D_EXPERT), D ** -0.5),
        'ex_w_up': nrm((L, N_EXPERTS, D, D_EXPERT), D ** -0.5),
        'ex_w_down': nrm((L, N_EXPERTS, D_EXPERT, D), D_EXPERT ** -0.5),
        'final_norm_g': 1.0 + nrm((D,), 0.02),
    }


def reference(x, c, ctx, c_ctx, norm1_g, norm2_g, mod_a, mod_b, mod_bias, w_in, q_norm_g, k_norm_g,
              hy_conv_w, hy_conv_b, hy_w1, hy_b1, hy_w2, hy_b2, hy_w3, hy_b3, hy_w4, hy_freq, hy_skip,
              gm_ln_g, gm_ln_b, gm_ws, gm_bs, gate_up, gate_b, w_br, w_out,
              rt_group_w, rt_group_b, rt_expert_w, rt_expert_b, ex_w_gate, ex_w_up, ex_w_down,
              final_norm_g):
    bsz, seq, _ = x.shape
    ROWS = seq // GRID_W
    rows = jnp.repeat(jnp.arange(ROWS, dtype=jnp.int32), GRID_W)
    cols = jnp.tile(jnp.arange(GRID_W, dtype=jnp.int32), ROWS)
    cs = ctx
    for l in range(DEPTH):
        last = l == DEPTH - 1
        m_x = _adaln(c, mod_a[l], mod_b[l], mod_bias[l])
        m_c = _adaln(c_ctx[None], mod_a[l], mod_b[l], mod_bias[l])
        hyena_w = (hy_w1[l], hy_b1[l], hy_w2[l], hy_b2[l], hy_w3[l], hy_b3[l], hy_w4[l], hy_freq[l])
        mixer_w = (hy_conv_w[l], hy_conv_b[l], hy_skip[l], gm_ln_g[l], gm_ln_b[l], gm_ws[l], gm_bs[l],
                   gate_up[l], gate_b[l], w_br[l], w_out[l])
        moe_w = (rt_group_w[l], rt_group_b[l], rt_expert_w[l], rt_expert_b[l],
                 ex_w_gate[l], ex_w_up[l], ex_w_down[l])
        hx = _modulate(x, norm1_g[l], m_x[:, 0], m_x[:, 1])
        hc = _modulate(cs, norm1_g[l], m_c[:, 0], m_c[:, 1])
        if last:
            pkv = hc @ w_in[l][:, Q_END:V_END]
            k_c = _heads(pkv[..., :D_KV], N_KV_HEADS, k_norm_g[l])
            v_c = pkv[..., D_KV:].reshape(bsz, -1, N_KV_HEADS, HEAD_DIM)
        else:
            pc = hc @ w_in[l]
            q_c = _heads(pc[..., :Q_END], N_Q_HEADS, q_norm_g[l])
            k_c = _heads(pc[..., Q_END:K_END], N_KV_HEADS, k_norm_g[l])
            v_c = pc[..., K_END:V_END].reshape(bsz, -1, N_KV_HEADS, HEAD_DIM)
            filt_c = _hyena_filters(cs.shape[1], *hyena_w)
            mix_c = _mixer(pc, q_c, k_c, v_c, filt_c, *mixer_w)
        px = hx @ w_in[l]
        q_x = _axial_rope(_heads(px[..., :Q_END], N_Q_HEADS, q_norm_g[l]), rows, cols)
        k_x = _axial_rope(_heads(px[..., Q_END:K_END], N_KV_HEADS, k_norm_g[l]), rows, cols)
        v_x = px[..., K_END:V_END].reshape(bsz, seq, N_KV_HEADS, HEAD_DIM)
        k_all = jnp.concatenate([k_c, k_x], axis=1)
        v_all = jnp.concatenate([v_c, v_x], axis=1)
        filt_x = _hyena_filters(seq, *hyena_w)
        mix_x = _mixer(px, q_x, k_all, v_all, filt_x, *mixer_w)
        x = x + m_x[:, 2][:, None] * mix_x
        x = x + m_x[:, 5][:, None] * _hier_moe(_modulate(x, norm2_g[l], m_x[:, 3], m_x[:, 4]), *moe_w)
        if not last:
            cs = cs + m_c[:, 2][:, None] * mix_c
            cs = cs + m_c[:, 5][:, None] * _hier_moe(_modulate(cs, norm2_g[l], m_c[:, 3], m_c[:, 4]), *moe_w)
    return _rms_norm(x, final_norm_g)
```

```python
import functools
import math

import numpy as np
import jax
import jax.numpy as jnp
from jax import lax
from jax.experimental import pallas as pl
from jax.experimental.pallas import tpu as pltpu

F32 = jnp.float32
BF16 = jnp.bfloat16
HIGHEST = lax.Precision.HIGHEST

HEAD_DIM = 128
KV_REP = 4
GRID_W = 64
ROPE_THETA = 10000.0
ATTN_SCALE = HEAD_DIM ** -0.5
HYENA_ORDER = 2
HYENA_SHIFT = 0.05
HYENA_FAST_PCT = 0.3
HYENA_SLOW_PCT = 1.5
HYENA_TARGET = 1e-2
GMLP_CHUNK = 128
GMLP_GROUP_W = 128
N_BRANCH = 3
N_MOD = 6
TOP_K = 2
EPS = 1e-6
LN_EPS = 1e-5

LANES = 128
ROUTER_COLS = LANES
VMEM_LIMIT_MB = 56


def _params(semantics, vmem_mb=VMEM_LIMIT_MB):
    return pltpu.CompilerParams(dimension_semantics=semantics, vmem_limit_bytes=vmem_mb * 1024 * 1024)


def _tile(n, target, mult):
    best = None
    for d in range(mult, min(n, target) + 1, mult):
        if n % d == 0:
            best = d
    assert best is not None, (n, target, mult)
    return best


ROW_CHUNK = 16


def _modulate_rows(x_ref, g_ref, sh_ref, sc_ref, h_ref, row0, n_lat, n_rows):
    def body(r, carry):
        rows = pl.ds(pl.multiple_of(r * ROW_CHUNK, ROW_CHUNK), ROW_CHUNK)
        xf = x_ref[rows, :]
        ms = jnp.mean(xf * xf, axis=-1, keepdims=True)
        y = xf * lax.rsqrt(ms + EPS) * g_ref[...]
        rid = row0 + r * ROW_CHUNK + lax.broadcasted_iota(jnp.int32, (ROW_CHUNK, 1), 0)
        is_ctx = rid >= n_lat
        sc = jnp.where(is_ctx, sc_ref[1:2, :], sc_ref[0:1, :])
        sh = jnp.where(is_ctx, sh_ref[1:2, :], sh_ref[0:1, :])
        h_ref[rows, :] = (y * (1.0 + sc) + sh).astype(h_ref.dtype)
        return carry

    lax.fori_loop(0, n_rows // ROW_CHUNK, body, 0)


def _modproj_kernel(x_ref, g_ref, sh_ref, sc_ref, w_ref, o_ref, h_ref, *, n_lat, tm):
    i = pl.program_id(0)

    @pl.when(pl.program_id(1) == 0)
    def _():
        _modulate_rows(x_ref, g_ref, sh_ref, sc_ref, h_ref, i * tm, n_lat, tm)

    o_ref[...] = jnp.dot(h_ref[...], w_ref[...], preferred_element_type=F32).astype(o_ref.dtype)


def _modproj(x, g, sh, sc, w, n_lat):
    t, d = x.shape
    n = w.shape[1]
    tm = _tile(t, 528, 16)
    tn = _tile(n, 512, LANES)
    return pl.pallas_call(
        functools.partial(_modproj_kernel, n_lat=n_lat, tm=tm),
        grid=(t // tm, n // tn),
        in_specs=[
            pl.BlockSpec((tm, d), lambda i, j: (i, 0)),
            pl.BlockSpec((1, d), lambda i, j: (0, 0)),
            pl.BlockSpec((2, d), lambda i, j: (0, 0)),
            pl.BlockSpec((2, d), lambda i, j: (0, 0)),
            pl.BlockSpec((d, tn), lambda i, j: (0, j)),
        ],
        out_specs=pl.BlockSpec((tm, tn), lambda i, j: (i, j)),
        out_shape=jax.ShapeDtypeStruct((t, n), BF16),
        scratch_shapes=[pltpu.VMEM((tm, d), BF16)],
        compiler_params=_params(("parallel", "arbitrary")),
        name="modproj",
    )(x, g.reshape(1, d), sh, sc, w)


def _qkprep_kernel(p_ref, cos_ref, sin_ref, qg_ref, kg_ref, q_ref, k_ref, *, n_q, n_kv):
    cos = cos_ref[...]
    sin = sin_ref[...]
    lane = lax.broadcasted_iota(jnp.int32, (1, HEAD_DIM), 1)
    first_half = (lane % (HEAD_DIM // 2)) < (HEAD_DIM // 4)

    def head(h, gain, scale):
        xh = p_ref[:, h * HEAD_DIM:(h + 1) * HEAD_DIM].astype(F32)
        ms = jnp.mean(xh * xh, axis=-1, keepdims=True)
        y = xh * lax.rsqrt(ms + EPS) * gain
        partner = jnp.where(first_half,
                            pltpu.roll(y, HEAD_DIM - HEAD_DIM // 4, 1),
                            pltpu.roll(y, HEAD_DIM // 4, 1))
        return (y * cos + partner * sin) * scale

    for h in range(n_q):
        q_ref[h] = head(h, qg_ref[...], ATTN_SCALE).astype(q_ref.dtype)
    for h in range(n_kv):
        k_ref[h] = head(n_q + h, kg_ref[...], 1.0).astype(k_ref.dtype)


def _qkprep(px, cos, sin, qg, kg, n_q, n_kv):
    t = px.shape[0]
    tr = _tile(t, 264, 8)
    width = (n_q + n_kv) * HEAD_DIM
    return pl.pallas_call(
        functools.partial(_qkprep_kernel, n_q=n_q, n_kv=n_kv),
        grid=(t // tr,),
        in_specs=[
            pl.BlockSpec((tr, width), lambda i: (i, 0)),
            pl.BlockSpec((tr, HEAD_DIM), lambda i: (i, 0)),
            pl.BlockSpec((tr, HEAD_DIM), lambda i: (i, 0)),
            pl.BlockSpec((1, HEAD_DIM), lambda i: (0, 0)),
            pl.BlockSpec((1, HEAD_DIM), lambda i: (0, 0)),
        ],
        out_specs=[
            pl.BlockSpec((n_q, tr, HEAD_DIM), lambda i: (0, i, 0)),
            pl.BlockSpec((n_kv, tr, HEAD_DIM), lambda i: (0, i, 0)),
        ],
        out_shape=[
            jax.ShapeDtypeStruct((n_q, t, HEAD_DIM), BF16),
            jax.ShapeDtypeStruct((n_kv, t, HEAD_DIM), BF16),
        ],
        compiler_params=_params(("parallel",)),
        name="qkprep",
    )(px, cos, sin, qg.reshape(1, HEAD_DIM), kg.reshape(1, HEAD_DIM))


def _rope_tables(n_lat, n_ctx):
    n = HEAD_DIM // 4
    inv = ROPE_THETA ** (-np.arange(n, dtype=np.float32) / n)
    t = np.arange(n_lat)
    rows = (t // GRID_W).astype(np.float32)
    cols = (t % GRID_W).astype(np.float32)
    ang = np.concatenate([rows[:, None] * inv[None, :]] * 2 + [cols[:, None] * inv[None, :]] * 2, axis=1)
    ang = jnp.asarray(ang.astype(np.float32))
    sign = np.tile(np.concatenate([-np.ones(n, np.float32), np.ones(n, np.float32)]), 2)
    cos = jnp.concatenate([jnp.cos(ang), jnp.ones((n_ctx, HEAD_DIM), F32)], axis=0)
    sin = jnp.concatenate([jnp.sin(ang) * sign[None, :], jnp.zeros((n_ctx, HEAD_DIM), F32)], axis=0)
    return cos, sin


def _attn_kernel(*refs, tq, tk, n_kv, aliased):
    if aliased:
        q_ref, k_ref, v_ref, _, o_ref, m_ref, l_ref, acc_ref = refs
    else:
        q_ref, k_ref, v_ref, o_ref, m_ref, l_ref, acc_ref = refs
    m_rows = KV_REP * tq
    q = q_ref[...].reshape(m_rows, HEAD_DIM)
    m_ref[...] = jnp.full((m_rows, 1), -jnp.inf, F32)
    l_ref[...] = jnp.zeros((m_rows, 1), F32)
    acc_ref[...] = jnp.zeros((m_rows, HEAD_DIM), F32)

    def body(c, carry):
        ks = pl.ds(pl.multiple_of(c * tk, tk), tk)
        k = k_ref[0, ks, :]
        v = v_ref[ks, :]
        s = lax.dot_general(q, k, (((1,), (1,)), ((), ())), preferred_element_type=F32)
        m_prev = m_ref[...]
        m_new = jnp.maximum(m_prev, jnp.max(s, axis=-1, keepdims=True))
        alpha = jnp.exp(m_prev - m_new)
        p = jnp.exp(s - m_new)
        l_ref[...] = alpha * l_ref[...] + jnp.sum(p, axis=-1, keepdims=True)
        acc_ref[...] = alpha * acc_ref[...] + jnp.dot(p.astype(BF16), v, preferred_element_type=F32)
        m_ref[...] = m_new
        return carry

    lax.fori_loop(0, n_kv // tk, body, 0)
    out = acc_ref[...] / l_ref[...]
    for r in range(KV_REP):
        o_ref[:, r * HEAD_DIM:(r + 1) * HEAD_DIM] = out[r * tq:(r + 1) * tq].astype(o_ref.dtype)


def _attention(q, k, px, v_col_block, q_row0, n_q_rows, kv_row0, n_kv, tq, tk, out=None):
    n_heads, t, _ = q.shape
    n_kv_heads = k.shape[0]
    assert q_row0 % tq == 0 and n_q_rows % tq == 0 and kv_row0 % n_kv == 0 and n_kv % tk == 0
    qb0 = q_row0 // tq
    kb0 = kv_row0 // n_kv
    aliased = out is not None
    in_specs = [
        pl.BlockSpec((KV_REP, tq, HEAD_DIM), lambda g, i: (g, qb0 + i, 0)),
        pl.BlockSpec((1, n_kv, HEAD_DIM), lambda g, i: (g, kb0, 0)),
        pl.BlockSpec((n_kv, HEAD_DIM), lambda g, i: (kb0, v_col_block + g)),
    ]
    args = [q, k, px]
    if aliased:
        in_specs.append(pl.BlockSpec(memory_space=pl.ANY))
        args.append(out)
    return pl.pallas_call(
        functools.partial(_attn_kernel, tq=tq, tk=tk, n_kv=n_kv, aliased=aliased),
        grid=(n_kv_heads, n_q_rows // tq),
        in_specs=in_specs,
        out_specs=pl.BlockSpec((tq, KV_REP * HEAD_DIM), lambda g, i: (qb0 + i, g)),
        out_shape=jax.ShapeDtypeStruct((t, n_heads * HEAD_DIM), BF16),
        scratch_shapes=[
            pltpu.VMEM((KV_REP * tq, 1), F32),
            pltpu.VMEM((KV_REP * tq, 1), F32),
            pltpu.VMEM((KV_REP * tq, HEAD_DIM), F32),
        ],
        input_output_aliases={3: 0} if aliased else {},
        compiler_params=_params(("parallel", "arbitrary")),
        name="attention",
    )(*args)


def _gmlp_kernel(p_ref, lng_ref, lnb_ref, ws_ref, bst_ref, o_ref, *, n_chunks, d_g):
    n_groups = d_g // GMLP_GROUP_W
    for b in range(n_chunks):
        rows = slice(b * GMLP_CHUNK, (b + 1) * GMLP_CHUNK)
        uv = jax.nn.gelu(p_ref[rows, :].astype(F32))
        u = uv[:, :d_g]
        v = uv[:, d_g:]
        vc = v - jnp.mean(v, axis=-1, keepdims=True)
        var = jnp.mean(vc * vc, axis=-1, keepdims=True)
        vn = (vc * lax.rsqrt(var + LN_EPS) * lng_ref[...] + lnb_ref[...]).astype(BF16)
        for g in range(n_groups):
            cols = slice(g * GMLP_GROUP_W, (g + 1) * GMLP_GROUP_W)
            vm = jnp.dot(ws_ref[g].astype(BF16), vn[:, cols], preferred_element_type=F32) + bst_ref[:, g:g + 1]
            o_ref[rows, cols] = (u[:, cols] * vm).astype(o_ref.dtype)


def _gmlp(px, col_block, d_g, ln_g, ln_b, ws, bs):
    t = px.shape[0]
    n_chunks = 2
    tr = n_chunks * GMLP_CHUNK
    assert t % tr == 0
    n_groups = d_g // GMLP_GROUP_W
    return pl.pallas_call(
        functools.partial(_gmlp_kernel, n_chunks=n_chunks, d_g=d_g),
        grid=(t // tr,),
        in_specs=[
            pl.BlockSpec((tr, 2 * d_g), lambda i: (i, col_block)),
            pl.BlockSpec((1, d_g), lambda i: (0, 0)),
            pl.BlockSpec((1, d_g), lambda i: (0, 0)),
            pl.BlockSpec((n_groups, GMLP_CHUNK, GMLP_CHUNK), lambda i: (0, 0, 0)),
            pl.BlockSpec((GMLP_CHUNK, n_groups), lambda i: (0, 0)),
        ],
        out_specs=pl.BlockSpec((tr, d_g), lambda i: (i, 0)),
        out_shape=jax.ShapeDtypeStruct((t, d_g), BF16),
        compiler_params=_params(("parallel",)),
        name="gmlp",
    )(px, ln_g.reshape(1, d_g), ln_b.reshape(1, d_g), ws, bs.T)


def _merge_kernel(hy_ref, at_ref, gm_ref, gc_ref, wbr_ref, gup_ref, gb_ref, o_ref, *, d_hy, d_at):
    gc = gc_ref[...]
    bounds = (0, d_hy, d_hy + d_at, wbr_ref.shape[0])
    merged = None
    for b, a_ref in enumerate((hy_ref, at_ref, gm_ref)):
        y = jnp.dot(a_ref[...], wbr_ref[bounds[b]:bounds[b + 1], :], preferred_element_type=F32)
        gate = jax.nn.sigmoid(jnp.dot(gc, gup_ref[b], preferred_element_type=F32) + gb_ref[b:b + 1, :])
        merged = gate * y if merged is None else merged + gate * y
    o_ref[...] = merged.astype(o_ref.dtype)


def _merge(o_hy, o_at, o_gm, px, gc_col_block, gate_rank, w_br, gate_up, gate_b):
    t = px.shape[0]
    d_hy, d_at, d_gm = o_hy.shape[1], o_at.shape[1], o_gm.shape[1]
    d = w_br.shape[1]
    tm = _tile(t, 528, 16)
    tn = _tile(d, 512, LANES)
    return pl.pallas_call(
        functools.partial(_merge_kernel, d_hy=d_hy, d_at=d_at),
        grid=(t // tm, d // tn),
        in_specs=[
            pl.BlockSpec((tm, d_hy), lambda i, j: (i, 0)),
            pl.BlockSpec((tm, d_at), lambda i, j: (i, 0)),
            pl.BlockSpec((tm, d_gm), lambda i, j: (i, 0)),
            pl.BlockSpec((tm, gate_rank), lambda i, j: (i, gc_col_block)),
            pl.BlockSpec((d_hy + d_at + d_gm, tn), lambda i, j: (0, j)),
            pl.BlockSpec((N_BRANCH, gate_rank, tn), lambda i, j: (0, 0, j)),
            pl.BlockSpec((N_BRANCH, tn), lambda i, j: (0, j)),
        ],
        out_specs=pl.BlockSpec((tm, tn), lambda i, j: (i, j)),
        out_shape=jax.ShapeDtypeStruct((t, d), BF16),
        compiler_params=_params(("parallel", "arbitrary")),
        name="merge",
    )(o_hy, o_at, o_gm, px, w_br, gate_up, gate_b)


def _outproj_kernel(a_ref, w_ref, x_ref, gate_ref, o_ref, *, n_lat, tm):
    y = jnp.dot(a_ref[...], w_ref[...], preferred_element_type=F32)
    rid = pl.program_id(0) * tm + lax.broadcasted_iota(jnp.int32, (tm, 1), 0)
    gate = jnp.where(rid >= n_lat, gate_ref[1:2, :], gate_ref[0:1, :])
    o_ref[...] = x_ref[...] + gate * y


def _outproj(a, w, x, gate, n_lat):
    t, k = a.shape
    d = w.shape[1]
    tm = _tile(t, 528, 16)
    tn = _tile(d, 512, LANES)
    return pl.pallas_call(
        functools.partial(_outproj_kernel, n_lat=n_lat, tm=tm),
        grid=(t // tm, d // tn),
        in_specs=[
            pl.BlockSpec((tm, k), lambda i, j: (i, 0)),
            pl.BlockSpec((k, tn), lambda i, j: (0, j)),
            pl.BlockSpec((tm, tn), lambda i, j: (i, j)),
            pl.BlockSpec((2, tn), lambda i, j: (0, j)),
        ],
        out_specs=pl.BlockSpec((tm, tn), lambda i, j: (i, j)),
        out_shape=jax.ShapeDtypeStruct((t, d), F32),
        compiler_params=_params(("parallel", "arbitrary")),
        name="outproj",
    )(a, w, x, gate)


def _router_kernel(x_ref, g_ref, sh_ref, sc_ref, rw_ref, rb_ref, h_ref, lg_ref, *, n_lat, tm):
    _modulate_rows(x_ref, g_ref, sh_ref, sc_ref, h_ref, pl.program_id(0) * tm, n_lat, tm)
    lg_ref[...] = jnp.dot(h_ref[...].astype(F32), rw_ref[...], preferred_element_type=F32,
                          precision=HIGHEST) + rb_ref[...]


def _router(x, g, sh, sc, rw, rb, n_lat):
    t, d = x.shape
    tm = _tile(t, 264, 16)
    return pl.pallas_call(
        functools.partial(_router_kernel, n_lat=n_lat, tm=tm),
        grid=(t // tm,),
        in_specs=[
            pl.BlockSpec((tm, d), lambda i: (i, 0)),
            pl.BlockSpec((1, d), lambda i: (0, 0)),
            pl.BlockSpec((2, d), lambda i: (0, 0)),
            pl.BlockSpec((2, d), lambda i: (0, 0)),
            pl.BlockSpec((d, ROUTER_COLS), lambda i: (0, 0)),
            pl.BlockSpec((1, ROUTER_COLS), lambda i: (0, 0)),
        ],
        out_specs=[
            pl.BlockSpec((tm, d), lambda i: (i, 0)),
            pl.BlockSpec((tm, ROUTER_COLS), lambda i: (i, 0)),
        ],
        out_shape=[
            jax.ShapeDtypeStruct((t, d), BF16),
            jax.ShapeDtypeStruct((t, ROUTER_COLS), F32),
        ],
        compiler_params=_params(("parallel",)),
        name="router",
    )(x, g.reshape(1, d), sh, sc, rw, rb)


def _experts_kernel(te_ref, tb_ref, tf_ref, nu_ref, x_ref, rw_ref, wg_ref, wu_ref, wd_ref, o_ref,
                    wg_s, wu_s, wd_s):
    i = pl.program_id(0)

    @pl.when(i < nu_ref[0])
    def _():
        @pl.when(tf_ref[i] == 1)
        def _():
            wg_s[...] = wg_ref[0].astype(BF16)
            wu_s[...] = wu_ref[0].astype(BF16)
            wd_s[...] = wd_ref[0].astype(BF16)

        xg = x_ref[...]
        hg = jnp.dot(xg, wg_s[...], preferred_element_type=F32)
        hu = jnp.dot(xg, wu_s[...], preferred_element_type=F32)
        hid = (jax.nn.silu(hg) * hu * rw_ref[...]).astype(BF16)
        o_ref[...] = jnp.dot(hid, wd_s[...], preferred_element_type=F32).astype(o_ref.dtype)


def _experts(xg, roww, tile_expert, tile_block, tile_first, n_used, w_gate, w_up, w_down, tme):
    p, d = xg.shape
    f = w_gate.shape[2]
    n_tiles = p // tme
    grid_spec = pltpu.PrefetchScalarGridSpec(
        num_scalar_prefetch=4,
        grid=(n_tiles,),
        in_specs=[
            pl.BlockSpec((tme, d), lambda i, te, tb, tf, nu: (tb[i], 0)),
            pl.BlockSpec((tme, 1), lambda i, te, tb, tf, nu: (tb[i], 0)),
            pl.BlockSpec((1, d, f), lambda i, te, tb, tf, nu: (te[i], 0, 0)),
            pl.BlockSpec((1, d, f), lambda i, te, tb, tf, nu: (te[i], 0, 0)),
            pl.BlockSpec((1, f, d), lambda i, te, tb, tf, nu: (te[i], 0, 0)),
        ],
        out_specs=pl.BlockSpec((tme, d), lambda i, te, tb, tf, nu: (tb[i], 0)),
        scratch_shapes=[
            pltpu.VMEM((d, f), BF16),
            pltpu.VMEM((d, f), BF16),
            pltpu.VMEM((f, d), BF16),
        ],
    )
    return pl.pallas_call(
        _experts_kernel,
        grid_spec=grid_spec,
        out_shape=jax.ShapeDtypeStruct((p, d), BF16),
        compiler_params=_params(("arbitrary",)),
        name="experts",
    )(tile_expert, tile_block, tile_first, n_used, xg, roww, w_gate, w_up, w_down)


def _route(logits, n_groups, per_group):
    lg = logits[:, :n_groups]
    le_all = logits[:, n_groups:n_groups + n_groups * per_group].reshape(-1, n_groups, per_group)
    grp = jnp.argmax(lg, axis=-1)
    p_grp = jnp.take_along_axis(jax.nn.softmax(lg, axis=-1), grp[:, None], axis=-1)
    le = jnp.take_along_axis(le_all, grp[:, None, None], axis=1)[:, 0]
    top_p, top_i = lax.top_k(jax.nn.softmax(le, axis=-1), TOP_K)
    w = top_p / jnp.sum(top_p, axis=-1, keepdims=True) * p_grp
    eid = grp[:, None] * per_group + top_i
    return eid.astype(jnp.int32), w


def _dispatch(eid, w, n_experts, tme):
    t = eid.shape[0]
    n_pairs = t * TOP_K
    p_rows = n_pairs + n_experts * tme
    p_rows = -(-p_rows // tme) * tme
    flat_e = eid.reshape(-1)
    onehot = (flat_e[:, None] == jnp.arange(n_experts, dtype=jnp.int32)[None, :]).astype(jnp.int32)
    csum = jnp.cumsum(onehot, axis=0)
    rank = jnp.take_along_axis(csum, flat_e[:, None], axis=1)[:, 0] - 1
    counts = csum[-1]
    padded = -(-counts // tme) * tme
    ends = jnp.cumsum(padded)
    starts = ends - padded
    dest = starts[flat_e] + rank
    src_tok = jnp.zeros((p_rows,), jnp.int32).at[dest].set(jnp.arange(n_pairs, dtype=jnp.int32) // TOP_K)
    roww = jnp.zeros((p_rows,), F32).at[dest].set(w.reshape(-1))
    n_used = (ends[-1] // tme).astype(jnp.int32)
    tile_ids = jnp.arange(p_rows // tme, dtype=jnp.int32)
    tile_block = jnp.minimum(tile_ids, n_used - 1)
    tile_expert = jnp.minimum(
        jnp.searchsorted(ends, tile_block * tme, side="right").astype(jnp.int32), n_experts - 1)
    tile_first = (tile_block * tme == starts[tile_expert]).astype(jnp.int32)
    return src_tok, roww.reshape(-1, 1), dest.reshape(t, TOP_K), tile_expert, tile_block, tile_first, n_used.reshape(1)


def _rmsnorm_kernel(x_ref, g_ref, o_ref):
    xf = x_ref[...]
    ms = jnp.mean(xf * xf, axis=-1, keepdims=True)
    o_ref[...] = xf * lax.rsqrt(ms + EPS) * g_ref[...]


def _final_norm(x, g, n_rows):
    d = x.shape[1]
    tr = _tile(n_rows, 256, 8)
    return pl.pallas_call(
        _rmsnorm_kernel,
        grid=(n_rows // tr,),
        in_specs=[pl.BlockSpec((tr, d), lambda i: (i, 0)), pl.BlockSpec((1, d), lambda i: (0, 0))],
        out_specs=pl.BlockSpec((tr, d), lambda i: (i, 0)),
        out_shape=jax.ShapeDtypeStruct((n_rows, d), F32),
        compiler_params=_params(("parallel",)),
        name="final_norm",
    )(x, g.reshape(1, d))


def _hyena_filters(length, w1, b1, w2, b2, w3, b3, w4, freq, d_hy):
    emb = w1.shape[0]
    t = jnp.linspace(0.0, 1.0, length, dtype=F32)[:, None]
    bands = (emb - 1) // 2
    f = jnp.linspace(1e-4, bands - 1, bands, dtype=F32)[None, :]
    w = 2.0 * math.pi * jnp.arange(length, dtype=F32)[:, None] / length
    z = jnp.concatenate([t, jnp.cos(-f * w), jnp.sin(-f * w)], axis=-1)
    h = jnp.sin(freq * (jnp.dot(z, w1, precision=HIGHEST) + b1))
    h = jnp.sin(freq * (jnp.dot(h, w2, precision=HIGHEST) + b2))
    h = jnp.sin(freq * (jnp.dot(h, w3, precision=HIGHEST) + b3))
    h = jnp.dot(h, w4, precision=HIGHEST).reshape(length, 2, HYENA_ORDER, d_hy)
    max_decay = math.log(HYENA_TARGET) / HYENA_FAST_PCT
    min_decay = math.log(HYENA_TARGET) / HYENA_SLOW_PCT
    deltas = jnp.abs(jnp.linspace(min_decay, max_decay, d_hy, dtype=F32))
    decay = jnp.exp(-t[:, :, None, None] * deltas)
    h = h * (decay + HYENA_SHIFT)
    return h / jnp.sum(jnp.abs(h), axis=(0, 1), keepdims=True)


def _bidir_long_conv(u, k_fwd, k_bwd, skip):
    length = u.shape[0]
    n = 2 * length
    k = jnp.concatenate([k_fwd, k_bwd[::-1]], axis=0)
    uf = jnp.fft.rfft(u, n=n, axis=0)
    kf = jnp.fft.rfft(k, n=n, axis=0)
    y = jnp.fft.irfft(uf * kf, n=n, axis=0)[:length]
    return y + u * skip


def _hyena_segment(p, filt, conv_w, conv_b, skip):
    pad = jnp.pad(p, ((1, 1), (0, 0)))
    p = pad[:-2] * conv_w[0] + pad[1:-1] * conv_w[1] + pad[2:] * conv_w[2] + conv_b
    v, x1, x2 = jnp.split(p, 3, axis=-1)
    z = x1 * _bidir_long_conv(v, filt[:, 0, 0], filt[:, 1, 0], skip[0])
    return x2 * _bidir_long_conv(z, filt[:, 0, 1], filt[:, 1, 1], skip[1])


def kernel(x, c, ctx, c_ctx, norm1_g, norm2_g, mod_a, mod_b, mod_bias, w_in, q_norm_g, k_norm_g, hy_conv_w, hy_conv_b, hy_w1, hy_b1, hy_w2, hy_b2, hy_w3, hy_b3, hy_w4, hy_freq, hy_skip, gm_ln_g, gm_ln_b, gm_ws, gm_bs, gate_up, gate_b, w_br, w_out, rt_group_w, rt_group_b, rt_expert_w, rt_expert_b, ex_w_gate, ex_w_up, ex_w_down, final_norm_g):
    bsz, seq, d = x.shape
    assert bsz == 1 and c.shape[0] == 1
    n_ctx = ctx.shape[1]
    depth = w_in.shape[0]
    t_all = seq + n_ctx
    d_hy = hy_skip.shape[2]
    d_gm = gm_ln_g.shape[1]
    d_at = w_br.shape[1] - d_hy - d_gm
    gate_rank = gate_up.shape[2]
    n_q = d_at // HEAD_DIM
    n_kv = n_q // KV_REP
    q_end = d_at
    v_start = q_end + n_kv * HEAD_DIM
    v_end = v_start + n_kv * HEAD_DIM
    hy_end = v_end + (HYENA_ORDER + 1) * d_hy
    gm_end = hy_end + 2 * d_gm
    n_groups = rt_group_w.shape[2]
    per_group = rt_expert_w.shape[3]
    n_experts = n_groups * per_group
    assert n_groups + n_experts <= ROUTER_COLS
    assert hy_end % (2 * d_gm) == 0 and gm_end % gate_rank == 0
    tme = 256

    xs = jnp.concatenate([x[0], ctx[0]], axis=0)
    cond = jnp.stack([c[0], c_ctx], axis=0)
    cos, sin = _rope_tables(seq, n_ctx)

    tq = _tile(seq, 256, 16)
    tk = _tile(t_all, 768, 256)
    tq_c = _tile(n_ctx, 256, 16)

    for l in range(depth):
        m = (jnp.dot(jnp.dot(jax.nn.silu(cond), mod_a[l], precision=HIGHEST), mod_b[l], precision=HIGHEST)
             + mod_bias[l]).reshape(2, N_MOD, d)

        px = _modproj(xs, norm1_g[l], m[:, 0], m[:, 1], w_in[l].astype(BF16), seq)
        q, k = _qkprep(px, cos, sin, q_norm_g[l], k_norm_g[l], n_q, n_kv)
        o_at = _attention(q, k, px, v_start // HEAD_DIM, 0, seq, 0, t_all, tq, tk)
        o_at = _attention(q, k, px, v_start // HEAD_DIM, seq, n_ctx, seq, n_ctx, tq_c, n_ctx, out=o_at)

        hy_w = (hy_w1[l], hy_b1[l], hy_w2[l], hy_b2[l], hy_w3[l], hy_b3[l], hy_w4[l], hy_freq[l])
        p_hy = px[:, v_end:hy_end].astype(F32)
        o_hy_x = _hyena_segment(p_hy[:seq], _hyena_filters(seq, *hy_w, d_hy), hy_conv_w[l], hy_conv_b[l], hy_skip[l])
        o_hy_c = _hyena_segment(p_hy[seq:], _hyena_filters(n_ctx, *hy_w, d_hy), hy_conv_w[l], hy_conv_b[l], hy_skip[l])
        o_hy = jnp.concatenate([o_hy_x, o_hy_c], axis=0).astype(BF16)

        o_gm = _gmlp(px, hy_end // (2 * d_gm), d_gm, gm_ln_g[l], gm_ln_b[l], gm_ws[l], gm_bs[l])

        merged = _merge(o_hy, o_at, o_gm, px, gm_end // gate_rank, gate_rank,
                        w_br[l].astype(BF16), gate_up[l].astype(BF16), gate_b[l])
        xs = _outproj(merged, w_out[l].astype(BF16), xs, m[:, 2], seq)

        rw = jnp.concatenate([rt_group_w[l]] + [rt_expert_w[l, g] for g in range(n_groups)], axis=1)
        rw = jnp.pad(rw, ((0, 0), (0, ROUTER_COLS - rw.shape[1])))
        rb = jnp.concatenate([rt_group_b[l], rt_expert_b[l].reshape(-1)])
        rb = jnp.pad(rb, (0, ROUTER_COLS - rb.shape[0])).reshape(1, ROUTER_COLS)
        h2, logits = _router(xs, norm2_g[l], m[:, 3], m[:, 4], rw, rb, seq)
        eid, wts = _route(logits, n_groups, per_group)
        src_tok, roww, pos, tile_expert, tile_block, tile_first, n_used = _dispatch(eid, wts, n_experts, tme)
        xg = jnp.take(h2, src_tok, axis=0)
        yo = _experts(xg, roww, tile_expert, tile_block, tile_first, n_used,
                      ex_w_gate[l], ex_w_up[l], ex_w_down[l], tme)
        moe = jnp.take(yo, pos[:, 0], axis=0).astype(F32) + jnp.take(yo, pos[:, 1], axis=0).astype(F32)
        gate = jnp.concatenate([jnp.broadcast_to(m[0:1, 5], (seq, d)), jnp.broadcast_to(m[1:2, 5], (n_ctx, d))], axis=0)
        xs = xs + gate * moe

    return _final_norm(xs, final_norm_g, seq).reshape(1, seq, d)
```

```python
import functools
import math

import numpy as np
import jax
import jax.numpy as jnp
from jax import lax
from jax.experimental import pallas as pl
from jax.experimental.pallas import tpu as pltpu

F32 = jnp.float32
BF16 = jnp.bfloat16
HIGHEST = lax.Precision.HIGHEST

HEAD_DIM = 128
KV_REP = 4
GRID_W = 64
ROPE_THETA = 10000.0
ATTN_SCALE = HEAD_DIM ** -0.5
HYENA_ORDER = 2
HYENA_SHIFT = 0.05
HYENA_FAST_PCT = 0.3
HYENA_SLOW_PCT = 1.5
HYENA_TARGET = 1e-2
GMLP_CHUNK = 128
GMLP_GROUP_W = 128
N_BRANCH = 3
N_MOD = 6
TOP_K = 2
EPS = 1e-6
LN_EPS = 1e-5

LANES = 128
ROUTER_COLS = LANES
VMEM_LIMIT_MB = 56


def _params(semantics, vmem_mb=VMEM_LIMIT_MB):
    return pltpu.CompilerParams(dimension_semantics=semantics, vmem_limit_bytes=vmem_mb * 1024 * 1024)


def _tile(n, target, mult):
    best = None
    for d in range(mult, min(n, target) + 1, mult):
        if n % d == 0:
            best = d
    assert best is not None, (n, target, mult)
    return best


ROW_CHUNK = 16


def _modulate_rows(x_ref, g_ref, sh_ref, sc_ref, h_ref, row0, n_lat, n_rows):
    def body(r, carry):
        rows = pl.ds(pl.multiple_of(r * ROW_CHUNK, ROW_CHUNK), ROW_CHUNK)
        xf = x_ref[rows, :]
        ms = jnp.mean(xf * xf, axis=-1, keepdims=True)
        y = xf * lax.rsqrt(ms + EPS) * g_ref[...]
        rid = row0 + r * ROW_CHUNK + lax.broadcasted_iota(jnp.int32, (ROW_CHUNK, 1), 0)
        is_ctx = rid >= n_lat
        sc = jnp.where(is_ctx, sc_ref[1:2, :], sc_ref[0:1, :])
        sh = jnp.where(is_ctx, sh_ref[1:2, :], sh_ref[0:1, :])
        h_ref[rows, :] = (y * (1.0 + sc) + sh).astype(h_ref.dtype)
        return carry

    lax.fori_loop(0, n_rows // ROW_CHUNK, body, 0)


def _modproj_kernel(x_ref, g_ref, sh_ref, sc_ref, w_ref, o_ref, h_ref, *, n_lat, tm):
    i = pl.program_id(0)

    @pl.when(pl.program_id(1) == 0)
    def _():
        _modulate_rows(x_ref, g_ref, sh_ref, sc_ref, h_ref, i * tm, n_lat, tm)

    o_ref[...] = jnp.dot(h_ref[...], w_ref[...], preferred_element_type=F32).astype(o_ref.dtype)


def _modproj(x, g, sh, sc, w, n_lat):
    t, d = x.shape
    n = w.shape[1]
    tm = _tile(t, 528, 16)
    tn = _tile(n, 512, LANES)
    return pl.pallas_call(
        functools.partial(_modproj_kernel, n_lat=n_lat, tm=tm),
        grid=(t // tm, n // tn),
        in_specs=[
            pl.BlockSpec((tm, d), lambda i, j: (i, 0)),
            pl.BlockSpec((1, d), lambda i, j: (0, 0)),
            pl.BlockSpec((2, d), lambda i, j: (0, 0)),
            pl.BlockSpec((2, d), lambda i, j: (0, 0)),
            pl.BlockSpec((d, tn), lambda i, j: (0, j)),
        ],
        out_specs=pl.BlockSpec((tm, tn), lambda i, j: (i, j)),
        out_shape=jax.ShapeDtypeStruct((t, n), BF16),
        scratch_shapes=[pltpu.VMEM((tm, d), BF16)],
        compiler_params=_params(("parallel", "arbitrary")),
        name="modproj",
    )(x, g.reshape(1, d), sh, sc, w)


def _qkprep_kernel(p_ref, cos_ref, sin_ref, qg_ref, kg_ref, q_ref, k_ref, *, n_q, n_kv):
    cos = cos_ref[...]
    sin = sin_ref[...]
    lane = lax.broadcasted_iota(jnp.int32, (1, HEAD_DIM), 1)
    first_half = (lane % (HEAD_DIM // 2)) < (HEAD_DIM // 4)

    def head(h, gain, scale):
        xh = p_ref[:, h * HEAD_DIM:(h + 1) * HEAD_DIM].astype(F32)
        ms = jnp.mean(xh * xh, axis=-1, keepdims=True)
        y = xh * lax.rsqrt(ms + EPS) * gain
        partner = jnp.where(first_half,
                            pltpu.roll(y, HEAD_DIM - HEAD_DIM // 4, 1),
                            pltpu.roll(y, HEAD_DIM // 4, 1))
        return (y * cos + partner * sin) * scale

    for h in range(n_q):
        q_ref[h] = head(h, qg_ref[...], ATTN_SCALE).astype(q_ref.dtype)
    for h in range(n_kv):
        k_ref[h] = head(n_q + h, kg_ref[...], 1.0).astype(k_ref.dtype)


def _qkprep(px, cos, sin, qg, kg, n_q, n_kv):
    t = px.shape[0]
    tr = _tile(t, 264, 8)
    width = (n_q + n_kv) * HEAD_DIM
    return pl.pallas_call(
        functools.partial(_qkprep_kernel, n_q=n_q, n_kv=n_kv),
        grid=(t // tr,),
        in_specs=[
            pl.BlockSpec((tr, width), lambda i: (i, 0)),
            pl.BlockSpec((tr, HEAD_DIM), lambda i: (i, 0)),
            pl.BlockSpec((tr, HEAD_DIM), lambda i: (i, 0)),
            pl.BlockSpec((1, HEAD_DIM), lambda i: (0, 0)),
            pl.BlockSpec((1, HEAD_DIM), lambda i: (0, 0)),
        ],
        out_specs=[
            pl.BlockSpec((n_q, tr, HEAD_DIM), lambda i: (0, i, 0)),
            pl.BlockSpec((n_kv, tr, HEAD_DIM), lambda i: (0, i, 0)),
        ],
        out_shape=[
            jax.ShapeDtypeStruct((n_q, t, HEAD_DIM), BF16),
            jax.ShapeDtypeStruct((n_kv, t, HEAD_DIM), BF16),
        ],
        compiler_params=_params(("parallel",)),
        name="qkprep",
    )(px, cos, sin, qg.reshape(1, HEAD_DIM), kg.reshape(1, HEAD_DIM))


def _rope_tables(n_lat, n_ctx):
    n = HEAD_DIM // 4
    inv = ROPE_THETA ** (-np.arange(n, dtype=np.float32) / n)
    t = np.arange(n_lat)
    rows = (t // GRID_W).astype(np.float32)
    cols = (t % GRID_W).astype(np.float32)
    ang = np.concatenate([rows[:, None] * inv[None, :]] * 2 + [cols[:, None] * inv[None, :]] * 2, axis=1)
    ang = jnp.asarray(ang.astype(np.float32))
    sign = np.tile(np.concatenate([-np.ones(n, np.float32), np.ones(n, np.float32)]), 2)
    cos = jnp.concatenate([jnp.cos(ang), jnp.ones((n_ctx, HEAD_DIM), F32)], axis=0)
    sin = jnp.concatenate([jnp.sin(ang) * sign[None, :], jnp.zeros((n_ctx, HEAD_DIM), F32)], axis=0)
    return cos, sin


def _attn_kernel(*refs, tq, tk, n_kv, aliased):
    if aliased:
        q_ref, k_ref, v_ref, _, o_ref, s_ref, m_ref, l_ref, acc_ref = refs
    else:
        q_ref, k_ref, v_ref, o_ref, s_ref, m_ref, l_ref, acc_ref = refs
    n_chunks = n_kv // tk
    m_ref[...] = jnp.full(m_ref.shape, -jnp.inf, F32)
    l_ref[...] = jnp.zeros(l_ref.shape, F32)
    acc_ref[...] = jnp.zeros(acc_ref.shape, F32)

    def scores(r, ks):
        return lax.dot_general(q_ref[r], k_ref[0, ks, :], (((1,), (1,)), ((), ())), preferred_element_type=F32)

    s_ref[0] = scores(0, pl.ds(0, tk))

    def body(c, carry):
        ks = pl.ds(pl.multiple_of(c * tk, tk), tk)
        kn = pl.ds(pl.multiple_of(jnp.minimum(c + 1, n_chunks - 1) * tk, tk), tk)
        v = v_ref[ks, :]
        for r in range(KV_REP):
            cur, nxt = r % 2, (r + 1) % 2
            s_ref[nxt] = scores(r + 1, ks) if r + 1 < KV_REP else scores(0, kn)
            s = s_ref[cur]
            m_prev = m_ref[r]
            m_new = jnp.maximum(m_prev, jnp.max(s, axis=-1, keepdims=True))
            alpha = jnp.exp(m_prev - m_new)
            p = jnp.exp(s - m_new)
            l_ref[r] = alpha * l_ref[r] + jnp.sum(p, axis=-1, keepdims=True)
            acc_ref[r] = alpha * acc_ref[r] + jnp.dot(p.astype(BF16), v, preferred_element_type=F32)
            m_ref[r] = m_new
        return carry

    lax.fori_loop(0, n_chunks, body, 0)
    for r in range(KV_REP):
        o_ref[:, r * HEAD_DIM:(r + 1) * HEAD_DIM] = (acc_ref[r] / l_ref[r]).astype(o_ref.dtype)


def _attention(q, k, px, v_col_block, q_row0, n_q_rows, kv_row0, n_kv, tq, tk, out=None):
    n_heads, t, _ = q.shape
    n_kv_heads = k.shape[0]
    assert q_row0 % tq == 0 and n_q_rows % tq == 0 and kv_row0 % n_kv == 0 and n_kv % tk == 0
    qb0 = q_row0 // tq
    kb0 = kv_row0 // n_kv
    aliased = out is not None
    in_specs = [
        pl.BlockSpec((KV_REP, tq, HEAD_DIM), lambda g, i: (g, qb0 + i, 0)),
        pl.BlockSpec((1, n_kv, HEAD_DIM), lambda g, i: (g, kb0, 0)),
        pl.BlockSpec((n_kv, HEAD_DIM), lambda g, i: (kb0, v_col_block + g)),
    ]
    args = [q, k, px]
    if aliased:
        in_specs.append(pl.BlockSpec(memory_space=pl.ANY))
        args.append(out)
    return pl.pallas_call(
        functools.partial(_attn_kernel, tq=tq, tk=tk, n_kv=n_kv, aliased=aliased),
        grid=(n_kv_heads, n_q_rows // tq),
        in_specs=in_specs,
        out_specs=pl.BlockSpec((tq, KV_REP * HEAD_DIM), lambda g, i: (qb0 + i, g)),
        out_shape=jax.ShapeDtypeStruct((t, n_heads * HEAD_DIM), BF16),
        scratch_shapes=[
            pltpu.VMEM((2, tq, tk), F32),
            pltpu.VMEM((KV_REP, tq, 1), F32),
            pltpu.VMEM((KV_REP, tq, 1), F32),
            pltpu.VMEM((KV_REP, tq, HEAD_DIM), F32),
        ],
        input_output_aliases={3: 0} if aliased else {},
        compiler_params=_params(("parallel", "arbitrary")),
        name="attention",
    )(*args)


def _gmlp_kernel(p_ref, lng_ref, lnb_ref, ws_ref, bst_ref, o_ref, *, n_chunks, d_g):
    n_groups = d_g // GMLP_GROUP_W
    for b in range(n_chunks):
        rows = slice(b * GMLP_CHUNK, (b + 1) * GMLP_CHUNK)
        uv = jax.nn.gelu(p_ref[rows, :].astype(F32))
        u = uv[:, :d_g]
        v = uv[:, d_g:]
        vc = v - jnp.mean(v, axis=-1, keepdims=True)
        var = jnp.mean(vc * vc, axis=-1, keepdims=True)
        vn = (vc * lax.rsqrt(var + LN_EPS) * lng_ref[...] + lnb_ref[...]).astype(BF16)
        for g in range(n_groups):
            cols = slice(g * GMLP_GROUP_W, (g + 1) * GMLP_GROUP_W)
            vm = jnp.dot(ws_ref[g].astype(BF16), vn[:, cols], preferred_element_type=F32) + bst_ref[:, g:g + 1]
            o_ref[rows, cols] = (u[:, cols] * vm).astype(o_ref.dtype)


def _gmlp(px, col_block, d_g, ln_g, ln_b, ws, bs):
    t = px.shape[0]
    n_chunks = 2
    tr = n_chunks * GMLP_CHUNK
    assert t % tr == 0
    n_groups = d_g // GMLP_GROUP_W
    return pl.pallas_call(
        functools.partial(_gmlp_kernel, n_chunks=n_chunks, d_g=d_g),
        grid=(t // tr,),
        in_specs=[
            pl.BlockSpec((tr, 2 * d_g), lambda i: (i, col_block)),
            pl.BlockSpec((1, d_g), lambda i: (0, 0)),
            pl.BlockSpec((1, d_g), lambda i: (0, 0)),
            pl.BlockSpec((n_groups, GMLP_CHUNK, GMLP_CHUNK), lambda i: (0, 0, 0)),
            pl.BlockSpec((GMLP_CHUNK, n_groups), lambda i: (0, 0)),
        ],
        out_specs=pl.BlockSpec((tr, d_g), lambda i: (i, 0)),
        out_shape=jax.ShapeDtypeStruct((t, d_g), BF16),
        compiler_params=_params(("parallel",)),
        name="gmlp",
    )(px, ln_g.reshape(1, d_g), ln_b.reshape(1, d_g), ws, bs.T)


def _merge_kernel(hy_ref, at_ref, gm_ref, gc_ref, wbr_ref, gup_ref, gb_ref, o_ref, *, d_hy, d_at):
    gc = gc_ref[...]
    bounds = (0, d_hy, d_hy + d_at, wbr_ref.shape[0])
    merged = None
    for b, a_ref in enumerate((hy_ref, at_ref, gm_ref)):
        y = jnp.dot(a_ref[...], wbr_ref[bounds[b]:bounds[b + 1], :], preferred_element_type=F32)
        gate = jax.nn.sigmoid(jnp.dot(gc, gup_ref[b], preferred_element_type=F32) + gb_ref[b:b + 1, :])
        merged = gate * y if merged is None else merged + gate * y
    o_ref[...] = merged.astype(o_ref.dtype)


def _merge(o_hy, o_at, o_gm, px, gc_col_block, gate_rank, w_br, gate_up, gate_b):
    t = px.shape[0]
    d_hy, d_at, d_gm = o_hy.shape[1], o_at.shape[1], o_gm.shape[1]
    d = w_br.shape[1]
    tm = _tile(t, 528, 16)
    tn = _tile(d, 512, LANES)
    return pl.pallas_call(
        functools.partial(_merge_kernel, d_hy=d_hy, d_at=d_at),
        grid=(t // tm, d // tn),
        in_specs=[
            pl.BlockSpec((tm, d_hy), lambda i, j: (i, 0)),
            pl.BlockSpec((tm, d_at), lambda i, j: (i, 0)),
            pl.BlockSpec((tm, d_gm), lambda i, j: (i, 0)),
            pl.BlockSpec((tm, gate_rank), lambda i, j: (i, gc_col_block)),
            pl.BlockSpec((d_hy + d_at + d_gm, tn), lambda i, j: (0, j)),
            pl.BlockSpec((N_BRANCH, gate_rank, tn), lambda i, j: (0, 0, j)),
            pl.BlockSpec((N_BRANCH, tn), lambda i, j: (0, j)),
        ],
        out_specs=pl.BlockSpec((tm, tn), lambda i, j: (i, j)),
        out_shape=jax.ShapeDtypeStruct((t, d), BF16),
        compiler_params=_params(("parallel", "arbitrary")),
        name="merge",
    )(o_hy, o_at, o_gm, px, w_br, gate_up, gate_b)


def _outproj_kernel(a_ref, w_ref, x_ref, gate_ref, o_ref, *, n_lat, tm):
    y = jnp.dot(a_ref[...], w_ref[...], preferred_element_type=F32)
    rid = pl.program_id(0) * tm + lax.broadcasted_iota(jnp.int32, (tm, 1), 0)
    gate = jnp.where(rid >= n_lat, gate_ref[1:2, :], gate_ref[0:1, :])
    o_ref[...] = x_ref[...] + gate * y


def _outproj(a, w, x, gate, n_lat):
    t, k = a.shape
    d = w.shape[1]
    tm = _tile(t, 528, 16)
    tn = _tile(d, 512, LANES)
    return pl.pallas_call(
        functools.partial(_outproj_kernel, n_lat=n_lat, tm=tm),
        grid=(t // tm, d // tn),
        in_specs=[
            pl.BlockSpec((tm, k), lambda i, j: (i, 0)),
            pl.BlockSpec((k, tn), lambda i, j: (0, j)),
            pl.BlockSpec((tm, tn), lambda i, j: (i, j)),
            pl.BlockSpec((2, tn), lambda i, j: (0, j)),
        ],
        out_specs=pl.BlockSpec((tm, tn), lambda i, j: (i, j)),
        out_shape=jax.ShapeDtypeStruct((t, d), F32),
        compiler_params=_params(("parallel", "arbitrary")),
        name="outproj",
    )(a, w, x, gate)


def _router_kernel(x_ref, g_ref, sh_ref, sc_ref, rw_ref, rb_ref, h_ref, lg_ref, *, n_lat, tm):
    _modulate_rows(x_ref, g_ref, sh_ref, sc_ref, h_ref, pl.program_id(0) * tm, n_lat, tm)
    lg_ref[...] = jnp.dot(h_ref[...].astype(F32), rw_ref[...], preferred_element_type=F32,
                          precision=HIGHEST) + rb_ref[...]


def _router(x, g, sh, sc, rw, rb, n_lat):
    t, d = x.shape
    tm = _tile(t, 264, 16)
    return pl.pallas_call(
        functools.partial(_router_kernel, n_lat=n_lat, tm=tm),
        grid=(t // tm,),
        in_specs=[
            pl.BlockSpec((tm, d), lambda i: (i, 0)),
            pl.BlockSpec((1, d), lambda i: (0, 0)),
            pl.BlockSpec((2, d), lambda i: (0, 0)),
            pl.BlockSpec((2, d), lambda i: (0, 0)),
            pl.BlockSpec((d, ROUTER_COLS), lambda i: (0, 0)),
            pl.BlockSpec((1, ROUTER_COLS), lambda i: (0, 0)),
        ],
        out_specs=[
            pl.BlockSpec((tm, d), lambda i: (i, 0)),
            pl.BlockSpec((tm, ROUTER_COLS), lambda i: (i, 0)),
        ],
        out_shape=[
            jax.ShapeDtypeStruct((t, d), BF16),
            jax.ShapeDtypeStruct((t, ROUTER_COLS), F32),
        ],
        compiler_params=_params(("parallel",)),
        name="router",
    )(x, g.reshape(1, d), sh, sc, rw, rb)


def _experts_kernel(te_ref, tb_ref, tf_ref, nu_ref, x_ref, rw_ref, wg_ref, wu_ref, wd_ref, o_ref,
                    wg_s, wu_s, wd_s):
    i = pl.program_id(0)

    @pl.when(i < nu_ref[0])
    def _():
        @pl.when(tf_ref[i] == 1)
        def _():
            wg_s[...] = wg_ref[0].astype(BF16)
            wu_s[...] = wu_ref[0].astype(BF16)
            wd_s[...] = wd_ref[0].astype(BF16)

        xg = x_ref[...]
        hg = jnp.dot(xg, wg_s[...], preferred_element_type=F32)
        hu = jnp.dot(xg, wu_s[...], preferred_element_type=F32)
        hid = (jax.nn.silu(hg) * hu * rw_ref[...]).astype(BF16)
        o_ref[...] = jnp.dot(hid, wd_s[...], preferred_element_type=F32).astype(o_ref.dtype)


def _experts(xg, roww, tile_expert, tile_block, tile_first, n_used, w_gate, w_up, w_down, tme):
    p, d = xg.shape
    f = w_gate.shape[2]
    n_tiles = p // tme
    grid_spec = pltpu.PrefetchScalarGridSpec(
        num_scalar_prefetch=4,
        grid=(n_tiles,),
        in_specs=[
            pl.BlockSpec((tme, d), lambda i, te, tb, tf, nu: (tb[i], 0)),
            pl.BlockSpec((tme, 1), lambda i, te, tb, tf, nu: (tb[i], 0)),
            pl.BlockSpec((1, d, f), lambda i, te, tb, tf, nu: (te[i], 0, 0)),
            pl.BlockSpec((1, d, f), lambda i, te, tb, tf, nu: (te[i], 0, 0)),
            pl.BlockSpec((1, f, d), lambda i, te, tb, tf, nu: (te[i], 0, 0)),
        ],
        out_specs=pl.BlockSpec((tme, d), lambda i, te, tb, tf, nu: (tb[i], 0)),
        scratch_shapes=[
            pltpu.VMEM((d, f), BF16),
            pltpu.VMEM((d, f), BF16),
            pltpu.VMEM((f, d), BF16),
        ],
    )
    return pl.pallas_call(
        _experts_kernel,
        grid_spec=grid_spec,
        out_shape=jax.ShapeDtypeStruct((p, d), BF16),
        compiler_params=_params(("arbitrary",)),
        name="experts",
    )(tile_expert, tile_block, tile_first, n_used, xg, roww, w_gate, w_up, w_down)


def _route(logits, n_groups, per_group):
    lg = logits[:, :n_groups]
    le_all = logits[:, n_groups:n_groups + n_groups * per_group].reshape(-1, n_groups, per_group)
    grp = jnp.argmax(lg, axis=-1)
    p_grp = jnp.take_along_axis(jax.nn.softmax(lg, axis=-1), grp[:, None], axis=-1)
    le = jnp.take_along_axis(le_all, grp[:, None, None], axis=1)[:, 0]
    top_p, top_i = lax.top_k(jax.nn.softmax(le, axis=-1), TOP_K)
    w = top_p / jnp.sum(top_p, axis=-1, keepdims=True) * p_grp
    eid = grp[:, None] * per_group + top_i
    return eid.astype(jnp.int32), w


def _dispatch(eid, w, n_experts, tme):
    t = eid.shape[0]
    n_pairs = t * TOP_K
    p_rows = n_pairs + n_experts * tme
    p_rows = -(-p_rows // tme) * tme
    flat_e = eid.reshape(-1)
    onehot = (flat_e[:, None] == jnp.arange(n_experts, dtype=jnp.int32)[None, :]).astype(jnp.int32)
    csum = jnp.cumsum(onehot, axis=0)
    rank = jnp.take_along_axis(csum, flat_e[:, None], axis=1)[:, 0] - 1
    counts = csum[-1]
    padded = -(-counts // tme) * tme
    ends = jnp.cumsum(padded)
    starts = ends - padded
    dest = starts[flat_e] + rank
    src_tok = jnp.zeros((p_rows,), jnp.int32).at[dest].set(jnp.arange(n_pairs, dtype=jnp.int32) // TOP_K)
    roww = jnp.zeros((p_rows,), F32).at[dest].set(w.reshape(-1))
    n_used = (ends[-1] // tme).astype(jnp.int32)
    tile_ids = jnp.arange(p_rows // tme, dtype=jnp.int32)
    tile_block = jnp.minimum(tile_ids, n_used - 1)
    tile_expert = jnp.minimum(
        jnp.searchsorted(ends, tile_block * tme, side="right").astype(jnp.int32), n_experts - 1)
    tile_first = (tile_block * tme == starts[tile_expert]).astype(jnp.int32)
    return src_tok, roww.reshape(-1, 1), dest.reshape(t, TOP_K), tile_expert, tile_block, tile_first, n_used.reshape(1)


def _rmsnorm_kernel(x_ref, g_ref, o_ref):
    xf = x_ref[...]
    ms = jnp.mean(xf * xf, axis=-1, keepdims=True)
    o_ref[...] = xf * lax.rsqrt(ms + EPS) * g_ref[...]


def _final_norm(x, g, n_rows):
    d = x.shape[1]
    tr = _tile(n_rows, 256, 8)
    return pl.pallas_call(
        _rmsnorm_kernel,
        grid=(n_rows // tr,),
        in_specs=[pl.BlockSpec((tr, d), lambda i: (i, 0)), pl.BlockSpec((1, d), lambda i: (0, 0))],
        out_specs=pl.BlockSpec((tr, d), lambda i: (i, 0)),
        out_shape=jax.ShapeDtypeStruct((n_rows, d), F32),
        compiler_params=_params(("parallel",)),
        name="final_norm",
    )(x, g.reshape(1, d))


def _hyena_filters(length, w1, b1, w2, b2, w3, b3, w4, freq, d_hy):
    emb = w1.shape[0]
    t = jnp.linspace(0.0, 1.0, length, dtype=F32)[:, None]
    bands = (emb - 1) // 2
    f = jnp.linspace(1e-4, bands - 1, bands, dtype=F32)[None, :]
    w = 2.0 * math.pi * jnp.arange(length, dtype=F32)[:, None] / length
    z = jnp.concatenate([t, jnp.cos(-f * w), jnp.sin(-f * w)], axis=-1)
    h = jnp.sin(freq * (jnp.dot(z, w1, precision=HIGHEST) + b1))
    h = jnp.sin(freq * (jnp.dot(h, w2, precision=HIGHEST) + b2))
    h = jnp.sin(freq * (jnp.dot(h, w3, precision=HIGHEST) + b3))
    h = jnp.dot(h, w4, precision=HIGHEST).reshape(length, 2, HYENA_ORDER, d_hy)
    max_decay = math.log(HYENA_TARGET) / HYENA_FAST_PCT
    min_decay = math.log(HYENA_TARGET) / HYENA_SLOW_PCT
    deltas = jnp.abs(jnp.linspace(min_decay, max_decay, d_hy, dtype=F32))
    decay = jnp.exp(-t[:, :, None, None] * deltas)
    h = h * (decay + HYENA_SHIFT)
    return h / jnp.sum(jnp.abs(h), axis=(0, 1), keepdims=True)


CONV_ROWS = 64
CONV_HALO = 8


def _shortconv_kernel(p_ref, w_ref, b_ref, lat_ref, ctx_ref, pad_ref, *, n_lat, n_ctx):
    t = n_lat + n_ctx
    tc = p_ref.shape[1]
    pad_ref[0:CONV_HALO, :] = jnp.zeros((CONV_HALO, tc), F32)
    pad_ref[t + CONV_HALO:t + 2 * CONV_HALO, :] = jnp.zeros((CONV_HALO, tc), F32)

    def fill(r, carry):
        src = pl.ds(pl.multiple_of(r * CONV_ROWS, CONV_ROWS), CONV_ROWS)
        dst = pl.ds(pl.multiple_of(r * CONV_ROWS + CONV_HALO, CONV_HALO), CONV_ROWS)
        pad_ref[dst, :] = p_ref[src, :].astype(F32)
        return carry

    lax.fori_loop(0, t // CONV_ROWS, fill, 0)
    w0, w1, w2, bias = w_ref[0:1, :], w_ref[1:2, :], w_ref[2:3, :], b_ref[...]
    ext_rows = CONV_ROWS + 2 * CONV_HALO
    inner = slice(CONV_HALO, CONV_HALO + CONV_ROWS)

    def segment(out_ref, row0, n_rows):
        def body(r, carry):
            ext = pad_ref[pl.ds(pl.multiple_of(row0 + r * CONV_ROWS, CONV_HALO), ext_rows), :]
            rid = r * CONV_ROWS + lax.broadcasted_iota(jnp.int32, (CONV_ROWS, 1), 0)
            prev = jnp.where(rid == 0, 0.0, pltpu.roll(ext, 1, 0)[inner])
            nxt = jnp.where(rid == n_rows - 1, 0.0, pltpu.roll(ext, ext_rows - 1, 0)[inner])
            y = prev * w0 + ext[inner] * w1 + nxt * w2 + bias
            out_ref[0, pl.ds(pl.multiple_of(r * CONV_ROWS, CONV_ROWS), CONV_ROWS), :] = y.astype(out_ref.dtype)
            return carry

        lax.fori_loop(0, n_rows // CONV_ROWS, body, 0)

    segment(lat_ref, 0, n_lat)
    segment(ctx_ref, n_lat, n_ctx)


def _shortconv(px, col0, d_hy, conv_w, conv_b, n_lat, n_ctx):
    t = px.shape[0]
    tc = _tile(d_hy, 256, LANES)
    assert col0 % tc == 0 and n_lat % CONV_ROWS == 0 and n_ctx % CONV_ROWS == 0
    per = d_hy // tc
    n_grp = HYENA_ORDER + 1
    return pl.pallas_call(
        functools.partial(_shortconv_kernel, n_lat=n_lat, n_ctx=n_ctx),
        grid=(n_grp, per),
        in_specs=[
            pl.BlockSpec((t, tc), lambda g, j: (0, col0 // tc + g * per + j)),
            pl.BlockSpec((3, tc), lambda g, j: (0, g * per + j)),
            pl.BlockSpec((1, tc), lambda g, j: (0, g * per + j)),
        ],
        out_specs=[
            pl.BlockSpec((1, n_lat, tc), lambda g, j: (g, 0, j)),
            pl.BlockSpec((1, n_ctx, tc), lambda g, j: (g, 0, j)),
        ],
        out_shape=[
            jax.ShapeDtypeStruct((n_grp, n_lat, d_hy), BF16),
            jax.ShapeDtypeStruct((n_grp, n_ctx, d_hy), BF16),
        ],
        scratch_shapes=[pltpu.VMEM((t + 2 * CONV_HALO, tc), F32)],
        compiler_params=_params(("parallel", "parallel")),
        name="shortconv",
    )(px, conv_w, conv_b.reshape(1, -1))


DFT_N2 = 128
K1_ALIGN = 16


def _dft_constants(length):
    n = 2 * length
    n1 = n // DFT_N2
    assert n1 * DFT_N2 == n and n1 % 2 == 0
    n_t1 = n1 // 2
    k1_used = n1 // 2 + 1
    k1p = -(-k1_used // K1_ALIGN) * K1_ALIGN
    k1 = np.arange(k1p, dtype=np.float64)
    live = (k1 < k1_used).astype(np.float64)
    t1 = np.arange(n1, dtype=np.float64)
    ang1 = -2.0 * np.pi * k1[:, None] * t1[None, :] / n1
    f1 = np.stack([np.cos(ang1), np.sin(ang1)]) * live[None, :, None]
    k2 = np.arange(DFT_N2, dtype=np.float64)
    t2 = np.arange(DFT_N2, dtype=np.float64)
    ang2 = -2.0 * np.pi * t2[None, None, :] * (k1[:, None, None] + n1 * k2[None, :, None]) / n
    gr = np.cos(ang2) * live[:, None, None]
    gi = np.sin(ang2) * live[:, None, None]
    g = np.stack([gr, gi, gr.transpose(0, 2, 1), gi.transpose(0, 2, 1)])
    weight = np.where((k1 == 0) | (k1 == n1 // 2), 1.0, 2.0) * live
    angc = 2.0 * np.pi * np.arange(n_t1, dtype=np.float64)[:, None] * k1[None, :] / n1
    cinv = np.stack([np.cos(angc), np.sin(angc)]) * weight[None, None, :] / n
    as_bf16 = lambda a: jnp.asarray(a.astype(np.float32)).astype(BF16)
    return dict(n1=n1, n_t1=n_t1, k1p=k1p, f1=as_bf16(f1), f1_sig=as_bf16(f1[:, :, :n_t1]),
                g=as_bf16(g), cinv=as_bf16(cinv))


def _dft_stage1_kernel(f_ref, u_ref, a_ref):
    u = u_ref[...]
    a_ref[0] = jnp.dot(f_ref[0], u, preferred_element_type=F32).astype(a_ref.dtype)
    a_ref[1] = jnp.dot(f_ref[1], u, preferred_element_type=F32).astype(a_ref.dtype)


def _dft_stage1(f, u3, group):
    _, k, w = u3.shape
    k1p = f.shape[1]
    nc = _tile(w, 4096, LANES)
    return pl.pallas_call(
        _dft_stage1_kernel,
        grid=(w // nc,),
        in_specs=[
            pl.BlockSpec((2, k1p, k), lambda j: (0, 0, 0)),
            pl.BlockSpec((None, k, nc), lambda j: (group, 0, j)),
        ],
        out_specs=pl.BlockSpec((2, k1p, nc), lambda j: (0, 0, j)),
        out_shape=jax.ShapeDtypeStruct((2, k1p, w), BF16),
        compiler_params=_params(("parallel",)),
        name="dft_stage1",
    )(f, u3)


def _dft_stage2_filter_kernel(a_ref, g_ref, kf_ref, *, k1c):
    def body(i, carry):
        ar, ai = a_ref[0, i], a_ref[1, i]
        gr, gi = g_ref[0, i], g_ref[1, i]
        dot = functools.partial(jnp.dot, preferred_element_type=F32)
        kf_ref[0, i] = dot(gr, ar) - dot(gi, ai)
        kf_ref[1, i] = dot(gr, ai) + dot(gi, ar)
        return carry

    lax.fori_loop(0, k1c, body, 0)


def _dft_stage2_conv_kernel(a_ref, g_ref, kf_ref, b_ref, *, k1c):
    def body(i, carry):
        ar, ai = a_ref[0, i], a_ref[1, i]
        gr, gi = g_ref[0, i], g_ref[1, i]
        dot = functools.partial(jnp.dot, preferred_element_type=F32)
        xr = dot(gr, ar) - dot(gi, ai)
        xi = dot(gr, ai) + dot(gi, ar)
        kr, ki = kf_ref[0, i], kf_ref[1, i]
        zr = (xr * kr - xi * ki).astype(BF16)
        zi = (xr * ki + xi * kr).astype(BF16)
        grt, git = g_ref[2, i], g_ref[3, i]
        b_ref[0, i] = (dot(grt, zr) + dot(git, zi)).astype(b_ref.dtype)
        b_ref[1, i] = (dot(grt, zi) - dot(git, zr)).astype(b_ref.dtype)
        return carry

    lax.fori_loop(0, k1c, body, 0)


def _dft_stage2(a4, g, kf=None):
    _, k1p, n2, c = a4.shape
    k1c = K1_ALIGN
    tc = _tile(c, 256, LANES)
    a_spec = pl.BlockSpec((2, k1c, n2, tc), lambda i, j: (0, i, 0, j))
    g_spec = pl.BlockSpec((4, k1c, n2, n2), lambda i, j: (0, i, 0, 0))
    if kf is None:
        body, args, in_specs, dtype = _dft_stage2_filter_kernel, (a4, g), [a_spec, g_spec], F32
    else:
        body, args, in_specs, dtype = _dft_stage2_conv_kernel, (a4, g, kf), [a_spec, g_spec, a_spec], BF16
    return pl.pallas_call(
        functools.partial(body, k1c=k1c),
        grid=(k1p // k1c, c // tc),
        in_specs=in_specs,
        out_specs=a_spec,
        out_shape=jax.ShapeDtypeStruct(a4.shape, dtype),
        compiler_params=_params(("parallel", "arbitrary")),
        name="dft_stage2",
    )(*args)


def _dft_stage3_kernel(c_ref, b_ref, u_ref, m_ref, skip_ref, o_ref):
    y = (jnp.dot(c_ref[0], b_ref[0], preferred_element_type=F32)
         - jnp.dot(c_ref[1], b_ref[1], preferred_element_type=F32))
    u = u_ref[...].astype(F32)
    o_ref[0] = (m_ref[...].astype(F32) * (y + u * skip_ref[...])).astype(o_ref.dtype)


def _dft_stage3(cinv, b3, u3, u_group, m3, m_group, skip_w):
    _, k1p, w = b3.shape
    n_t1 = cinv.shape[1]
    nc = _tile(w, 4096, LANES)
    return pl.pallas_call(
        _dft_stage3_kernel,
        grid=(w // nc,),
        in_specs=[
            pl.BlockSpec((2, n_t1, k1p), lambda j: (0, 0, 0)),
            pl.BlockSpec((2, k1p, nc), lambda j: (0, 0, j)),
            pl.BlockSpec((None, n_t1, nc), lambda j: (u_group, 0, j)),
            pl.BlockSpec((None, n_t1, nc), lambda j: (m_group, 0, j)),
            pl.BlockSpec((1, nc), lambda j: (0, j)),
        ],
        out_specs=pl.BlockSpec((1, n_t1, nc), lambda j: (0, 0, j)),
        out_shape=jax.ShapeDtypeStruct((1, n_t1, w), BF16),
        compiler_params=_params(("parallel",)),
        name="dft_stage3",
    )(cinv, b3, u3, m3, skip_w)


def _bidir_kernel_rows(filt, order):
    return jnp.concatenate([filt[:, 0, order], filt[::-1, 1, order]], axis=0)


def _hyena_latent(p_lat, filt, skip, consts):
    _, length, c = p_lat.shape
    n1, n_t1, k1p = consts["n1"], consts["n_t1"], consts["k1p"]
    w = DFT_N2 * c
    p3 = p_lat.reshape(HYENA_ORDER + 1, n_t1, w)
    u3, u_group = p3, 0
    for order in range(HYENA_ORDER):
        k3 = _bidir_kernel_rows(filt, order).astype(BF16).reshape(1, n1, w)
        kf = _dft_stage2(_dft_stage1(consts["f1"], k3, 0).reshape(2, k1p, DFT_N2, c), consts["g"])
        a = _dft_stage1(consts["f1_sig"], u3, u_group)
        b = _dft_stage2(a.reshape(2, k1p, DFT_N2, c), consts["g"], kf)
        skip_w = jnp.tile(skip[order], DFT_N2).reshape(1, w)
        u3 = _dft_stage3(consts["cinv"], b.reshape(2, k1p, w), u3, u_group, p3, order + 1, skip_w)
        u_group = 0
    return u3.reshape(length, c)


def _ctx_dft_constants(length):
    n = 2 * length
    k = np.arange(n, dtype=np.float64)
    ang = -2.0 * np.pi * k[:, None] * k[None, :] / n
    fwd = np.stack([np.cos(ang), np.sin(ang)])
    inv = np.stack([np.cos(ang[:length]), -np.sin(ang[:length])]) / n
    as_f32 = lambda a: jnp.asarray(a.astype(np.float32))
    return dict(fwd=as_f32(fwd), fwd_sig=as_f32(fwd[:, :, :length]), inv=as_f32(inv))


def _ctxconv_kernel(ff_ref, fs_ref, fi_ref, k_ref, u_ref, m_ref, skip_ref, o_ref):
    dot = functools.partial(jnp.dot, preferred_element_type=F32, precision=HIGHEST)
    k = k_ref[...]
    u = u_ref[...].astype(F32)
    kr, ki = dot(ff_ref[0], k), dot(ff_ref[1], k)
    xr, xi = dot(fs_ref[0], u), dot(fs_ref[1], u)
    zr = xr * kr - xi * ki
    zi = xr * ki + xi * kr
    y = dot(fi_ref[0], zr) - dot(fi_ref[1], zi)
    o_ref[0] = (m_ref[...].astype(F32) * (y + u * skip_ref[...])).astype(o_ref.dtype)


def _ctxconv(consts, k, u3, u_group, m3, m_group, skip):
    _, length, c = u3.shape
    n = 2 * length
    tc = _tile(c, 256, LANES)
    return pl.pallas_call(
        _ctxconv_kernel,
        grid=(c // tc,),
        in_specs=[
            pl.BlockSpec((2, n, n), lambda j: (0, 0, 0)),
            pl.BlockSpec((2, n, length), lambda j: (0, 0, 0)),
            pl.BlockSpec((2, length, n), lambda j: (0, 0, 0)),
            pl.BlockSpec((n, tc), lambda j: (0, j)),
            pl.BlockSpec((None, length, tc), lambda j: (u_group, 0, j)),
            pl.BlockSpec((None, length, tc), lambda j: (m_group, 0, j)),
            pl.BlockSpec((1, tc), lambda j: (0, j)),
        ],
        out_specs=pl.BlockSpec((1, length, tc), lambda j: (0, 0, j)),
        out_shape=jax.ShapeDtypeStruct((1, length, c), BF16),
        compiler_params=_params(("parallel",)),
        name="ctxconv",
    )(consts["fwd"], consts["fwd_sig"], consts["inv"], k, u3, m3, skip.reshape(1, c))


def _hyena_context(p_ctx, filt, skip, consts):
    u3, u_group = p_ctx, 0
    for order in range(HYENA_ORDER):
        u3 = _ctxconv(consts, _bidir_kernel_rows(filt, order), u3, u_group, p_ctx, order + 1, skip[order])
        u_group = 0
    return u3[0]


def kernel(x, c, ctx, c_ctx, norm1_g, norm2_g, mod_a, mod_b, mod_bias, w_in, q_norm_g, k_norm_g, hy_conv_w, hy_conv_b, hy_w1, hy_b1, hy_w2, hy_b2, hy_w3, hy_b3, hy_w4, hy_freq, hy_skip, gm_ln_g, gm_ln_b, gm_ws, gm_bs, gate_up, gate_b, w_br, w_out, rt_group_w, rt_group_b, rt_expert_w, rt_expert_b, ex_w_gate, ex_w_up, ex_w_down, final_norm_g):
    bsz, seq, d = x.shape
    assert bsz == 1 and c.shape[0] == 1
    n_ctx = ctx.shape[1]
    depth = w_in.shape[0]
    t_all = seq + n_ctx
    d_hy = hy_skip.shape[2]
    d_gm = gm_ln_g.shape[1]
    d_at = w_br.shape[1] - d_hy - d_gm
    gate_rank = gate_up.shape[2]
    n_q = d_at // HEAD_DIM
    n_kv = n_q // KV_REP
    q_end = d_at
    v_start = q_end + n_kv * HEAD_DIM
    v_end = v_start + n_kv * HEAD_DIM
    hy_end = v_end + (HYENA_ORDER + 1) * d_hy
    gm_end = hy_end + 2 * d_gm
    n_groups = rt_group_w.shape[2]
    per_group = rt_expert_w.shape[3]
    n_experts = n_groups * per_group
    assert n_groups + n_experts <= ROUTER_COLS
    assert hy_end % (2 * d_gm) == 0 and gm_end % gate_rank == 0
    tme = 256

    xs = jnp.concatenate([x[0], ctx[0]], axis=0)
    cond = jnp.stack([c[0], c_ctx], axis=0)
    cos, sin = _rope_tables(seq, n_ctx)

    tq = _tile(seq, 256, 16)
    tk = _tile(t_all, 2816, 256)
    tq_c = _tile(n_ctx, 256, 16)
    dft_lat = _dft_constants(seq)
    dft_ctx = _ctx_dft_constants(n_ctx)

    for l in range(depth):
        m = (jnp.dot(jnp.dot(jax.nn.silu(cond), mod_a[l], precision=HIGHEST), mod_b[l], precision=HIGHEST)
             + mod_bias[l]).reshape(2, N_MOD, d)

        px = _modproj(xs, norm1_g[l], m[:, 0], m[:, 1], w_in[l].astype(BF16), seq)
        q, k = _qkprep(px, cos, sin, q_norm_g[l], k_norm_g[l], n_q, n_kv)
        o_at = _attention(q, k, px, v_start // HEAD_DIM, 0, seq, 0, t_all, tq, tk)
        o_at = _attention(q, k, px, v_start // HEAD_DIM, seq, n_ctx, seq, n_ctx, tq_c, n_ctx, out=o_at)

        hy_w = (hy_w1[l], hy_b1[l], hy_w2[l], hy_b2[l], hy_w3[l], hy_b3[l], hy_w4[l], hy_freq[l])
        p_lat, p_ctx = _shortconv(px, v_end, d_hy, hy_conv_w[l], hy_conv_b[l], seq, n_ctx)
        o_hy_x = _hyena_latent(p_lat, _hyena_filters(seq, *hy_w, d_hy), hy_skip[l], dft_lat)
        o_hy_c = _hyena_context(p_ctx, _hyena_filters(n_ctx, *hy_w, d_hy), hy_skip[l], dft_ctx)
        o_hy = jnp.concatenate([o_hy_x, o_hy_c], axis=0)

        o_gm = _gmlp(px, hy_end // (2 * d_gm), d_gm, gm_ln_g[l], gm_ln_b[l], gm_ws[l], gm_bs[l])

        merged = _merge(o_hy, o_at, o_gm, px, gm_end // gate_rank, gate_rank,
                        w_br[l].astype(BF16), gate_up[l].astype(BF16), gate_b[l])
        xs = _outproj(merged, w_out[l].astype(BF16), xs, m[:, 2], seq)

        rw = jnp.concatenate([rt_group_w[l]] + [rt_expert_w[l, g] for g in range(n_groups)], axis=1)
        rw = jnp.pad(rw, ((0, 0), (0, ROUTER_COLS - rw.shape[1])))
        rb = jnp.concatenate([rt_group_b[l], rt_expert_b[l].reshape(-1)])
        rb = jnp.pad(rb, (0, ROUTER_COLS - rb.shape[0])).reshape(1, ROUTER_COLS)
        h2, logits = _router(xs, norm2_g[l], m[:, 3], m[:, 4], rw, rb, seq)
        eid, wts = _route(logits, n_groups, per_group)
        src_tok, roww, pos, tile_expert, tile_block, tile_first, n_used = _dispatch(eid, wts, n_experts, tme)
        xg = jnp.take(h2, src_tok, axis=0)
        yo = _experts(xg, roww, tile_expert, tile_block, tile_first, n_used,
                      ex_w_gate[l], ex_w_up[l], ex_w_down[l], tme)
        moe = jnp.take(yo, pos[:, 0], axis=0).astype(F32) + jnp.take(yo, pos[:, 1], axis=0).astype(F32)
        gate = jnp.concatenate([jnp.broadcast_to(m[0:1, 5], (seq, d)), jnp.broadcast_to(m[1:2, 5], (n_ctx, d))], axis=0)
        xs = xs + gate * moe

    return _final_norm(xs, final_norm_g, seq).reshape(1, seq, d)
```

```python
import functools
import math

import numpy as np
import jax
import jax.numpy as jnp
from jax import lax
from jax.experimental import pallas as pl
from jax.experimental.pallas import tpu as pltpu

F32 = jnp.float32
BF16 = jnp.bfloat16
HIGHEST = lax.Precision.HIGHEST

HEAD_DIM = 128
KV_REP = 4
GRID_W = 64
ROPE_THETA = 10000.0
ATTN_SCALE = HEAD_DIM ** -0.5
LOG2_E = math.log2(math.e)
HYENA_ORDER = 2
HYENA_SHIFT = 0.05
HYENA_FAST_PCT = 0.3
HYENA_SLOW_PCT = 1.5
HYENA_TARGET = 1e-2
GMLP_CHUNK = 128
GMLP_GROUP_W = 128
N_BRANCH = 3
N_MOD = 6
TOP_K = 2
EPS = 1e-6
LN_EPS = 1e-5

LANES = 128
ROUTER_COLS = LANES
VMEM_LIMIT_MB = 56


def _params(semantics, vmem_mb=VMEM_LIMIT_MB):
    return pltpu.CompilerParams(dimension_semantics=semantics, vmem_limit_bytes=vmem_mb * 1024 * 1024)


def _tile(n, target, mult):
    best = None
    for d in range(mult, min(n, target) + 1, mult):
        if n % d == 0:
            best = d
    assert best is not None, (n, target, mult)
    return best


ROW_CHUNK = 16


def _modulate_rows(x_ref, g_ref, sh_ref, sc_ref, h_ref, row0, n_lat, n_rows):
    def body(r, carry):
        rows = pl.ds(pl.multiple_of(r * ROW_CHUNK, ROW_CHUNK), ROW_CHUNK)
        xf = x_ref[rows, :]
        ms = jnp.mean(xf * xf, axis=-1, keepdims=True)
        y = xf * lax.rsqrt(ms + EPS) * g_ref[...]
        rid = row0 + r * ROW_CHUNK + lax.broadcasted_iota(jnp.int32, (ROW_CHUNK, 1), 0)
        is_ctx = rid >= n_lat
        sc = jnp.where(is_ctx, sc_ref[1:2, :], sc_ref[0:1, :])
        sh = jnp.where(is_ctx, sh_ref[1:2, :], sh_ref[0:1, :])
        h_ref[rows, :] = (y * (1.0 + sc) + sh).astype(h_ref.dtype)
        return carry

    lax.fori_loop(0, n_rows // ROW_CHUNK, body, 0)


def _modproj_kernel(x_ref, g_ref, sh_ref, sc_ref, w_ref, o_ref, h_ref, *, n_lat, tm):
    i = pl.program_id(0)

    @pl.when(pl.program_id(1) == 0)
    def _():
        _modulate_rows(x_ref, g_ref, sh_ref, sc_ref, h_ref, i * tm, n_lat, tm)

    o_ref[...] = jnp.dot(h_ref[...], w_ref[...], preferred_element_type=F32).astype(o_ref.dtype)


def _modproj(x, g, sh, sc, w, n_lat):
    t, d = x.shape
    n = w.shape[1]
    tm = _tile(t, 528, 16)
    tn = _tile(n, 512, LANES)
    return pl.pallas_call(
        functools.partial(_modproj_kernel, n_lat=n_lat, tm=tm),
        grid=(t // tm, n // tn),
        in_specs=[
            pl.BlockSpec((tm, d), lambda i, j: (i, 0)),
            pl.BlockSpec((1, d), lambda i, j: (0, 0)),
            pl.BlockSpec((2, d), lambda i, j: (0, 0)),
            pl.BlockSpec((2, d), lambda i, j: (0, 0)),
            pl.BlockSpec((d, tn), lambda i, j: (0, j)),
        ],
        out_specs=pl.BlockSpec((tm, tn), lambda i, j: (i, j)),
        out_shape=jax.ShapeDtypeStruct((t, n), BF16),
        scratch_shapes=[pltpu.VMEM((tm, d), BF16)],
        compiler_params=_params(("parallel", "arbitrary")),
        name="modproj",
    )(x, g.reshape(1, d), sh, sc, w)


def _qkprep_kernel(p_ref, cos_ref, sin_ref, qg_ref, kg_ref, q_ref, k_ref, *, n_q, n_kv):
    cos = cos_ref[...]
    sin = sin_ref[...]
    lane = lax.broadcasted_iota(jnp.int32, (1, HEAD_DIM), 1)
    first_half = (lane % (HEAD_DIM // 2)) < (HEAD_DIM // 4)

    def head(h, gain, scale):
        xh = p_ref[:, h * HEAD_DIM:(h + 1) * HEAD_DIM].astype(F32)
        ms = jnp.mean(xh * xh, axis=-1, keepdims=True)
        y = xh * lax.rsqrt(ms + EPS) * gain
        partner = jnp.where(first_half,
                            pltpu.roll(y, HEAD_DIM - HEAD_DIM // 4, 1),
                            pltpu.roll(y, HEAD_DIM // 4, 1))
        return (y * cos + partner * sin) * scale

    for h in range(n_q):
        q_ref[h] = head(h, qg_ref[...], ATTN_SCALE * LOG2_E).astype(q_ref.dtype)
    for h in range(n_kv):
        k_ref[h] = head(n_q + h, kg_ref[...], 1.0).astype(k_ref.dtype)


def _qkprep(px, cos, sin, qg, kg, n_q, n_kv):
    t = px.shape[0]
    tr = _tile(t, 264, 8)
    width = (n_q + n_kv) * HEAD_DIM
    return pl.pallas_call(
        functools.partial(_qkprep_kernel, n_q=n_q, n_kv=n_kv),
        grid=(t // tr,),
        in_specs=[
            pl.BlockSpec((tr, width), lambda i: (i, 0)),
            pl.BlockSpec((tr, HEAD_DIM), lambda i: (i, 0)),
            pl.BlockSpec((tr, HEAD_DIM), lambda i: (i, 0)),
            pl.BlockSpec((1, HEAD_DIM), lambda i: (0, 0)),
            pl.BlockSpec((1, HEAD_DIM), lambda i: (0, 0)),
        ],
        out_specs=[
            pl.BlockSpec((n_q, tr, HEAD_DIM), lambda i: (0, i, 0)),
            pl.BlockSpec((n_kv, tr, HEAD_DIM), lambda i: (0, i, 0)),
        ],
        out_shape=[
            jax.ShapeDtypeStruct((n_q, t, HEAD_DIM), BF16),
            jax.ShapeDtypeStruct((n_kv, t, HEAD_DIM), BF16),
        ],
        compiler_params=_params(("parallel",)),
        name="qkprep",
    )(px, cos, sin, qg.reshape(1, HEAD_DIM), kg.reshape(1, HEAD_DIM))


def _rope_tables(n_lat, n_ctx):
    n = HEAD_DIM // 4
    inv = ROPE_THETA ** (-np.arange(n, dtype=np.float32) / n)
    t = np.arange(n_lat)
    rows = (t // GRID_W).astype(np.float32)
    cols = (t % GRID_W).astype(np.float32)
    ang = np.concatenate([rows[:, None] * inv[None, :]] * 2 + [cols[:, None] * inv[None, :]] * 2, axis=1)
    ang = jnp.asarray(ang.astype(np.float32))
    sign = np.tile(np.concatenate([-np.ones(n, np.float32), np.ones(n, np.float32)]), 2)
    cos = jnp.concatenate([jnp.cos(ang), jnp.ones((n_ctx, HEAD_DIM), F32)], axis=0)
    sin = jnp.concatenate([jnp.sin(ang) * sign[None, :], jnp.zeros((n_ctx, HEAD_DIM), F32)], axis=0)
    return cos, sin


def _attn_kernel(*refs, tq, tk, n_kv, aliased):
    if aliased:
        q_ref, k_ref, v_ref, _, o_ref, s_ref, m_ref, l_ref, acc_ref = refs
    else:
        q_ref, k_ref, v_ref, o_ref, s_ref, m_ref, l_ref, acc_ref = refs
    n_chunks = n_kv // tk
    m_ref[...] = jnp.full(m_ref.shape, -jnp.inf, F32)
    l_ref[...] = jnp.zeros(l_ref.shape, F32)
    acc_ref[...] = jnp.zeros(acc_ref.shape, F32)

    def scores(r, ks):
        return lax.dot_general(q_ref[r], k_ref[0, ks, :], (((1,), (1,)), ((), ())), preferred_element_type=F32)

    s_ref[0] = scores(0, pl.ds(0, tk))

    def body(c, carry):
        ks = pl.ds(pl.multiple_of(c * tk, tk), tk)
        kn = pl.ds(pl.multiple_of(jnp.minimum(c + 1, n_chunks - 1) * tk, tk), tk)
        v = v_ref[ks, :]
        for r in range(KV_REP):
            cur, nxt = r % 2, (r + 1) % 2
            s_ref[nxt] = scores(r + 1, ks) if r + 1 < KV_REP else scores(0, kn)
            s = s_ref[cur]
            m_prev = m_ref[r]
            m_new = jnp.maximum(m_prev, jnp.max(s, axis=-1, keepdims=True))
            alpha = jnp.exp2(m_prev - m_new)
            p = jnp.exp2(s - m_new)
            l_ref[r] = alpha * l_ref[r] + jnp.sum(p, axis=-1, keepdims=True)
            acc_ref[r] = alpha * acc_ref[r] + jnp.dot(p.astype(BF16), v, preferred_element_type=F32)
            m_ref[r] = m_new
        return carry

    lax.fori_loop(0, n_chunks, body, 0)
    for r in range(KV_REP):
        o_ref[:, r * HEAD_DIM:(r + 1) * HEAD_DIM] = (acc_ref[r] / l_ref[r]).astype(o_ref.dtype)


def _attention(q, k, px, v_col_block, q_row0, n_q_rows, kv_row0, n_kv, tq, tk, out=None):
    n_heads, t, _ = q.shape
    n_kv_heads = k.shape[0]
    assert q_row0 % tq == 0 and n_q_rows % tq == 0 and kv_row0 % n_kv == 0 and n_kv % tk == 0
    qb0 = q_row0 // tq
    kb0 = kv_row0 // n_kv
    aliased = out is not None
    in_specs = [
        pl.BlockSpec((KV_REP, tq, HEAD_DIM), lambda g, i: (g, qb0 + i, 0)),
        pl.BlockSpec((1, n_kv, HEAD_DIM), lambda g, i: (g, kb0, 0)),
        pl.BlockSpec((n_kv, HEAD_DIM), lambda g, i: (kb0, v_col_block + g)),
    ]
    args = [q, k, px]
    if aliased:
        in_specs.append(pl.BlockSpec(memory_space=pl.ANY))
        args.append(out)
    return pl.pallas_call(
        functools.partial(_attn_kernel, tq=tq, tk=tk, n_kv=n_kv, aliased=aliased),
        grid=(n_kv_heads, n_q_rows // tq),
        in_specs=in_specs,
        out_specs=pl.BlockSpec((tq, KV_REP * HEAD_DIM), lambda g, i: (qb0 + i, g)),
        out_shape=jax.ShapeDtypeStruct((t, n_heads * HEAD_DIM), BF16),
        scratch_shapes=[
            pltpu.VMEM((2, tq, tk), F32),
            pltpu.VMEM((KV_REP, tq, 1), F32),
            pltpu.VMEM((KV_REP, tq, 1), F32),
            pltpu.VMEM((KV_REP, tq, HEAD_DIM), F32),
        ],
        input_output_aliases={3: 0} if aliased else {},
        compiler_params=_params(("parallel", "arbitrary")),
        name="attention",
    )(*args)


def _gmlp_kernel(p_ref, lng_ref, lnb_ref, ws_ref, bst_ref, o_ref, *, n_chunks, d_g):
    n_groups = d_g // GMLP_GROUP_W
    for b in range(n_chunks):
        rows = slice(b * GMLP_CHUNK, (b + 1) * GMLP_CHUNK)
        uv = jax.nn.gelu(p_ref[rows, :].astype(F32))
        u = uv[:, :d_g]
        v = uv[:, d_g:]
        vc = v - jnp.mean(v, axis=-1, keepdims=True)
        var = jnp.mean(vc * vc, axis=-1, keepdims=True)
        vn = (vc * lax.rsqrt(var + LN_EPS) * lng_ref[...] + lnb_ref[...]).astype(BF16)
        for g in range(n_groups):
            cols = slice(g * GMLP_GROUP_W, (g + 1) * GMLP_GROUP_W)
            vm = jnp.dot(ws_ref[g].astype(BF16), vn[:, cols], preferred_element_type=F32) + bst_ref[:, g:g + 1]
            o_ref[rows, cols] = (u[:, cols] * vm).astype(o_ref.dtype)


def _gmlp(px, col_block, d_g, ln_g, ln_b, ws, bs):
    t = px.shape[0]
    n_chunks = 2
    tr = n_chunks * GMLP_CHUNK
    assert t % tr == 0
    n_groups = d_g // GMLP_GROUP_W
    return pl.pallas_call(
        functools.partial(_gmlp_kernel, n_chunks=n_chunks, d_g=d_g),
        grid=(t // tr,),
        in_specs=[
            pl.BlockSpec((tr, 2 * d_g), lambda i: (i, col_block)),
            pl.BlockSpec((1, d_g), lambda i: (0, 0)),
            pl.BlockSpec((1, d_g), lambda i: (0, 0)),
            pl.BlockSpec((n_groups, GMLP_CHUNK, GMLP_CHUNK), lambda i: (0, 0, 0)),
            pl.BlockSpec((GMLP_CHUNK, n_groups), lambda i: (0, 0)),
        ],
        out_specs=pl.BlockSpec((tr, d_g), lambda i: (i, 0)),
        out_shape=jax.ShapeDtypeStruct((t, d_g), BF16),
        compiler_params=_params(("parallel",)),
        name="gmlp",
    )(px, ln_g.reshape(1, d_g), ln_b.reshape(1, d_g), ws, bs.T)


def _merge_kernel(hy_ref, at_ref, gm_ref, gc_ref, wbr_ref, gup_ref, gb_ref, o_ref, *, d_hy, d_at):
    gc = gc_ref[...]
    bounds = (0, d_hy, d_hy + d_at, wbr_ref.shape[0])
    merged = None
    for b, a_ref in enumerate((hy_ref, at_ref, gm_ref)):
        y = jnp.dot(a_ref[...], wbr_ref[bounds[b]:bounds[b + 1], :], preferred_element_type=F32)
        gate = jax.nn.sigmoid(jnp.dot(gc, gup_ref[b], preferred_element_type=F32) + gb_ref[b:b + 1, :])
        merged = gate * y if merged is None else merged + gate * y
    o_ref[...] = merged.astype(o_ref.dtype)


def _merge(o_hy, o_at, o_gm, px, gc_col_block, gate_rank, w_br, gate_up, gate_b):
    t = px.shape[0]
    d_hy, d_at, d_gm = o_hy.shape[1], o_at.shape[1], o_gm.shape[1]
    d = w_br.shape[1]
    tm = _tile(t, 528, 16)
    tn = _tile(d, 512, LANES)
    return pl.pallas_call(
        functools.partial(_merge_kernel, d_hy=d_hy, d_at=d_at),
        grid=(t // tm, d // tn),
        in_specs=[
            pl.BlockSpec((tm, d_hy), lambda i, j: (i, 0)),
            pl.BlockSpec((tm, d_at), lambda i, j: (i, 0)),
            pl.BlockSpec((tm, d_gm), lambda i, j: (i, 0)),
            pl.BlockSpec((tm, gate_rank), lambda i, j: (i, gc_col_block)),
            pl.BlockSpec((d_hy + d_at + d_gm, tn), lambda i, j: (0, j)),
            pl.BlockSpec((N_BRANCH, gate_rank, tn), lambda i, j: (0, 0, j)),
            pl.BlockSpec((N_BRANCH, tn), lambda i, j: (0, j)),
        ],
        out_specs=pl.BlockSpec((tm, tn), lambda i, j: (i, j)),
        out_shape=jax.ShapeDtypeStruct((t, d), BF16),
        compiler_params=_params(("parallel", "arbitrary")),
        name="merge",
    )(o_hy, o_at, o_gm, px, w_br, gate_up, gate_b)


def _outproj_kernel(a_ref, w_ref, x_ref, gate_ref, o_ref, *, n_lat, tm):
    y = jnp.dot(a_ref[...], w_ref[...], preferred_element_type=F32)
    rid = pl.program_id(0) * tm + lax.broadcasted_iota(jnp.int32, (tm, 1), 0)
    gate = jnp.where(rid >= n_lat, gate_ref[1:2, :], gate_ref[0:1, :])
    o_ref[...] = x_ref[...] + gate * y


def _outproj(a, w, x, gate, n_lat):
    t, k = a.shape
    d = w.shape[1]
    tm = _tile(t, 528, 16)
    tn = _tile(d, 512, LANES)
    return pl.pallas_call(
        functools.partial(_outproj_kernel, n_lat=n_lat, tm=tm),
        grid=(t // tm, d // tn),
        in_specs=[
            pl.BlockSpec((tm, k), lambda i, j: (i, 0)),
            pl.BlockSpec((k, tn), lambda i, j: (0, j)),
            pl.BlockSpec((tm, tn), lambda i, j: (i, j)),
            pl.BlockSpec((2, tn), lambda i, j: (0, j)),
        ],
        out_specs=pl.BlockSpec((tm, tn), lambda i, j: (i, j)),
        out_shape=jax.ShapeDtypeStruct((t, d), F32),
        compiler_params=_params(("parallel", "arbitrary")),
        name="outproj",
    )(a, w, x, gate)


def _router_kernel(x_ref, g_ref, sh_ref, sc_ref, rw_ref, rb_ref, h_ref, lg_ref, *, n_lat, tm):
    _modulate_rows(x_ref, g_ref, sh_ref, sc_ref, h_ref, pl.program_id(0) * tm, n_lat, tm)
    lg_ref[...] = jnp.dot(h_ref[...].astype(F32), rw_ref[...], preferred_element_type=F32,
                          precision=HIGHEST) + rb_ref[...]


def _router(x, g, sh, sc, rw, rb, n_lat):
    t, d = x.shape
    tm = _tile(t, 264, 16)
    return pl.pallas_call(
        functools.partial(_router_kernel, n_lat=n_lat, tm=tm),
        grid=(t // tm,),
        in_specs=[
            pl.BlockSpec((tm, d), lambda i: (i, 0)),
            pl.BlockSpec((1, d), lambda i: (0, 0)),
            pl.BlockSpec((2, d), lambda i: (0, 0)),
            pl.BlockSpec((2, d), lambda i: (0, 0)),
            pl.BlockSpec((d, ROUTER_COLS), lambda i: (0, 0)),
            pl.BlockSpec((1, ROUTER_COLS), lambda i: (0, 0)),
        ],
        out_specs=[
            pl.BlockSpec((tm, d), lambda i: (i, 0)),
            pl.BlockSpec((tm, ROUTER_COLS), lambda i: (i, 0)),
        ],
        out_shape=[
            jax.ShapeDtypeStruct((t, d), BF16),
            jax.ShapeDtypeStruct((t, ROUTER_COLS), F32),
        ],
        compiler_params=_params(("parallel",)),
        name="router",
    )(x, g.reshape(1, d), sh, sc, rw, rb)


def _experts_kernel(te_ref, tb_ref, tf_ref, nu_ref, x_ref, rw_ref, wg_ref, wu_ref, wd_ref, o_ref,
                    wg_s, wu_s, wd_s):
    i = pl.program_id(0)

    @pl.when(i < nu_ref[0])
    def _():
        @pl.when(tf_ref[i] == 1)
        def _():
            wg_s[...] = wg_ref[0].astype(BF16)
            wu_s[...] = wu_ref[0].astype(BF16)
            wd_s[...] = wd_ref[0].astype(BF16)

        xg = x_ref[...]
        hg = jnp.dot(xg, wg_s[...], preferred_element_type=F32)
        hu = jnp.dot(xg, wu_s[...], preferred_element_type=F32)
        hid = (jax.nn.silu(hg) * hu * rw_ref[...]).astype(BF16)
        o_ref[...] = jnp.dot(hid, wd_s[...], preferred_element_type=F32).astype(o_ref.dtype)


def _experts(xg, roww, tile_expert, tile_block, tile_first, n_used, w_gate, w_up, w_down, tme):
    p, d = xg.shape
    f = w_gate.shape[2]
    n_tiles = p // tme
    grid_spec = pltpu.PrefetchScalarGridSpec(
        num_scalar_prefetch=4,
        grid=(n_tiles,),
        in_specs=[
            pl.BlockSpec((tme, d), lambda i, te, tb, tf, nu: (tb[i], 0)),
            pl.BlockSpec((tme, 1), lambda i, te, tb, tf, nu: (tb[i], 0)),
            pl.BlockSpec((1, d, f), lambda i, te, tb, tf, nu: (te[i], 0, 0)),
            pl.BlockSpec((1, d, f), lambda i, te, tb, tf, nu: (te[i], 0, 0)),
            pl.BlockSpec((1, f, d), lambda i, te, tb, tf, nu: (te[i], 0, 0)),
        ],
        out_specs=pl.BlockSpec((tme, d), lambda i, te, tb, tf, nu: (tb[i], 0)),
        scratch_shapes=[
            pltpu.VMEM((d, f), BF16),
            pltpu.VMEM((d, f), BF16),
            pltpu.VMEM((f, d), BF16),
        ],
    )
    return pl.pallas_call(
        _experts_kernel,
        grid_spec=grid_spec,
        out_shape=jax.ShapeDtypeStruct((p, d), BF16),
        compiler_params=_params(("arbitrary",)),
        name="experts",
    )(tile_expert, tile_block, tile_first, n_used, xg, roww, w_gate, w_up, w_down)


def _route(logits, n_groups, per_group):
    lg = logits[:, :n_groups]
    le_all = logits[:, n_groups:n_groups + n_groups * per_group].reshape(-1, n_groups, per_group)
    grp = jnp.argmax(lg, axis=-1)
    p_grp = jnp.take_along_axis(jax.nn.softmax(lg, axis=-1), grp[:, None], axis=-1)
    le = jnp.take_along_axis(le_all, grp[:, None, None], axis=1)[:, 0]
    top_p, top_i = lax.top_k(jax.nn.softmax(le, axis=-1), TOP_K)
    w = top_p / jnp.sum(top_p, axis=-1, keepdims=True) * p_grp
    eid = grp[:, None] * per_group + top_i
    return eid.astype(jnp.int32), w


def _dispatch(eid, w, n_experts, tme):
    t = eid.shape[0]
    n_pairs = t * TOP_K
    p_rows = n_pairs + n_experts * tme
    p_rows = -(-p_rows // tme) * tme
    flat_e = eid.reshape(-1)
    onehot = (flat_e[:, None] == jnp.arange(n_experts, dtype=jnp.int32)[None, :]).astype(jnp.int32)
    csum = jnp.cumsum(onehot, axis=0)
    rank = jnp.take_along_axis(csum, flat_e[:, None], axis=1)[:, 0] - 1
    counts = csum[-1]
    padded = -(-counts // tme) * tme
    ends = jnp.cumsum(padded)
    starts = ends - padded
    dest = starts[flat_e] + rank
    src_tok = jnp.zeros((p_rows,), jnp.int32).at[dest].set(jnp.arange(n_pairs, dtype=jnp.int32) // TOP_K)
    roww = jnp.zeros((p_rows,), F32).at[dest].set(w.reshape(-1))
    n_used = (ends[-1] // tme).astype(jnp.int32)
    tile_ids = jnp.arange(p_rows // tme, dtype=jnp.int32)
    tile_block = jnp.minimum(tile_ids, n_used - 1)
    tile_expert = jnp.minimum(
        jnp.searchsorted(ends, tile_block * tme, side="right").astype(jnp.int32), n_experts - 1)
    tile_first = (tile_block * tme == starts[tile_expert]).astype(jnp.int32)
    return src_tok, roww.reshape(-1, 1), dest.reshape(t, TOP_K), tile_expert, tile_block, tile_first, n_used.reshape(1)


def _rmsnorm_kernel(x_ref, g_ref, o_ref):
    xf = x_ref[...]
    ms = jnp.mean(xf * xf, axis=-1, keepdims=True)
    o_ref[...] = xf * lax.rsqrt(ms + EPS) * g_ref[...]


def _final_norm(x, g, n_rows):
    d = x.shape[1]
    tr = _tile(n_rows, 256, 8)
    return pl.pallas_call(
        _rmsnorm_kernel,
        grid=(n_rows // tr,),
        in_specs=[pl.BlockSpec((tr, d), lambda i: (i, 0)), pl.BlockSpec((1, d), lambda i: (0, 0))],
        out_specs=pl.BlockSpec((tr, d), lambda i: (i, 0)),
        out_shape=jax.ShapeDtypeStruct((n_rows, d), F32),
        compiler_params=_params(("parallel",)),
        name="final_norm",
    )(x, g.reshape(1, d))


SUBLANES = 8


def _filter_features(length, emb):
    bands = (emb - 1) // 2
    i = np.arange(length, dtype=np.float64)
    t = i / (length - 1)
    f = np.linspace(1e-4, bands - 1, bands)
    w = 2.0 * np.pi * i / length
    z = np.concatenate([t[:, None], np.cos(-f[None, :] * w[:, None]), np.sin(-f[None, :] * w[:, None])], axis=1)
    z = np.concatenate([z, z[::-1]], axis=0)
    emb_pad = -(-emb // SUBLANES) * SUBLANES
    z = np.pad(z, ((0, 0), (0, emb_pad - emb)))
    return jnp.asarray(z.astype(np.float32))


def _filter_kernel(z_ref, w1_ref, b1_ref, w2_ref, b2_ref, w3_ref, b3_ref, w4_ref, fr_ref, dl_ref,
                   k_ref, nrm_ref, *, d_hy):
    dot = functools.partial(jnp.dot, preferred_element_type=F32, precision=HIGHEST)

    @pl.when(jnp.logical_and(pl.program_id(0) == 0, pl.program_id(1) == 0))
    def _():
        nrm_ref[...] = jnp.zeros(nrm_ref.shape, F32)

    z = z_ref[...]
    fr = fr_ref[...]
    h = jnp.sin(fr * (dot(z, w1_ref[...]) + b1_ref[...]))
    h = jnp.sin(fr * (dot(h, w2_ref[...]) + b2_ref[...]))
    h = jnp.sin(fr * (dot(h, w3_ref[...]) + b3_ref[...]))
    hk = dot(h, w4_ref[...])
    window = jnp.exp(-z[:, 0:1] * dl_ref[...]) + HYENA_SHIFT
    for o in range(HYENA_ORDER):
        k = hk[:, o * d_hy:(o + 1) * d_hy] * window
        k_ref[o] = k.astype(k_ref.dtype)
        nrm_ref[o:o + 1, :] += jnp.sum(jnp.abs(k), axis=0, keepdims=True)


def _hyena_filters(length, w1, b1, w2, b2, w3, b3, w4, freq, d_hy, out_dtype):
    emb, hidden = w1.shape
    z = _filter_features(length, emb)
    emb_pad = z.shape[1]
    w1p = jnp.pad(w1, ((0, emb_pad - emb), (0, 0)))
    max_decay = math.log(HYENA_TARGET) / HYENA_FAST_PCT
    min_decay = math.log(HYENA_TARGET) / HYENA_SLOW_PCT
    deltas = jnp.asarray(np.abs(np.linspace(min_decay, max_decay, d_hy)).astype(np.float32)).reshape(1, d_hy)
    tr = _tile(length, 512, SUBLANES)
    per = length // tr
    vec = lambda a: a.reshape(1, hidden)
    full = lambda shape: pl.BlockSpec(shape, lambda d, i: (0,) * len(shape))
    return pl.pallas_call(
        functools.partial(_filter_kernel, d_hy=d_hy),
        grid=(2, per),
        in_specs=[
            pl.BlockSpec((tr, emb_pad), lambda d, i: (d * per + i, 0)),
            full((emb_pad, hidden)), full((1, hidden)),
            full((hidden, hidden)), full((1, hidden)),
            full((hidden, hidden)), full((1, hidden)),
            pl.BlockSpec((hidden, HYENA_ORDER * d_hy), lambda d, i: (0, d)),
            full((1, hidden)), full((1, d_hy)),
        ],
        out_specs=[
            pl.BlockSpec((HYENA_ORDER, tr, d_hy), lambda d, i: (0, d * per + i, 0)),
            pl.BlockSpec((HYENA_ORDER, d_hy), lambda d, i: (0, 0)),
        ],
        out_shape=[
            jax.ShapeDtypeStruct((HYENA_ORDER, 2 * length, d_hy), out_dtype),
            jax.ShapeDtypeStruct((HYENA_ORDER, d_hy), F32),
        ],
        compiler_params=_params(("arbitrary", "arbitrary")),
        name="hyena_filter",
    )(z, w1p, vec(b1), w2, vec(b2), w3, vec(b3), w4, vec(freq), deltas)


CONV_ROWS = 64
CONV_HALO = 8


def _shortconv_kernel(p_ref, w_ref, b_ref, lat_ref, ctx_ref, pad_ref, *, n_lat, n_ctx):
    t = n_lat + n_ctx
    tc = p_ref.shape[1]
    pad_ref[0:CONV_HALO, :] = jnp.zeros((CONV_HALO, tc), F32)
    pad_ref[t + CONV_HALO:t + 2 * CONV_HALO, :] = jnp.zeros((CONV_HALO, tc), F32)

    def fill(r, carry):
        src = pl.ds(pl.multiple_of(r * CONV_ROWS, CONV_ROWS), CONV_ROWS)
        dst = pl.ds(pl.multiple_of(r * CONV_ROWS + CONV_HALO, CONV_HALO), CONV_ROWS)
        pad_ref[dst, :] = p_ref[src, :].astype(F32)
        return carry

    lax.fori_loop(0, t // CONV_ROWS, fill, 0)
    w0, w1, w2, bias = w_ref[0:1, :], w_ref[1:2, :], w_ref[2:3, :], b_ref[...]
    ext_rows = CONV_ROWS + 2 * CONV_HALO
    inner = slice(CONV_HALO, CONV_HALO + CONV_ROWS)

    def segment(out_ref, row0, n_rows):
        def body(r, carry):
            ext = pad_ref[pl.ds(pl.multiple_of(row0 + r * CONV_ROWS, CONV_HALO), ext_rows), :]
            rid = r * CONV_ROWS + lax.broadcasted_iota(jnp.int32, (CONV_ROWS, 1), 0)
            prev = jnp.where(rid == 0, 0.0, pltpu.roll(ext, 1, 0)[inner])
            nxt = jnp.where(rid == n_rows - 1, 0.0, pltpu.roll(ext, ext_rows - 1, 0)[inner])
            y = prev * w0 + ext[inner] * w1 + nxt * w2 + bias
            out_ref[0, pl.ds(pl.multiple_of(r * CONV_ROWS, CONV_ROWS), CONV_ROWS), :] = y.astype(out_ref.dtype)
            return carry

        lax.fori_loop(0, n_rows // CONV_ROWS, body, 0)

    segment(lat_ref, 0, n_lat)
    segment(ctx_ref, n_lat, n_ctx)


def _shortconv(px, col0, d_hy, conv_w, conv_b, n_lat, n_ctx):
    t = px.shape[0]
    tc = _tile(d_hy, 256, LANES)
    assert col0 % tc == 0 and n_lat % CONV_ROWS == 0 and n_ctx % CONV_ROWS == 0
    per = d_hy // tc
    n_grp = HYENA_ORDER + 1
    return pl.pallas_call(
        functools.partial(_shortconv_kernel, n_lat=n_lat, n_ctx=n_ctx),
        grid=(n_grp, per),
        in_specs=[
            pl.BlockSpec((t, tc), lambda g, j: (0, col0 // tc + g * per + j)),
            pl.BlockSpec((3, tc), lambda g, j: (0, g * per + j)),
            pl.BlockSpec((1, tc), lambda g, j: (0, g * per + j)),
        ],
        out_specs=[
            pl.BlockSpec((1, n_lat, tc), lambda g, j: (g, 0, j)),
            pl.BlockSpec((1, n_ctx, tc), lambda g, j: (g, 0, j)),
        ],
        out_shape=[
            jax.ShapeDtypeStruct((n_grp, n_lat, d_hy), BF16),
            jax.ShapeDtypeStruct((n_grp, n_ctx, d_hy), BF16),
        ],
        scratch_shapes=[pltpu.VMEM((t + 2 * CONV_HALO, tc), F32)],
        compiler_params=_params(("parallel", "parallel")),
        name="shortconv",
    )(px, conv_w, conv_b.reshape(1, -1))


DFT_N2 = 128
K1_ALIGN = 16


def _dft_constants(length):
    n = 2 * length
    n1 = n // DFT_N2
    assert n1 * DFT_N2 == n and n1 % 2 == 0
    n_t1 = n1 // 2
    k1_used = n1 // 2 + 1
    k1p = -(-k1_used // K1_ALIGN) * K1_ALIGN
    k1 = np.arange(k1p, dtype=np.float64)
    live = (k1 < k1_used).astype(np.float64)
    t1 = np.arange(n1, dtype=np.float64)
    ang1 = -2.0 * np.pi * k1[:, None] * t1[None, :] / n1
    f1 = np.stack([np.cos(ang1), np.sin(ang1)]) * live[None, :, None]
    k2 = np.arange(DFT_N2, dtype=np.float64)
    t2 = np.arange(DFT_N2, dtype=np.float64)
    ang2 = -2.0 * np.pi * t2[None, None, :] * (k1[:, None, None] + n1 * k2[None, :, None]) / n
    gr = np.cos(ang2) * live[:, None, None]
    gi = np.sin(ang2) * live[:, None, None]
    g = np.stack([gr, gi, gr.transpose(0, 2, 1), gi.transpose(0, 2, 1)])
    weight = np.where((k1 == 0) | (k1 == n1 // 2), 1.0, 2.0) * live
    angc = 2.0 * np.pi * np.arange(n_t1, dtype=np.float64)[:, None] * k1[None, :] / n1
    cinv = np.stack([np.cos(angc), np.sin(angc)]) * weight[None, None, :] / n
    as_bf16 = lambda a: jnp.asarray(a.astype(np.float32)).astype(BF16)
    return dict(n1=n1, n_t1=n_t1, k1p=k1p, f1=as_bf16(f1), f1_sig=as_bf16(f1[:, :, :n_t1]),
                g=as_bf16(g), cinv=as_bf16(cinv))


def _dft_stage1_kernel(f_ref, u_ref, a_ref):
    u = u_ref[...]
    a_ref[0] = jnp.dot(f_ref[0], u, preferred_element_type=F32).astype(a_ref.dtype)
    a_ref[1] = jnp.dot(f_ref[1], u, preferred_element_type=F32).astype(a_ref.dtype)


def _dft_stage1(f, u3, group):
    _, k, w = u3.shape
    k1p = f.shape[1]
    nc = _tile(w, 4096, LANES)
    return pl.pallas_call(
        _dft_stage1_kernel,
        grid=(w // nc,),
        in_specs=[
            pl.BlockSpec((2, k1p, k), lambda j: (0, 0, 0)),
            pl.BlockSpec((None, k, nc), lambda j: (group, 0, j)),
        ],
        out_specs=pl.BlockSpec((2, k1p, nc), lambda j: (0, 0, j)),
        out_shape=jax.ShapeDtypeStruct((2, k1p, w), BF16),
        compiler_params=_params(("parallel",)),
        name="dft_stage1",
    )(f, u3)


def _dft_stage2_filter_kernel(a_ref, g_ref, scale_ref, kf_ref, *, k1c):
    scale = scale_ref[...]

    def body(i, carry):
        ar, ai = a_ref[0, i], a_ref[1, i]
        gr, gi = g_ref[0, i], g_ref[1, i]
        dot = functools.partial(jnp.dot, preferred_element_type=F32)
        kf_ref[0, i] = (dot(gr, ar) - dot(gi, ai)) * scale
        kf_ref[1, i] = (dot(gr, ai) + dot(gi, ar)) * scale
        return carry

    lax.fori_loop(0, k1c, body, 0)


def _dft_stage2_conv_kernel(a_ref, g_ref, kf_ref, b_ref, *, k1c):
    def body(i, carry):
        ar, ai = a_ref[0, i], a_ref[1, i]
        gr, gi = g_ref[0, i], g_ref[1, i]
        dot = functools.partial(jnp.dot, preferred_element_type=F32)
        xr = dot(gr, ar) - dot(gi, ai)
        xi = dot(gr, ai) + dot(gi, ar)
        kr, ki = kf_ref[0, i], kf_ref[1, i]
        zr = (xr * kr - xi * ki).astype(BF16)
        zi = (xr * ki + xi * kr).astype(BF16)
        grt, git = g_ref[2, i], g_ref[3, i]
        b_ref[0, i] = (dot(grt, zr) + dot(git, zi)).astype(b_ref.dtype)
        b_ref[1, i] = (dot(grt, zi) - dot(git, zr)).astype(b_ref.dtype)
        return carry

    lax.fori_loop(0, k1c, body, 0)


def _dft_stage2(a4, g, kf=None, scale=None):
    _, k1p, n2, c = a4.shape
    k1c = K1_ALIGN
    tc = _tile(c, 256, LANES)
    a_spec = pl.BlockSpec((2, k1c, n2, tc), lambda i, j: (0, i, 0, j))
    g_spec = pl.BlockSpec((4, k1c, n2, n2), lambda i, j: (0, i, 0, 0))
    if kf is None:
        s_spec = pl.BlockSpec((1, tc), lambda i, j: (0, j))
        body, args, in_specs, dtype = _dft_stage2_filter_kernel, (a4, g, scale), [a_spec, g_spec, s_spec], F32
    else:
        body, args, in_specs, dtype = _dft_stage2_conv_kernel, (a4, g, kf), [a_spec, g_spec, a_spec], BF16
    return pl.pallas_call(
        functools.partial(body, k1c=k1c),
        grid=(k1p // k1c, c // tc),
        in_specs=in_specs,
        out_specs=a_spec,
        out_shape=jax.ShapeDtypeStruct(a4.shape, dtype),
        compiler_params=_params(("parallel", "arbitrary")),
        name="dft_stage2",
    )(*args)


def _dft_stage3_kernel(c_ref, b_ref, u_ref, m_ref, skip_ref, o_ref):
    y = (jnp.dot(c_ref[0], b_ref[0], preferred_element_type=F32)
         - jnp.dot(c_ref[1], b_ref[1], preferred_element_type=F32))
    u = u_ref[...].astype(F32)
    o_ref[0] = (m_ref[...].astype(F32) * (y + u * skip_ref[...])).astype(o_ref.dtype)


def _dft_stage3(cinv, b3, u3, u_group, m3, m_group, skip_w):
    _, k1p, w = b3.shape
    n_t1 = cinv.shape[1]
    nc = _tile(w, 4096, LANES)
    return pl.pallas_call(
        _dft_stage3_kernel,
        grid=(w // nc,),
        in_specs=[
            pl.BlockSpec((2, n_t1, k1p), lambda j: (0, 0, 0)),
            pl.BlockSpec((2, k1p, nc), lambda j: (0, 0, j)),
            pl.BlockSpec((None, n_t1, nc), lambda j: (u_group, 0, j)),
            pl.BlockSpec((None, n_t1, nc), lambda j: (m_group, 0, j)),
            pl.BlockSpec((1, nc), lambda j: (0, j)),
        ],
        out_specs=pl.BlockSpec((1, n_t1, nc), lambda j: (0, 0, j)),
        out_shape=jax.ShapeDtypeStruct((1, n_t1, w), BF16),
        compiler_params=_params(("parallel",)),
        name="dft_stage3",
    )(cinv, b3, u3, m3, skip_w)


def _hyena_latent(p_lat, kern, inv_norm, skip, consts):
    _, length, c = p_lat.shape
    n1, n_t1, k1p = consts["n1"], consts["n_t1"], consts["k1p"]
    w = DFT_N2 * c
    p3 = p_lat.reshape(HYENA_ORDER + 1, n_t1, w)
    k3 = kern.reshape(HYENA_ORDER, n1, w)
    u3, u_group = p3, 0
    for order in range(HYENA_ORDER):
        kf = _dft_stage2(_dft_stage1(consts["f1"], k3, order).reshape(2, k1p, DFT_N2, c), consts["g"],
                         scale=inv_norm[order:order + 1])
        a = _dft_stage1(consts["f1_sig"], u3, u_group)
        b = _dft_stage2(a.reshape(2, k1p, DFT_N2, c), consts["g"], kf)
        skip_w = jnp.tile(skip[order], DFT_N2).reshape(1, w)
        u3 = _dft_stage3(consts["cinv"], b.reshape(2, k1p, w), u3, u_group, p3, order + 1, skip_w)
        u_group = 0
    return u3.reshape(length, c)


def _ctx_dft_constants(length):
    n = 2 * length
    k = np.arange(n, dtype=np.float64)
    ang = -2.0 * np.pi * k[:, None] * k[None, :] / n
    fwd = np.stack([np.cos(ang), np.sin(ang)])
    inv = np.stack([np.cos(ang[:length]), -np.sin(ang[:length])]) / n
    as_f32 = lambda a: jnp.asarray(a.astype(np.float32))
    return dict(fwd=as_f32(fwd), fwd_sig=as_f32(fwd[:, :, :length]), inv=as_f32(inv))


def _ctxconv_kernel(ff_ref, fs_ref, fi_ref, k_ref, scale_ref, u_ref, m_ref, skip_ref, o_ref):
    dot = functools.partial(jnp.dot, preferred_element_type=F32, precision=HIGHEST)
    k = k_ref[...] * scale_ref[...]
    u = u_ref[...].astype(F32)
    kr, ki = dot(ff_ref[0], k), dot(ff_ref[1], k)
    xr, xi = dot(fs_ref[0], u), dot(fs_ref[1], u)
    zr = xr * kr - xi * ki
    zi = xr * ki + xi * kr
    y = dot(fi_ref[0], zr) - dot(fi_ref[1], zi)
    o_ref[0] = (m_ref[...].astype(F32) * (y + u * skip_ref[...])).astype(o_ref.dtype)


def _ctxconv(consts, kern, order, inv_norm, u3, u_group, m3, m_group, skip):
    _, length, c = u3.shape
    n = 2 * length
    tc = _tile(c, 256, LANES)
    return pl.pallas_call(
        _ctxconv_kernel,
        grid=(c // tc,),
        in_specs=[
            pl.BlockSpec((2, n, n), lambda j: (0, 0, 0)),
            pl.BlockSpec((2, n, length), lambda j: (0, 0, 0)),
            pl.BlockSpec((2, length, n), lambda j: (0, 0, 0)),
            pl.BlockSpec((None, n, tc), lambda j: (order, 0, j)),
            pl.BlockSpec((1, tc), lambda j: (0, j)),
            pl.BlockSpec((None, length, tc), lambda j: (u_group, 0, j)),
            pl.BlockSpec((None, length, tc), lambda j: (m_group, 0, j)),
            pl.BlockSpec((1, tc), lambda j: (0, j)),
        ],
        out_specs=pl.BlockSpec((1, length, tc), lambda j: (0, 0, j)),
        out_shape=jax.ShapeDtypeStruct((1, length, c), BF16),
        compiler_params=_params(("parallel",)),
        name="ctxconv",
    )(consts["fwd"], consts["fwd_sig"], consts["inv"], kern, inv_norm[order:order + 1], u3, m3, skip.reshape(1, c))


def _hyena_context(p_ctx, kern, inv_norm, skip, consts):
    u3, u_group = p_ctx, 0
    for order in range(HYENA_ORDER):
        u3 = _ctxconv(consts, kern, order, inv_norm, u3, u_group, p_ctx, order + 1, skip[order])
        u_group = 0
    return u3[0]


def kernel(x, c, ctx, c_ctx, norm1_g, norm2_g, mod_a, mod_b, mod_bias, w_in, q_norm_g, k_norm_g, hy_conv_w, hy_conv_b, hy_w1, hy_b1, hy_w2, hy_b2, hy_w3, hy_b3, hy_w4, hy_freq, hy_skip, gm_ln_g, gm_ln_b, gm_ws, gm_bs, gate_up, gate_b, w_br, w_out, rt_group_w, rt_group_b, rt_expert_w, rt_expert_b, ex_w_gate, ex_w_up, ex_w_down, final_norm_g):
    bsz, seq, d = x.shape
    assert bsz == 1 and c.shape[0] == 1
    n_ctx = ctx.shape[1]
    depth = w_in.shape[0]
    t_all = seq + n_ctx
    d_hy = hy_skip.shape[2]
    d_gm = gm_ln_g.shape[1]
    d_at = w_br.shape[1] - d_hy - d_gm
    gate_rank = gate_up.shape[2]
    n_q = d_at // HEAD_DIM
    n_kv = n_q // KV_REP
    q_end = d_at
    v_start = q_end + n_kv * HEAD_DIM
    v_end = v_start + n_kv * HEAD_DIM
    hy_end = v_end + (HYENA_ORDER + 1) * d_hy
    gm_end = hy_end + 2 * d_gm
    n_groups = rt_group_w.shape[2]
    per_group = rt_expert_w.shape[3]
    n_experts = n_groups * per_group
    assert n_groups + n_experts <= ROUTER_COLS
    assert hy_end % (2 * d_gm) == 0 and gm_end % gate_rank == 0
    tme = 256

    xs = jnp.concatenate([x[0], ctx[0]], axis=0)
    cond = jnp.stack([c[0], c_ctx], axis=0)
    cos, sin = _rope_tables(seq, n_ctx)

    tq = _tile(seq, 256, 16)
    tk = _tile(t_all, 2816, 256)
    tq_c = _tile(n_ctx, 256, 16)
    dft_lat = _dft_constants(seq)
    dft_ctx = _ctx_dft_constants(n_ctx)

    for l in range(depth):
        m = (jnp.dot(jnp.dot(jax.nn.silu(cond), mod_a[l], precision=HIGHEST), mod_b[l], precision=HIGHEST)
             + mod_bias[l]).reshape(2, N_MOD, d)

        px = _modproj(xs, norm1_g[l], m[:, 0], m[:, 1], w_in[l].astype(BF16), seq)
        q, k = _qkprep(px, cos, sin, q_norm_g[l], k_norm_g[l], n_q, n_kv)
        o_at = _attention(q, k, px, v_start // HEAD_DIM, 0, seq, 0, t_all, tq, tk)
        o_at = _attention(q, k, px, v_start // HEAD_DIM, seq, n_ctx, seq, n_ctx, tq_c, n_ctx, out=o_at)

        hy_w = (hy_w1[l], hy_b1[l], hy_w2[l], hy_b2[l], hy_w3[l], hy_b3[l], hy_w4[l], hy_freq[l])
        p_lat, p_ctx = _shortconv(px, v_end, d_hy, hy_conv_w[l], hy_conv_b[l], seq, n_ctx)
        kern_x, norm_x = _hyena_filters(seq, *hy_w, d_hy, BF16)
        kern_c, norm_c = _hyena_filters(n_ctx, *hy_w, d_hy, F32)
        o_hy_x = _hyena_latent(p_lat, kern_x, 1.0 / norm_x, hy_skip[l], dft_lat)
        o_hy_c = _hyena_context(p_ctx, kern_c, 1.0 / norm_c, hy_skip[l], dft_ctx)
        o_hy = jnp.concatenate([o_hy_x, o_hy_c], axis=0)

        o_gm = _gmlp(px, hy_end // (2 * d_gm), d_gm, gm_ln_g[l], gm_ln_b[l], gm_ws[l], gm_bs[l])

        merged = _merge(o_hy, o_at, o_gm, px, gm_end // gate_rank, gate_rank,
                        w_br[l].astype(BF16), gate_up[l].astype(BF16), gate_b[l])
        xs = _outproj(merged, w_out[l].astype(BF16), xs, m[:, 2], seq)

        rw = jnp.concatenate([rt_group_w[l]] + [rt_expert_w[l, g] for g in range(n_groups)], axis=1)
        rw = jnp.pad(rw, ((0, 0), (0, ROUTER_COLS - rw.shape[1])))
        rb = jnp.concatenate([rt_group_b[l], rt_expert_b[l].reshape(-1)])
        rb = jnp.pad(rb, (0, ROUTER_COLS - rb.shape[0])).reshape(1, ROUTER_COLS)
        h2, logits = _router(xs, norm2_g[l], m[:, 3], m[:, 4], rw, rb, seq)
        eid, wts = _route(logits, n_groups, per_group)
        src_tok, roww, pos, tile_expert, tile_block, tile_first, n_used = _dispatch(eid, wts, n_experts, tme)
        xg = jnp.take(h2, src_tok, axis=0)
        yo = _experts(xg, roww, tile_expert, tile_block, tile_first, n_used,
                      ex_w_gate[l], ex_w_up[l], ex_w_down[l], tme)
        moe = jnp.take(yo, pos[:, 0], axis=0).astype(F32) + jnp.take(yo, pos[:, 1], axis=0).astype(F32)
        gate = jnp.concatenate([jnp.broadcast_to(m[0:1, 5], (seq, d)), jnp.broadcast_to(m[1:2, 5], (n_ctx, d))], axis=0)
        xs = xs + gate * moe

    return _final_norm(xs, final_norm_g, seq).reshape(1, seq, d)
```

```python
import functools
import math

import numpy as np
import jax
import jax.numpy as jnp
from jax import lax
from jax.experimental import pallas as pl
from jax.experimental.pallas import tpu as pltpu

F32 = jnp.float32
BF16 = jnp.bfloat16
HIGHEST = lax.Precision.HIGHEST

HEAD_DIM = 128
KV_REP = 4
GRID_W = 64
ROPE_THETA = 10000.0
ATTN_SCALE = HEAD_DIM ** -0.5
LOG2_E = math.log2(math.e)
HYENA_ORDER = 2
HYENA_SHIFT = 0.05
HYENA_FAST_PCT = 0.3
HYENA_SLOW_PCT = 1.5
HYENA_TARGET = 1e-2
GMLP_CHUNK = 128
GMLP_GROUP_W = 128
N_BRANCH = 3
N_MOD = 6
TOP_K = 2
EPS = 1e-6
LN_EPS = 1e-5

LANES = 128
ROUTER_COLS = LANES
VMEM_LIMIT_MB = 56


def _params(semantics, vmem_mb=VMEM_LIMIT_MB):
    return pltpu.CompilerParams(dimension_semantics=semantics, vmem_limit_bytes=vmem_mb * 1024 * 1024)


def _tile(n, target, mult):
    best = None
    for d in range(mult, min(n, target) + 1, mult):
        if n % d == 0:
            best = d
    assert best is not None, (n, target, mult)
    return best


ROW_CHUNK = 16


def _modulate_rows(x_ref, g_ref, sh_ref, sc_ref, h_ref, row0, n_lat, n_rows):
    def body(r, carry):
        rows = pl.ds(pl.multiple_of(r * ROW_CHUNK, ROW_CHUNK), ROW_CHUNK)
        xf = x_ref[rows, :]
        ms = jnp.mean(xf * xf, axis=-1, keepdims=True)
        y = xf * lax.rsqrt(ms + EPS) * g_ref[...]
        rid = row0 + r * ROW_CHUNK + lax.broadcasted_iota(jnp.int32, (ROW_CHUNK, 1), 0)
        is_ctx = rid >= n_lat
        sc = jnp.where(is_ctx, sc_ref[1:2, :], sc_ref[0:1, :])
        sh = jnp.where(is_ctx, sh_ref[1:2, :], sh_ref[0:1, :])
        h_ref[rows, :] = (y * (1.0 + sc) + sh).astype(h_ref.dtype)
        return carry

    lax.fori_loop(0, n_rows // ROW_CHUNK, body, 0)


def _modproj_kernel(x_ref, g_ref, sh_ref, sc_ref, w_ref, o_ref, h_ref, *, n_lat, tm):
    i = pl.program_id(0)

    @pl.when(pl.program_id(1) == 0)
    def _():
        _modulate_rows(x_ref, g_ref, sh_ref, sc_ref, h_ref, i * tm, n_lat, tm)

    o_ref[...] = jnp.dot(h_ref[...], w_ref[...], preferred_element_type=F32).astype(o_ref.dtype)


def _modproj(x, g, sh, sc, w_all, layer, n_lat):
    t, d = x.shape
    n = w_all.shape[2]
    tm = _tile(t, 528, 16)
    tn = _tile(n, 512, LANES)
    return pl.pallas_call(
        functools.partial(_modproj_kernel, n_lat=n_lat, tm=tm),
        grid=(t // tm, n // tn),
        in_specs=[
            pl.BlockSpec((tm, d), lambda i, j: (i, 0)),
            pl.BlockSpec((1, d), lambda i, j: (0, 0)),
            pl.BlockSpec((2, d), lambda i, j: (0, 0)),
            pl.BlockSpec((2, d), lambda i, j: (0, 0)),
            pl.BlockSpec((None, d, tn), lambda i, j: (layer, 0, j)),
        ],
        out_specs=pl.BlockSpec((tm, tn), lambda i, j: (i, j)),
        out_shape=jax.ShapeDtypeStruct((t, n), BF16),
        scratch_shapes=[pltpu.VMEM((tm, d), BF16)],
        compiler_params=_params(("parallel", "arbitrary")),
        name="modproj",
    )(x, g.reshape(1, d), sh, sc, w_all)


def _qkprep_kernel(p_ref, cos_ref, sin_ref, qg_ref, kg_ref, q_ref, k_ref, *, n_q, n_kv):
    cos = cos_ref[...]
    sin = sin_ref[...]
    lane = lax.broadcasted_iota(jnp.int32, (1, HEAD_DIM), 1)
    first_half = (lane % (HEAD_DIM // 2)) < (HEAD_DIM // 4)

    def head(h, gain, scale):
        xh = p_ref[:, h * HEAD_DIM:(h + 1) * HEAD_DIM].astype(F32)
        ms = jnp.mean(xh * xh, axis=-1, keepdims=True)
        y = xh * lax.rsqrt(ms + EPS) * gain
        partner = jnp.where(first_half,
                            pltpu.roll(y, HEAD_DIM - HEAD_DIM // 4, 1),
                            pltpu.roll(y, HEAD_DIM // 4, 1))
        return (y * cos + partner * sin) * scale

    for h in range(n_q):
        q_ref[h] = head(h, qg_ref[...], ATTN_SCALE * LOG2_E).astype(q_ref.dtype)
    for h in range(n_kv):
        k_ref[h] = head(n_q + h, kg_ref[...], 1.0).astype(k_ref.dtype)


def _qkprep(px, cos, sin, qg, kg, n_q, n_kv):
    t = px.shape[0]
    tr = _tile(t, 264, 8)
    width = (n_q + n_kv) * HEAD_DIM
    return pl.pallas_call(
        functools.partial(_qkprep_kernel, n_q=n_q, n_kv=n_kv),
        grid=(t // tr,),
        in_specs=[
            pl.BlockSpec((tr, width), lambda i: (i, 0)),
            pl.BlockSpec((tr, HEAD_DIM), lambda i: (i, 0)),
            pl.BlockSpec((tr, HEAD_DIM), lambda i: (i, 0)),
            pl.BlockSpec((1, HEAD_DIM), lambda i: (0, 0)),
            pl.BlockSpec((1, HEAD_DIM), lambda i: (0, 0)),
        ],
        out_specs=[
            pl.BlockSpec((n_q, tr, HEAD_DIM), lambda i: (0, i, 0)),
            pl.BlockSpec((n_kv, tr, HEAD_DIM), lambda i: (0, i, 0)),
        ],
        out_shape=[
            jax.ShapeDtypeStruct((n_q, t, HEAD_DIM), BF16),
            jax.ShapeDtypeStruct((n_kv, t, HEAD_DIM), BF16),
        ],
        compiler_params=_params(("parallel",)),
        name="qkprep",
    )(px, cos, sin, qg.reshape(1, HEAD_DIM), kg.reshape(1, HEAD_DIM))


def _rope_tables(n_lat, n_ctx):
    n = HEAD_DIM // 4
    inv = ROPE_THETA ** (-np.arange(n, dtype=np.float32) / n)
    t = np.arange(n_lat)
    rows = (t // GRID_W).astype(np.float32)
    cols = (t % GRID_W).astype(np.float32)
    ang = np.concatenate([rows[:, None] * inv[None, :]] * 2 + [cols[:, None] * inv[None, :]] * 2, axis=1)
    ang = jnp.asarray(ang.astype(np.float32))
    sign = np.tile(np.concatenate([-np.ones(n, np.float32), np.ones(n, np.float32)]), 2)
    cos = jnp.concatenate([jnp.cos(ang), jnp.ones((n_ctx, HEAD_DIM), F32)], axis=0)
    sin = jnp.concatenate([jnp.sin(ang) * sign[None, :], jnp.zeros((n_ctx, HEAD_DIM), F32)], axis=0)
    return cos, sin


def _attn_kernel(*refs, tq, tk, n_kv, aliased):
    if aliased:
        q_ref, k_ref, v_ref, _, o_ref, s_ref, m_ref, l_ref, acc_ref = refs
    else:
        q_ref, k_ref, v_ref, o_ref, s_ref, m_ref, l_ref, acc_ref = refs
    n_chunks = n_kv // tk
    m_ref[...] = jnp.full(m_ref.shape, -jnp.inf, F32)
    l_ref[...] = jnp.zeros(l_ref.shape, F32)
    acc_ref[...] = jnp.zeros(acc_ref.shape, F32)

    def scores(r, ks):
        return lax.dot_general(q_ref[r], k_ref[0, ks, :], (((1,), (1,)), ((), ())), preferred_element_type=F32)

    s_ref[0] = scores(0, pl.ds(0, tk))

    def body(c, carry):
        ks = pl.ds(pl.multiple_of(c * tk, tk), tk)
        kn = pl.ds(pl.multiple_of(jnp.minimum(c + 1, n_chunks - 1) * tk, tk), tk)
        v = v_ref[ks, :]
        for r in range(KV_REP):
            cur, nxt = r % 2, (r + 1) % 2
            s_ref[nxt] = scores(r + 1, ks) if r + 1 < KV_REP else scores(0, kn)
            s = s_ref[cur]
            m_prev = m_ref[r]
            m_new = jnp.maximum(m_prev, jnp.max(s, axis=-1, keepdims=True))
            alpha = jnp.exp2(m_prev - m_new)
            p = jnp.exp2(s - m_new)
            l_ref[r] = alpha * l_ref[r] + jnp.sum(p, axis=-1, keepdims=True)
            acc_ref[r] = alpha * acc_ref[r] + jnp.dot(p.astype(BF16), v, preferred_element_type=F32)
            m_ref[r] = m_new
        return carry

    lax.fori_loop(0, n_chunks, body, 0)
    for r in range(KV_REP):
        o_ref[:, r * HEAD_DIM:(r + 1) * HEAD_DIM] = (acc_ref[r] / l_ref[r]).astype(o_ref.dtype)


def _attention(q, k, px, v_col_block, q_row0, n_q_rows, kv_row0, n_kv, tq, tk, out=None):
    n_heads, t, _ = q.shape
    n_kv_heads = k.shape[0]
    assert q_row0 % tq == 0 and n_q_rows % tq == 0 and kv_row0 % n_kv == 0 and n_kv % tk == 0
    qb0 = q_row0 // tq
    kb0 = kv_row0 // n_kv
    aliased = out is not None
    in_specs = [
        pl.BlockSpec((KV_REP, tq, HEAD_DIM), lambda g, i: (g, qb0 + i, 0)),
        pl.BlockSpec((1, n_kv, HEAD_DIM), lambda g, i: (g, kb0, 0)),
        pl.BlockSpec((n_kv, HEAD_DIM), lambda g, i: (kb0, v_col_block + g)),
    ]
    args = [q, k, px]
    if aliased:
        in_specs.append(pl.BlockSpec(memory_space=pl.ANY))
        args.append(out)
    return pl.pallas_call(
        functools.partial(_attn_kernel, tq=tq, tk=tk, n_kv=n_kv, aliased=aliased),
        grid=(n_kv_heads, n_q_rows // tq),
        in_specs=in_specs,
        out_specs=pl.BlockSpec((tq, KV_REP * HEAD_DIM), lambda g, i: (qb0 + i, g)),
        out_shape=jax.ShapeDtypeStruct((t, n_heads * HEAD_DIM), BF16),
        scratch_shapes=[
            pltpu.VMEM((2, tq, tk), F32),
            pltpu.VMEM((KV_REP, tq, 1), F32),
            pltpu.VMEM((KV_REP, tq, 1), F32),
            pltpu.VMEM((KV_REP, tq, HEAD_DIM), F32),
        ],
        input_output_aliases={3: 0} if aliased else {},
        compiler_params=_params(("parallel", "arbitrary")),
        name="attention",
    )(*args)


def _gmlp_kernel(p_ref, lng_ref, lnb_ref, ws_ref, bst_ref, o_ref, *, n_chunks, d_g):
    n_groups = d_g // GMLP_GROUP_W
    for b in range(n_chunks):
        rows = slice(b * GMLP_CHUNK, (b + 1) * GMLP_CHUNK)
        uv = jax.nn.gelu(p_ref[rows, :].astype(F32))
        u = uv[:, :d_g]
        v = uv[:, d_g:]
        vc = v - jnp.mean(v, axis=-1, keepdims=True)
        var = jnp.mean(vc * vc, axis=-1, keepdims=True)
        vn = (vc * lax.rsqrt(var + LN_EPS) * lng_ref[...] + lnb_ref[...]).astype(BF16)
        for g in range(n_groups):
            cols = slice(g * GMLP_GROUP_W, (g + 1) * GMLP_GROUP_W)
            vm = jnp.dot(ws_ref[g].astype(BF16), vn[:, cols], preferred_element_type=F32) + bst_ref[:, g:g + 1]
            o_ref[rows, cols] = (u[:, cols] * vm).astype(o_ref.dtype)


def _gmlp(px, col_block, d_g, ln_g, ln_b, ws, bs):
    t = px.shape[0]
    n_chunks = 2
    tr = n_chunks * GMLP_CHUNK
    assert t % tr == 0
    n_groups = d_g // GMLP_GROUP_W
    return pl.pallas_call(
        functools.partial(_gmlp_kernel, n_chunks=n_chunks, d_g=d_g),
        grid=(t // tr,),
        in_specs=[
            pl.BlockSpec((tr, 2 * d_g), lambda i: (i, col_block)),
            pl.BlockSpec((1, d_g), lambda i: (0, 0)),
            pl.BlockSpec((1, d_g), lambda i: (0, 0)),
            pl.BlockSpec((n_groups, GMLP_CHUNK, GMLP_CHUNK), lambda i: (0, 0, 0)),
            pl.BlockSpec((GMLP_CHUNK, n_groups), lambda i: (0, 0)),
        ],
        out_specs=pl.BlockSpec((tr, d_g), lambda i: (i, 0)),
        out_shape=jax.ShapeDtypeStruct((t, d_g), BF16),
        compiler_params=_params(("parallel",)),
        name="gmlp",
    )(px, ln_g.reshape(1, d_g), ln_b.reshape(1, d_g), ws, bs.T)


def _merge_kernel(hy_ref, at_ref, gm_ref, gc_ref, wbr_ref, gup_ref, gb_ref, o_ref, *, d_hy, d_at):
    gc = gc_ref[...]
    bounds = (0, d_hy, d_hy + d_at, wbr_ref.shape[0])
    merged = None
    for b, a_ref in enumerate((hy_ref, at_ref, gm_ref)):
        y = jnp.dot(a_ref[...], wbr_ref[bounds[b]:bounds[b + 1], :], preferred_element_type=F32)
        gate = jax.nn.sigmoid(jnp.dot(gc, gup_ref[b], preferred_element_type=F32) + gb_ref[b:b + 1, :])
        merged = gate * y if merged is None else merged + gate * y
    o_ref[...] = merged.astype(o_ref.dtype)


def _merge(o_hy, o_at, o_gm, px, gc_col_block, gate_rank, w_br, gate_up, gate_b, layer):
    t = px.shape[0]
    d_hy, d_at, d_gm = o_hy.shape[1], o_at.shape[1], o_gm.shape[1]
    d = w_br.shape[2]
    tm = _tile(t, 528, 16)
    tn = _tile(d, 512, LANES)
    return pl.pallas_call(
        functools.partial(_merge_kernel, d_hy=d_hy, d_at=d_at),
        grid=(t // tm, d // tn),
        in_specs=[
            pl.BlockSpec((tm, d_hy), lambda i, j: (i, 0)),
            pl.BlockSpec((tm, d_at), lambda i, j: (i, 0)),
            pl.BlockSpec((tm, d_gm), lambda i, j: (i, 0)),
            pl.BlockSpec((tm, gate_rank), lambda i, j: (i, gc_col_block)),
            pl.BlockSpec((None, d_hy + d_at + d_gm, tn), lambda i, j: (layer, 0, j)),
            pl.BlockSpec((None, N_BRANCH, gate_rank, tn), lambda i, j: (layer, 0, 0, j)),
            pl.BlockSpec((None, N_BRANCH, tn), lambda i, j: (layer, 0, j)),
        ],
        out_specs=pl.BlockSpec((tm, tn), lambda i, j: (i, j)),
        out_shape=jax.ShapeDtypeStruct((t, d), BF16),
        compiler_params=_params(("parallel", "arbitrary")),
        name="merge",
    )(o_hy, o_at, o_gm, px, w_br, gate_up, gate_b)


def _outproj_kernel(a_ref, w_ref, x_ref, gate_ref, o_ref, *, n_lat, tm):
    y = jnp.dot(a_ref[...], w_ref[...], preferred_element_type=F32)
    rid = pl.program_id(0) * tm + lax.broadcasted_iota(jnp.int32, (tm, 1), 0)
    gate = jnp.where(rid >= n_lat, gate_ref[1:2, :], gate_ref[0:1, :])
    o_ref[...] = x_ref[...] + gate * y


def _outproj(a, w_all, layer, x, gate, n_lat):
    t, k = a.shape
    d = w_all.shape[2]
    tm = _tile(t, 528, 16)
    tn = _tile(d, 512, LANES)
    return pl.pallas_call(
        functools.partial(_outproj_kernel, n_lat=n_lat, tm=tm),
        grid=(t // tm, d // tn),
        in_specs=[
            pl.BlockSpec((tm, k), lambda i, j: (i, 0)),
            pl.BlockSpec((None, k, tn), lambda i, j: (layer, 0, j)),
            pl.BlockSpec((tm, tn), lambda i, j: (i, j)),
            pl.BlockSpec((2, tn), lambda i, j: (0, j)),
        ],
        out_specs=pl.BlockSpec((tm, tn), lambda i, j: (i, j)),
        out_shape=jax.ShapeDtypeStruct((t, d), F32),
        compiler_params=_params(("parallel", "arbitrary")),
        name="outproj",
    )(a, w_all, x, gate)


def _router_kernel(x_ref, g_ref, sh_ref, sc_ref, rw_ref, rb_ref, h_ref, lg_ref, *, n_lat, tm):
    _modulate_rows(x_ref, g_ref, sh_ref, sc_ref, h_ref, pl.program_id(0) * tm, n_lat, tm)
    lg_ref[...] = jnp.dot(h_ref[...].astype(F32), rw_ref[...], preferred_element_type=F32,
                          precision=HIGHEST) + rb_ref[...]


def _router(x, g, sh, sc, rw, rb, n_lat):
    t, d = x.shape
    tm = _tile(t, 264, 16)
    return pl.pallas_call(
        functools.partial(_router_kernel, n_lat=n_lat, tm=tm),
        grid=(t // tm,),
        in_specs=[
            pl.BlockSpec((tm, d), lambda i: (i, 0)),
            pl.BlockSpec((1, d), lambda i: (0, 0)),
            pl.BlockSpec((2, d), lambda i: (0, 0)),
            pl.BlockSpec((2, d), lambda i: (0, 0)),
            pl.BlockSpec((d, ROUTER_COLS), lambda i: (0, 0)),
            pl.BlockSpec((1, ROUTER_COLS), lambda i: (0, 0)),
        ],
        out_specs=[
            pl.BlockSpec((tm, d), lambda i: (i, 0)),
            pl.BlockSpec((tm, ROUTER_COLS), lambda i: (i, 0)),
        ],
        out_shape=[
            jax.ShapeDtypeStruct((t, d), BF16),
            jax.ShapeDtypeStruct((t, ROUTER_COLS), F32),
        ],
        compiler_params=_params(("parallel",)),
        name="router",
    )(x, g.reshape(1, d), sh, sc, rw, rb)


def _experts_kernel(te_ref, tb_ref, tf_ref, nu_ref, x_ref, rw_ref, wg_ref, wu_ref, wd_ref, o_ref,
                    wg_s, wu_s, wd_s):
    i = pl.program_id(0)

    @pl.when(i < nu_ref[0])
    def _():
        @pl.when(tf_ref[i] == 1)
        def _():
            wg_s[...] = wg_ref[...].astype(BF16)
            wu_s[...] = wu_ref[...].astype(BF16)
            wd_s[...] = wd_ref[...].astype(BF16)

        xg = x_ref[...]
        hg = jnp.dot(xg, wg_s[...], preferred_element_type=F32)
        hu = jnp.dot(xg, wu_s[...], preferred_element_type=F32)
        hid = (jax.nn.silu(hg) * hu * rw_ref[...]).astype(BF16)
        o_ref[...] = jnp.dot(hid, wd_s[...], preferred_element_type=F32).astype(o_ref.dtype)


def _experts(xg, roww, tile_expert, tile_block, tile_first, n_used, w_gate, w_up, w_down, layer, tme):
    p, d = xg.shape
    f = w_gate.shape[3]
    n_tiles = p // tme
    grid_spec = pltpu.PrefetchScalarGridSpec(
        num_scalar_prefetch=4,
        grid=(n_tiles,),
        in_specs=[
            pl.BlockSpec((tme, d), lambda i, te, tb, tf, nu: (tb[i], 0)),
            pl.BlockSpec((tme, 1), lambda i, te, tb, tf, nu: (tb[i], 0)),
            pl.BlockSpec((None, None, d, f), lambda i, te, tb, tf, nu: (layer, te[i], 0, 0)),
            pl.BlockSpec((None, None, d, f), lambda i, te, tb, tf, nu: (layer, te[i], 0, 0)),
            pl.BlockSpec((None, None, f, d), lambda i, te, tb, tf, nu: (layer, te[i], 0, 0)),
        ],
        out_specs=pl.BlockSpec((tme, d), lambda i, te, tb, tf, nu: (tb[i], 0)),
        scratch_shapes=[
            pltpu.VMEM((d, f), BF16),
            pltpu.VMEM((d, f), BF16),
            pltpu.VMEM((f, d), BF16),
        ],
    )
    return pl.pallas_call(
        _experts_kernel,
        grid_spec=grid_spec,
        out_shape=jax.ShapeDtypeStruct((p, d), BF16),
        compiler_params=_params(("arbitrary",)),
        name="experts",
    )(tile_expert, tile_block, tile_first, n_used, xg, roww, w_gate, w_up, w_down)


def _combine_kernel(x_ref, y0_ref, y1_ref, gate_ref, o_ref, *, n_lat, tm):
    rid = pl.program_id(0) * tm + lax.broadcasted_iota(jnp.int32, (tm, 1), 0)
    gate = jnp.where(rid >= n_lat, gate_ref[1:2, :], gate_ref[0:1, :])
    o_ref[...] = x_ref[...] + gate * (y0_ref[...].astype(F32) + y1_ref[...].astype(F32))


def _combine(x, y0, y1, gate, n_lat):
    t, d = x.shape
    tm = _tile(t, 264, 16)
    row = lambda: pl.BlockSpec((tm, d), lambda i: (i, 0))
    return pl.pallas_call(
        functools.partial(_combine_kernel, n_lat=n_lat, tm=tm),
        grid=(t // tm,),
        in_specs=[row(), row(), row(), pl.BlockSpec((2, d), lambda i: (0, 0))],
        out_specs=row(),
        out_shape=jax.ShapeDtypeStruct((t, d), F32),
        compiler_params=_params(("parallel",)),
        name="moe_combine",
    )(x, y0, y1, gate)


def _route(logits, n_groups, per_group):
    lg = logits[:, :n_groups]
    le_all = logits[:, n_groups:n_groups + n_groups * per_group].reshape(-1, n_groups, per_group)
    grp = jnp.argmax(lg, axis=-1)
    p_grp = jnp.take_along_axis(jax.nn.softmax(lg, axis=-1), grp[:, None], axis=-1)
    le = jnp.take_along_axis(le_all, grp[:, None, None], axis=1)[:, 0]
    top_p, top_i = lax.top_k(jax.nn.softmax(le, axis=-1), TOP_K)
    w = top_p / jnp.sum(top_p, axis=-1, keepdims=True) * p_grp
    eid = grp[:, None] * per_group + top_i
    return eid.astype(jnp.int32), w


def _dispatch(eid, w, n_experts, tme):
    t = eid.shape[0]
    n_pairs = t * TOP_K
    p_rows = n_pairs + n_experts * tme
    p_rows = -(-p_rows // tme) * tme
    flat_e = eid.reshape(-1)
    onehot = (flat_e[:, None] == jnp.arange(n_experts, dtype=jnp.int32)[None, :]).astype(jnp.int32)
    csum = jnp.cumsum(onehot, axis=0)
    rank = jnp.take_along_axis(csum, flat_e[:, None], axis=1)[:, 0] - 1
    counts = csum[-1]
    padded = -(-counts // tme) * tme
    ends = jnp.cumsum(padded)
    starts = ends - padded
    dest = starts[flat_e] + rank
    src_tok = (jnp.arange(p_rows, dtype=jnp.int32) % t).at[dest].set(jnp.arange(n_pairs, dtype=jnp.int32) // TOP_K)
    roww = jnp.zeros((p_rows,), F32).at[dest].set(w.reshape(-1))
    n_used = (ends[-1] // tme).astype(jnp.int32)
    tile_ids = jnp.arange(p_rows // tme, dtype=jnp.int32)
    tile_block = jnp.minimum(tile_ids, n_used - 1)
    tile_expert = jnp.minimum(
        jnp.searchsorted(ends, tile_block * tme, side="right").astype(jnp.int32), n_experts - 1)
    tile_first = (tile_block * tme == starts[tile_expert]).astype(jnp.int32)
    return src_tok, roww.reshape(-1, 1), dest.reshape(t, TOP_K), tile_expert, tile_block, tile_first, n_used.reshape(1)


def _rmsnorm_kernel(x_ref, g_ref, o_ref):
    xf = x_ref[...]
    ms = jnp.mean(xf * xf, axis=-1, keepdims=True)
    o_ref[...] = xf * lax.rsqrt(ms + EPS) * g_ref[...]


def _final_norm(x, g, n_rows):
    d = x.shape[1]
    tr = _tile(n_rows, 256, 8)
    return pl.pallas_call(
        _rmsnorm_kernel,
        grid=(n_rows // tr,),
        in_specs=[pl.BlockSpec((tr, d), lambda i: (i, 0)), pl.BlockSpec((1, d), lambda i: (0, 0))],
        out_specs=pl.BlockSpec((tr, d), lambda i: (i, 0)),
        out_shape=jax.ShapeDtypeStruct((n_rows, d), F32),
        compiler_params=_params(("parallel",)),
        name="final_norm",
    )(x, g.reshape(1, d))


SUBLANES = 8


def _filter_features(length, emb):
    bands = (emb - 1) // 2
    i = np.arange(length, dtype=np.float64)
    t = i / (length - 1)
    f = np.linspace(1e-4, bands - 1, bands)
    w = 2.0 * np.pi * i / length
    z = np.concatenate([t[:, None], np.cos(-f[None, :] * w[:, None]), np.sin(-f[None, :] * w[:, None])], axis=1)
    z = np.concatenate([z, z[::-1]], axis=0)
    emb_pad = -(-emb // SUBLANES) * SUBLANES
    z = np.pad(z, ((0, 0), (0, emb_pad - emb)))
    return jnp.asarray(z.astype(np.float32))


def _filter_kernel(z_ref, w1_ref, b1_ref, w2_ref, b2_ref, w3_ref, b3_ref, w4_ref, fr_ref, dl_ref,
                   k_ref, nrm_ref, *, d_hy):
    dot = functools.partial(jnp.dot, preferred_element_type=F32, precision=HIGHEST)

    @pl.when(jnp.logical_and(pl.program_id(0) == 0, pl.program_id(1) == 0))
    def _():
        nrm_ref[...] = jnp.zeros(nrm_ref.shape, F32)

    z = z_ref[...]
    fr = fr_ref[...]
    h = jnp.sin(fr * (dot(z, w1_ref[...]) + b1_ref[...]))
    h = jnp.sin(fr * (dot(h, w2_ref[...]) + b2_ref[...]))
    h = jnp.sin(fr * (dot(h, w3_ref[...]) + b3_ref[...]))
    hk = dot(h, w4_ref[...])
    window = jnp.exp(-z[:, 0:1] * dl_ref[...]) + HYENA_SHIFT
    for o in range(HYENA_ORDER):
        k = hk[:, o * d_hy:(o + 1) * d_hy] * window
        k_ref[o] = k.astype(k_ref.dtype)
        nrm_ref[o:o + 1, :] += jnp.sum(jnp.abs(k), axis=0, keepdims=True)


def _hyena_filters(length, w1, b1, w2, b2, w3, b3, w4, freq, d_hy, out_dtype):
    emb, hidden = w1.shape
    z = _filter_features(length, emb)
    emb_pad = z.shape[1]
    w1p = jnp.pad(w1, ((0, emb_pad - emb), (0, 0)))
    max_decay = math.log(HYENA_TARGET) / HYENA_FAST_PCT
    min_decay = math.log(HYENA_TARGET) / HYENA_SLOW_PCT
    deltas = jnp.asarray(np.abs(np.linspace(min_decay, max_decay, d_hy)).astype(np.float32)).reshape(1, d_hy)
    tr = _tile(length, 512, SUBLANES)
    per = length // tr
    vec = lambda a: a.reshape(1, hidden)
    full = lambda shape: pl.BlockSpec(shape, lambda d, i: (0,) * len(shape))
    return pl.pallas_call(
        functools.partial(_filter_kernel, d_hy=d_hy),
        grid=(2, per),
        in_specs=[
            pl.BlockSpec((tr, emb_pad), lambda d, i: (d * per + i, 0)),
            full((emb_pad, hidden)), full((1, hidden)),
            full((hidden, hidden)), full((1, hidden)),
            full((hidden, hidden)), full((1, hidden)),
            pl.BlockSpec((hidden, HYENA_ORDER * d_hy), lambda d, i: (0, d)),
            full((1, hidden)), full((1, d_hy)),
        ],
        out_specs=[
            pl.BlockSpec((HYENA_ORDER, tr, d_hy), lambda d, i: (0, d * per + i, 0)),
            pl.BlockSpec((HYENA_ORDER, d_hy), lambda d, i: (0, 0)),
        ],
        out_shape=[
            jax.ShapeDtypeStruct((HYENA_ORDER, 2 * length, d_hy), out_dtype),
            jax.ShapeDtypeStruct((HYENA_ORDER, d_hy), F32),
        ],
        compiler_params=_params(("arbitrary", "arbitrary")),
        name="hyena_filter",
    )(z, w1p, vec(b1), w2, vec(b2), w3, vec(b3), w4, vec(freq), deltas)


CONV_ROWS = 64
CONV_HALO = 8


def _shortconv_kernel(p_ref, w_ref, b_ref, lat_ref, ctx_ref, pad_ref, *, n_lat, n_ctx):
    t = n_lat + n_ctx
    tc = p_ref.shape[1]
    pad_ref[0:CONV_HALO, :] = jnp.zeros((CONV_HALO, tc), F32)
    pad_ref[t + CONV_HALO:t + 2 * CONV_HALO, :] = jnp.zeros((CONV_HALO, tc), F32)

    def fill(r, carry):
        src = pl.ds(pl.multiple_of(r * CONV_ROWS, CONV_ROWS), CONV_ROWS)
        dst = pl.ds(pl.multiple_of(r * CONV_ROWS + CONV_HALO, CONV_HALO), CONV_ROWS)
        pad_ref[dst, :] = p_ref[src, :].astype(F32)
        return carry

    lax.fori_loop(0, t // CONV_ROWS, fill, 0)
    w0, w1, w2, bias = w_ref[0:1, :], w_ref[1:2, :], w_ref[2:3, :], b_ref[...]
    ext_rows = CONV_ROWS + 2 * CONV_HALO
    inner = slice(CONV_HALO, CONV_HALO + CONV_ROWS)

    def segment(out_ref, row0, n_rows):
        def body(r, carry):
            ext = pad_ref[pl.ds(pl.multiple_of(row0 + r * CONV_ROWS, CONV_HALO), ext_rows), :]
            rid = r * CONV_ROWS + lax.broadcasted_iota(jnp.int32, (CONV_ROWS, 1), 0)
            prev = jnp.where(rid == 0, 0.0, pltpu.roll(ext, 1, 0)[inner])
            nxt = jnp.where(rid == n_rows - 1, 0.0, pltpu.roll(ext, ext_rows - 1, 0)[inner])
            y = prev * w0 + ext[inner] * w1 + nxt * w2 + bias
            out_ref[0, pl.ds(pl.multiple_of(r * CONV_ROWS, CONV_ROWS), CONV_ROWS), :] = y.astype(out_ref.dtype)
            return carry

        lax.fori_loop(0, n_rows // CONV_ROWS, body, 0)

    segment(lat_ref, 0, n_lat)
    segment(ctx_ref, n_lat, n_ctx)


def _shortconv(px, col0, d_hy, conv_w, conv_b, n_lat, n_ctx):
    t = px.shape[0]
    tc = _tile(d_hy, 256, LANES)
    assert col0 % tc == 0 and n_lat % CONV_ROWS == 0 and n_ctx % CONV_ROWS == 0
    per = d_hy // tc
    n_grp = HYENA_ORDER + 1
    return pl.pallas_call(
        functools.partial(_shortconv_kernel, n_lat=n_lat, n_ctx=n_ctx),
        grid=(n_grp, per),
        in_specs=[
            pl.BlockSpec((t, tc), lambda g, j: (0, col0 // tc + g * per + j)),
            pl.BlockSpec((3, tc), lambda g, j: (0, g * per + j)),
            pl.BlockSpec((1, tc), lambda g, j: (0, g * per + j)),
        ],
        out_specs=[
            pl.BlockSpec((1, n_lat, tc), lambda g, j: (g, 0, j)),
            pl.BlockSpec((1, n_ctx, tc), lambda g, j: (g, 0, j)),
        ],
        out_shape=[
            jax.ShapeDtypeStruct((n_grp, n_lat, d_hy), BF16),
            jax.ShapeDtypeStruct((n_grp, n_ctx, d_hy), BF16),
        ],
        scratch_shapes=[pltpu.VMEM((t + 2 * CONV_HALO, tc), F32)],
        compiler_params=_params(("parallel", "parallel")),
        name="shortconv",
    )(px, conv_w, conv_b.reshape(1, -1))


DFT_N2 = 128
K1_ALIGN = 16
T2_BLOCK = 16
STAGE2_UNROLL = 4


def _dft_constants(length):
    n = 2 * length
    n1 = n // DFT_N2
    assert n1 * DFT_N2 == n and n1 % 2 == 0
    n_t1 = n1 // 2
    k1_used = n1 // 2 + 1
    k1p = -(-k1_used // K1_ALIGN) * K1_ALIGN
    k1 = np.arange(k1p, dtype=np.float64)
    live = (k1 < k1_used).astype(np.float64)
    t1 = np.arange(n1, dtype=np.float64)
    ang1 = -2.0 * np.pi * k1[:, None] * t1[None, :] / n1
    f1 = np.stack([np.cos(ang1), np.sin(ang1)]) * live[None, :, None]
    k2 = np.arange(DFT_N2, dtype=np.float64)
    t2 = np.arange(DFT_N2, dtype=np.float64)
    ang2 = -2.0 * np.pi * t2[None, None, :] * (k1[:, None, None] + n1 * k2[None, :, None]) / n
    gr = np.cos(ang2) * live[:, None, None]
    gi = np.sin(ang2) * live[:, None, None]
    g = np.stack([gr, gi, gr.transpose(0, 2, 1), gi.transpose(0, 2, 1)])
    weight = np.where((k1 == 0) | (k1 == n1 // 2), 1.0, 2.0) * live
    angc = 2.0 * np.pi * np.arange(n_t1, dtype=np.float64)[:, None] * k1[None, :] / n1
    cinv = np.stack([np.cos(angc), np.sin(angc)]) * weight[None, None, :] / n
    as_bf16 = lambda a: jnp.asarray(a.astype(np.float32)).astype(BF16)
    eye = np.eye(T2_BLOCK)
    kron = lambda m: np.stack([np.kron(m[0], eye), np.kron(m[1], eye)])
    return dict(n1=n1, n_t1=n_t1, k1p=k1p, f1=as_bf16(kron(f1)), f1_sig=as_bf16(kron(f1[:, :, :n_t1])),
                g=as_bf16(g), cinv=as_bf16(kron(cinv)))


def _dft_stage1_kernel(f_ref, u_ref, a_ref):
    k, r, tc = u_ref.shape
    u = u_ref[...].reshape(k * r, tc)
    for part in range(2):
        a = jnp.dot(f_ref[part], u, preferred_element_type=F32).astype(a_ref.dtype)
        a_ref[part] = a.reshape(a_ref.shape[1:])


def _dft_stage1(f, u4, group):
    _, k, n2, c = u4.shape
    k1p = f.shape[1] // T2_BLOCK
    tc = _tile(c, 256, LANES)
    return pl.pallas_call(
        _dft_stage1_kernel,
        grid=(n2 // T2_BLOCK, c // tc),
        in_specs=[
            pl.BlockSpec((2, k1p * T2_BLOCK, k * T2_BLOCK), lambda i, j: (0, 0, 0)),
            pl.BlockSpec((None, k, T2_BLOCK, tc), lambda i, j: (group, 0, i, j)),
        ],
        out_specs=pl.BlockSpec((2, k1p, T2_BLOCK, tc), lambda i, j: (0, 0, i, j)),
        out_shape=jax.ShapeDtypeStruct((2, k1p, n2, c), BF16),
        compiler_params=_params(("parallel", "parallel")),
        name="dft_stage1",
    )(f, u4)


def _dft_stage2_filter_kernel(a_ref, g_ref, scale_ref, kf_ref, *, k1c):
    scale = scale_ref[...]

    def body(i, carry):
        ar, ai = a_ref[0, i], a_ref[1, i]
        gr, gi = g_ref[0, i], g_ref[1, i]
        dot = functools.partial(jnp.dot, preferred_element_type=F32)
        kf_ref[0, i] = (dot(gr, ar) - dot(gi, ai)) * scale
        kf_ref[1, i] = (dot(gr, ai) + dot(gi, ar)) * scale
        return carry

    lax.fori_loop(0, k1c, body, 0, unroll=STAGE2_UNROLL)


def _dft_stage2_conv_kernel(a_ref, g_ref, kf_ref, b_ref, *, k1c):
    def body(i, carry):
        ar, ai = a_ref[0, i], a_ref[1, i]
        gr, gi = g_ref[0, i], g_ref[1, i]
        dot = functools.partial(jnp.dot, preferred_element_type=F32)
        xr = dot(gr, ar) - dot(gi, ai)
        xi = dot(gr, ai) + dot(gi, ar)
        kr, ki = kf_ref[0, i], kf_ref[1, i]
        zr = (xr * kr - xi * ki).astype(BF16)
        zi = (xr * ki + xi * kr).astype(BF16)
        grt, git = g_ref[2, i], g_ref[3, i]
        b_ref[0, i] = (dot(grt, zr) + dot(git, zi)).astype(b_ref.dtype)
        b_ref[1, i] = (dot(grt, zi) - dot(git, zr)).astype(b_ref.dtype)
        return carry

    lax.fori_loop(0, k1c, body, 0, unroll=STAGE2_UNROLL)


def _dft_stage2(a4, g, kf=None, scale=None):
    _, k1p, n2, c = a4.shape
    k1c = K1_ALIGN
    tc = _tile(c, 256, LANES)
    a_spec = pl.BlockSpec((2, k1c, n2, tc), lambda i, j: (0, i, 0, j))
    g_spec = pl.BlockSpec((4, k1c, n2, n2), lambda i, j: (0, i, 0, 0))
    if kf is None:
        s_spec = pl.BlockSpec((1, tc), lambda i, j: (0, j))
        body, args, in_specs, dtype = _dft_stage2_filter_kernel, (a4, g, scale), [a_spec, g_spec, s_spec], F32
    else:
        body, args, in_specs, dtype = _dft_stage2_conv_kernel, (a4, g, kf), [a_spec, g_spec, a_spec], BF16
    return pl.pallas_call(
        functools.partial(body, k1c=k1c),
        grid=(k1p // k1c, c // tc),
        in_specs=in_specs,
        out_specs=a_spec,
        out_shape=jax.ShapeDtypeStruct(a4.shape, dtype),
        compiler_params=_params(("parallel", "arbitrary")),
        name="dft_stage2",
    )(*args)


def _dft_stage3_kernel(c_ref, b_ref, u_ref, m_ref, skip_ref, o_ref):
    _, k1p, r, tc = b_ref.shape
    rows = u_ref.shape[0] * r
    y = (jnp.dot(c_ref[0], b_ref[0].reshape(k1p * r, tc), preferred_element_type=F32)
         - jnp.dot(c_ref[1], b_ref[1].reshape(k1p * r, tc), preferred_element_type=F32))
    u = u_ref[...].reshape(rows, tc).astype(F32)
    out = m_ref[...].reshape(rows, tc).astype(F32) * (y + u * skip_ref[...])
    o_ref[0] = out.astype(o_ref.dtype).reshape(o_ref.shape[1:])


def _dft_stage3(cinv, b4, u4, u_group, m4, m_group, skip):
    _, k1p, n2, c = b4.shape
    n_t1 = cinv.shape[1] // T2_BLOCK
    tc = _tile(c, 256, LANES)
    sig_spec = lambda grp: pl.BlockSpec((None, n_t1, T2_BLOCK, tc), lambda i, j: (grp, 0, i, j))
    return pl.pallas_call(
        _dft_stage3_kernel,
        grid=(n2 // T2_BLOCK, c // tc),
        in_specs=[
            pl.BlockSpec((2, n_t1 * T2_BLOCK, k1p * T2_BLOCK), lambda i, j: (0, 0, 0)),
            pl.BlockSpec((2, k1p, T2_BLOCK, tc), lambda i, j: (0, 0, i, j)),
            sig_spec(u_group),
            sig_spec(m_group),
            pl.BlockSpec((1, tc), lambda i, j: (0, j)),
        ],
        out_specs=pl.BlockSpec((1, n_t1, T2_BLOCK, tc), lambda i, j: (0, 0, i, j)),
        out_shape=jax.ShapeDtypeStruct((1, n_t1, n2, c), BF16),
        compiler_params=_params(("parallel", "parallel")),
        name="dft_stage3",
    )(cinv, b4, u4, m4, skip.reshape(1, c))


def _hyena_latent(p_lat, kern, inv_norm, skip, consts):
    _, length, c = p_lat.shape
    n1, n_t1 = consts["n1"], consts["n_t1"]
    p4 = p_lat.reshape(HYENA_ORDER + 1, n_t1, DFT_N2, c)
    k4 = kern.reshape(HYENA_ORDER, n1, DFT_N2, c)
    u4, u_group = p4, 0
    for order in range(HYENA_ORDER):
        kf = _dft_stage2(_dft_stage1(consts["f1"], k4, order), consts["g"], scale=inv_norm[order:order + 1])
        b = _dft_stage2(_dft_stage1(consts["f1_sig"], u4, u_group), consts["g"], kf)
        u4 = _dft_stage3(consts["cinv"], b, u4, u_group, p4, order + 1, skip[order])
        u_group = 0
    return u4.reshape(length, c)


def _ctx_dft_constants(length):
    n = 2 * length
    k = np.arange(n, dtype=np.float64)
    ang = -2.0 * np.pi * k[:, None] * k[None, :] / n
    fwd = np.stack([np.cos(ang), np.sin(ang)])
    inv = np.stack([np.cos(ang[:length]), -np.sin(ang[:length])]) / n
    as_f32 = lambda a: jnp.asarray(a.astype(np.float32))
    return dict(fwd=as_f32(fwd), fwd_sig=as_f32(fwd[:, :, :length]), inv=as_f32(inv))


def _ctxconv_kernel(ff_ref, fs_ref, fi_ref, k_ref, scale_ref, u_ref, m_ref, skip_ref, o_ref):
    dot = functools.partial(jnp.dot, preferred_element_type=F32, precision=HIGHEST)
    k = k_ref[...] * scale_ref[...]
    u = u_ref[...].astype(F32)
    kr, ki = dot(ff_ref[0], k), dot(ff_ref[1], k)
    xr, xi = dot(fs_ref[0], u), dot(fs_ref[1], u)
    zr = xr * kr - xi * ki
    zi = xr * ki + xi * kr
    y = dot(fi_ref[0], zr) - dot(fi_ref[1], zi)
    o_ref[0] = (m_ref[...].astype(F32) * (y + u * skip_ref[...])).astype(o_ref.dtype)


def _ctxconv(consts, kern, order, inv_norm, u3, u_group, m3, m_group, skip):
    _, length, c = u3.shape
    n = 2 * length
    tc = _tile(c, 256, LANES)
    return pl.pallas_call(
        _ctxconv_kernel,
        grid=(c // tc,),
        in_specs=[
            pl.BlockSpec((2, n, n), lambda j: (0, 0, 0)),
            pl.BlockSpec((2, n, length), lambda j: (0, 0, 0)),
            pl.BlockSpec((2, length, n), lambda j: (0, 0, 0)),
            pl.BlockSpec((None, n, tc), lambda j: (order, 0, j)),
            pl.BlockSpec((1, tc), lambda j: (0, j)),
            pl.BlockSpec((None, length, tc), lambda j: (u_group, 0, j)),
            pl.BlockSpec((None, length, tc), lambda j: (m_group, 0, j)),
            pl.BlockSpec((1, tc), lambda j: (0, j)),
        ],
        out_specs=pl.BlockSpec((1, length, tc), lambda j: (0, 0, j)),
        out_shape=jax.ShapeDtypeStruct((1, length, c), BF16),
        compiler_params=_params(("parallel",)),
        name="ctxconv",
    )(consts["fwd"], consts["fwd_sig"], consts["inv"], kern, inv_norm[order:order + 1], u3, m3, skip.reshape(1, c))


def _hyena_context(p_ctx, kern, inv_norm, skip, consts):
    u3, u_group = p_ctx, 0
    for order in range(HYENA_ORDER):
        u3 = _ctxconv(consts, kern, order, inv_norm, u3, u_group, p_ctx, order + 1, skip[order])
        u_group = 0
    return u3[0]


def kernel(x, c, ctx, c_ctx, norm1_g, norm2_g, mod_a, mod_b, mod_bias, w_in, q_norm_g, k_norm_g, hy_conv_w, hy_conv_b, hy_w1, hy_b1, hy_w2, hy_b2, hy_w3, hy_b3, hy_w4, hy_freq, hy_skip, gm_ln_g, gm_ln_b, gm_ws, gm_bs, gate_up, gate_b, w_br, w_out, rt_group_w, rt_group_b, rt_expert_w, rt_expert_b, ex_w_gate, ex_w_up, ex_w_down, final_norm_g):
    bsz, seq, d = x.shape
    assert bsz == 1 and c.shape[0] == 1
    n_ctx = ctx.shape[1]
    depth = w_in.shape[0]
    t_all = seq + n_ctx
    d_hy = hy_skip.shape[2]
    d_gm = gm_ln_g.shape[1]
    d_at = w_br.shape[1] - d_hy - d_gm
    gate_rank = gate_up.shape[2]
    n_q = d_at // HEAD_DIM
    n_kv = n_q // KV_REP
    q_end = d_at
    v_start = q_end + n_kv * HEAD_DIM
    v_end = v_start + n_kv * HEAD_DIM
    hy_end = v_end + (HYENA_ORDER + 1) * d_hy
    gm_end = hy_end + 2 * d_gm
    n_groups = rt_group_w.shape[2]
    per_group = rt_expert_w.shape[3]
    n_experts = n_groups * per_group
    assert n_groups + n_experts <= ROUTER_COLS
    assert hy_end % (2 * d_gm) == 0 and gm_end % gate_rank == 0
    tme = 256

    xs = jnp.concatenate([x[0], ctx[0]], axis=0)
    cond = jnp.stack([c[0], c_ctx], axis=0)
    cos, sin = _rope_tables(seq, n_ctx)

    tq = _tile(seq, 256, 16)
    tk = _tile(t_all, 2816, 256)
    tq_c = _tile(n_ctx, 256, 16)
    dft_lat = _dft_constants(seq)
    dft_ctx = _ctx_dft_constants(n_ctx)
    w_in_b, w_br_b, gate_up_b, w_out_b = (w.astype(BF16) for w in (w_in, w_br, gate_up, w_out))

    for l in range(depth):
        m = (jnp.dot(jnp.dot(jax.nn.silu(cond), mod_a[l], precision=HIGHEST), mod_b[l], precision=HIGHEST)
             + mod_bias[l]).reshape(2, N_MOD, d)

        px = _modproj(xs, norm1_g[l], m[:, 0], m[:, 1], w_in_b, l, seq)
        q, k = _qkprep(px, cos, sin, q_norm_g[l], k_norm_g[l], n_q, n_kv)
        o_at = _attention(q, k, px, v_start // HEAD_DIM, 0, seq, 0, t_all, tq, tk)
        o_at = _attention(q, k, px, v_start // HEAD_DIM, seq, n_ctx, seq, n_ctx, tq_c, n_ctx, out=o_at)

        hy_w = (hy_w1[l], hy_b1[l], hy_w2[l], hy_b2[l], hy_w3[l], hy_b3[l], hy_w4[l], hy_freq[l])
        p_lat, p_ctx = _shortconv(px, v_end, d_hy, hy_conv_w[l], hy_conv_b[l], seq, n_ctx)
        kern_x, norm_x = _hyena_filters(seq, *hy_w, d_hy, BF16)
        kern_c, norm_c = _hyena_filters(n_ctx, *hy_w, d_hy, F32)
        o_hy_x = _hyena_latent(p_lat, kern_x, 1.0 / norm_x, hy_skip[l], dft_lat)
        o_hy_c = _hyena_context(p_ctx, kern_c, 1.0 / norm_c, hy_skip[l], dft_ctx)
        o_hy = jnp.concatenate([o_hy_x, o_hy_c], axis=0)

        o_gm = _gmlp(px, hy_end // (2 * d_gm), d_gm, gm_ln_g[l], gm_ln_b[l], gm_ws[l], gm_bs[l])

        merged = _merge(o_hy, o_at, o_gm, px, gm_end // gate_rank, gate_rank, w_br_b, gate_up_b, gate_b, l)
        xs = _outproj(merged, w_out_b, l, xs, m[:, 2], seq)

        rw = jnp.concatenate([rt_group_w[l]] + [rt_expert_w[l, g] for g in range(n_groups)], axis=1)
        rw = jnp.pad(rw, ((0, 0), (0, ROUTER_COLS - rw.shape[1])))
        rb = jnp.concatenate([rt_group_b[l], rt_expert_b[l].reshape(-1)])
        rb = jnp.pad(rb, (0, ROUTER_COLS - rb.shape[0])).reshape(1, ROUTER_COLS)
        h2, logits = _router(xs, norm2_g[l], m[:, 3], m[:, 4], rw, rb, seq)
        eid, wts = _route(logits, n_groups, per_group)
        src_tok, roww, pos, tile_expert, tile_block, tile_first, n_used = _dispatch(eid, wts, n_experts, tme)
        xg = jnp.take(h2, src_tok, axis=0)
        yo = _experts(xg, roww, tile_expert, tile_block, tile_first, n_used,
                      ex_w_gate, ex_w_up, ex_w_down, l, tme)
        xs = _combine(xs, jnp.take(yo, pos[:, 0], axis=0), jnp.take(yo, pos[:, 1], axis=0), m[:, 5], seq)

    return _final_norm(xs, final_norm_g, seq).reshape(1, seq, d)
```

```python
import functools
import math

import numpy as np
import jax
import jax.numpy as jnp
from jax import lax
from jax.experimental import pallas as pl
from jax.experimental.pallas import tpu as pltpu

F32 = jnp.float32
BF16 = jnp.bfloat16
HIGHEST = lax.Precision.HIGHEST

HEAD_DIM = 128
KV_REP = 4
GRID_W = 64
ROPE_THETA = 10000.0
ATTN_SCALE = HEAD_DIM ** -0.5
LOG2_E = math.log2(math.e)
HYENA_ORDER = 2
HYENA_SHIFT = 0.05
HYENA_FAST_PCT = 0.3
HYENA_SLOW_PCT = 1.5
HYENA_TARGET = 1e-2
GMLP_CHUNK = 128
GMLP_GROUP_W = 128
N_BRANCH = 3
N_MOD = 6
TOP_K = 2
EPS = 1e-6
LN_EPS = 1e-5

LANES = 128
ROUTER_COLS = LANES
VMEM_LIMIT_MB = 56


def _params(semantics, vmem_mb=VMEM_LIMIT_MB):
    return pltpu.CompilerParams(dimension_semantics=semantics, vmem_limit_bytes=vmem_mb * 1024 * 1024)


def _tile(n, target, mult):
    best = None
    for d in range(mult, min(n, target) + 1, mult):
        if n % d == 0:
            best = d
    assert best is not None, (n, target, mult)
    return best


ROW_CHUNK = 16


def _modulate_rows(x_ref, g_ref, sh_ref, sc_ref, h_ref, row0, n_lat, n_rows):
    n_chunks = n_rows // ROW_CHUNK
    lat_chunks = jnp.clip((n_lat - row0) // ROW_CHUNK, 0, n_chunks)

    def segment(seg, lo, hi):
        gain = g_ref[...] * (1.0 + sc_ref[seg:seg + 1, :])
        shift = sh_ref[seg:seg + 1, :]

        def body(r, carry):
            rows = pl.ds(pl.multiple_of(r * ROW_CHUNK, ROW_CHUNK), ROW_CHUNK)
            xf = x_ref[rows, :]
            ms = jnp.mean(xf * xf, axis=-1, keepdims=True)
            h_ref[rows, :] = (xf * lax.rsqrt(ms + EPS) * gain + shift).astype(h_ref.dtype)
            return carry

        lax.fori_loop(lo, hi, body, 0)

    segment(0, 0, lat_chunks)
    segment(1, lat_chunks, n_chunks)


def _modproj_kernel(x_ref, g_ref, sh_ref, sc_ref, w_ref, o_ref, h_ref, *, n_lat, tm):
    i = pl.program_id(0)

    @pl.when(pl.program_id(1) == 0)
    def _():
        _modulate_rows(x_ref, g_ref, sh_ref, sc_ref, h_ref, i * tm, n_lat, tm)

    o_ref[...] = jnp.dot(h_ref[...], w_ref[...], preferred_element_type=F32).astype(o_ref.dtype)


def _modproj(x, g, sh, sc, w_all, layer, n_lat):
    t, d = x.shape
    n = w_all.shape[2]
    tm = _tile(t, 528, 16)
    tn = _tile(n, 512, LANES)
    return pl.pallas_call(
        functools.partial(_modproj_kernel, n_lat=n_lat, tm=tm),
        grid=(t // tm, n // tn),
        in_specs=[
            pl.BlockSpec((tm, d), lambda i, j: (i, 0)),
            pl.BlockSpec((1, d), lambda i, j: (0, 0)),
            pl.BlockSpec((2, d), lambda i, j: (0, 0)),
            pl.BlockSpec((2, d), lambda i, j: (0, 0)),
            pl.BlockSpec((None, d, tn), lambda i, j: (layer, 0, j)),
        ],
        out_specs=pl.BlockSpec((tm, tn), lambda i, j: (i, j)),
        out_shape=jax.ShapeDtypeStruct((t, n), BF16),
        scratch_shapes=[pltpu.VMEM((tm, d), BF16)],
        compiler_params=_params(("parallel", "arbitrary")),
        name="modproj",
    )(x, g.reshape(1, d), sh, sc, w_all)


def _qkprep_kernel(p_ref, cos_ref, sin_ref, qg_ref, kg_ref, q_ref, k_ref, *, n_q, n_kv):
    cos = cos_ref[...]
    sin = sin_ref[...]
    lane = lax.broadcasted_iota(jnp.int32, (1, HEAD_DIM), 1)
    first_half = (lane % (HEAD_DIM // 2)) < (HEAD_DIM // 4)

    def head(h, gain, scale):
        xh = p_ref[:, h * HEAD_DIM:(h + 1) * HEAD_DIM].astype(F32)
        ms = jnp.mean(xh * xh, axis=-1, keepdims=True)
        y = xh * lax.rsqrt(ms + EPS) * gain
        partner = jnp.where(first_half,
                            pltpu.roll(y, HEAD_DIM - HEAD_DIM // 4, 1),
                            pltpu.roll(y, HEAD_DIM // 4, 1))
        return (y * cos + partner * sin) * scale

    for h in range(n_q):
        q_ref[h] = head(h, qg_ref[...], ATTN_SCALE * LOG2_E).astype(q_ref.dtype)
    for h in range(n_kv):
        k_ref[h] = head(n_q + h, kg_ref[...], 1.0).astype(k_ref.dtype)


def _qkprep(px, cos, sin, qg, kg, n_q, n_kv):
    t = px.shape[0]
    tr = _tile(t, 264, 8)
    width = (n_q + n_kv) * HEAD_DIM
    return pl.pallas_call(
        functools.partial(_qkprep_kernel, n_q=n_q, n_kv=n_kv),
        grid=(t // tr,),
        in_specs=[
            pl.BlockSpec((tr, width), lambda i: (i, 0)),
            pl.BlockSpec((tr, HEAD_DIM), lambda i: (i, 0)),
            pl.BlockSpec((tr, HEAD_DIM), lambda i: (i, 0)),
            pl.BlockSpec((1, HEAD_DIM), lambda i: (0, 0)),
            pl.BlockSpec((1, HEAD_DIM), lambda i: (0, 0)),
        ],
        out_specs=[
            pl.BlockSpec((n_q, tr, HEAD_DIM), lambda i: (0, i, 0)),
            pl.BlockSpec((n_kv, tr, HEAD_DIM), lambda i: (0, i, 0)),
        ],
        out_shape=[
            jax.ShapeDtypeStruct((n_q, t, HEAD_DIM), BF16),
            jax.ShapeDtypeStruct((n_kv, t, HEAD_DIM), BF16),
        ],
        compiler_params=_params(("parallel",)),
        name="qkprep",
    )(px, cos, sin, qg.reshape(1, HEAD_DIM), kg.reshape(1, HEAD_DIM))


def _rope_tables(n_lat, n_ctx):
    n = HEAD_DIM // 4
    inv = ROPE_THETA ** (-np.arange(n, dtype=np.float32) / n)
    t = np.arange(n_lat)
    rows = (t // GRID_W).astype(np.float32)
    cols = (t % GRID_W).astype(np.float32)
    ang = np.concatenate([rows[:, None] * inv[None, :]] * 2 + [cols[:, None] * inv[None, :]] * 2, axis=1)
    ang = jnp.asarray(ang.astype(np.float32))
    sign = np.tile(np.concatenate([-np.ones(n, np.float32), np.ones(n, np.float32)]), 2)
    cos = jnp.concatenate([jnp.cos(ang), jnp.ones((n_ctx, HEAD_DIM), F32)], axis=0)
    sin = jnp.concatenate([jnp.sin(ang) * sign[None, :], jnp.zeros((n_ctx, HEAD_DIM), F32)], axis=0)
    return cos, sin


def _attn_kernel(*refs, tq, tk, n_kv, aliased):
    if aliased:
        q_ref, k_ref, v_ref, _, o_ref, s_ref, m_ref, l_ref, acc_ref = refs
    else:
        q_ref, k_ref, v_ref, o_ref, s_ref, m_ref, l_ref, acc_ref = refs
    n_chunks = n_kv // tk
    m_ref[...] = jnp.full(m_ref.shape, -jnp.inf, F32)
    l_ref[...] = jnp.zeros(l_ref.shape, F32)
    acc_ref[...] = jnp.zeros(acc_ref.shape, F32)

    def scores(r, ks):
        return lax.dot_general(q_ref[r], k_ref[0, ks, :], (((1,), (1,)), ((), ())), preferred_element_type=F32)

    s_ref[0] = scores(0, pl.ds(0, tk))

    def body(c, carry):
        ks = pl.ds(pl.multiple_of(c * tk, tk), tk)
        kn = pl.ds(pl.multiple_of(jnp.minimum(c + 1, n_chunks - 1) * tk, tk), tk)
        v = v_ref[ks, :]
        for r in range(KV_REP):
            cur, nxt = r % 2, (r + 1) % 2
            s_ref[nxt] = scores(r + 1, ks) if r + 1 < KV_REP else scores(0, kn)
            s = s_ref[cur]
            m_prev = m_ref[r]
            m_new = jnp.maximum(m_prev, jnp.max(s, axis=-1, keepdims=True))
            alpha = jnp.exp2(m_prev - m_new)
            p = jnp.exp2(s - m_new)
            l_ref[r] = alpha * l_ref[r] + jnp.sum(p, axis=-1, keepdims=True)
            acc_ref[r] = alpha * acc_ref[r] + jnp.dot(p.astype(BF16), v, preferred_element_type=F32)
            m_ref[r] = m_new
        return carry

    lax.fori_loop(0, n_chunks, body, 0)
    for r in range(KV_REP):
        o_ref[:, r * HEAD_DIM:(r + 1) * HEAD_DIM] = (acc_ref[r] / l_ref[r]).astype(o_ref.dtype)


def _attention(q, k, px, v_col_block, q_row0, n_q_rows, kv_row0, n_kv, tq, tk, out=None):
    n_heads, t, _ = q.shape
    n_kv_heads = k.shape[0]
    assert q_row0 % tq == 0 and n_q_rows % tq == 0 and kv_row0 % n_kv == 0 and n_kv % tk == 0
    qb0 = q_row0 // tq
    kb0 = kv_row0 // n_kv
    aliased = out is not None
    in_specs = [
        pl.BlockSpec((KV_REP, tq, HEAD_DIM), lambda g, i: (g, qb0 + i, 0)),
        pl.BlockSpec((1, n_kv, HEAD_DIM), lambda g, i: (g, kb0, 0)),
        pl.BlockSpec((n_kv, HEAD_DIM), lambda g, i: (kb0, v_col_block + g)),
    ]
    args = [q, k, px]
    if aliased:
        in_specs.append(pl.BlockSpec(memory_space=pl.ANY))
        args.append(out)
    return pl.pallas_call(
        functools.partial(_attn_kernel, tq=tq, tk=tk, n_kv=n_kv, aliased=aliased),
        grid=(n_kv_heads, n_q_rows // tq),
        in_specs=in_specs,
        out_specs=pl.BlockSpec((tq, KV_REP * HEAD_DIM), lambda g, i: (qb0 + i, g)),
        out_shape=jax.ShapeDtypeStruct((t, n_heads * HEAD_DIM), BF16),
        scratch_shapes=[
            pltpu.VMEM((2, tq, tk), F32),
            pltpu.VMEM((KV_REP, tq, 1), F32),
            pltpu.VMEM((KV_REP, tq, 1), F32),
            pltpu.VMEM((KV_REP, tq, HEAD_DIM), F32),
        ],
        input_output_aliases={3: 0} if aliased else {},
        compiler_params=_params(("parallel", "arbitrary")),
        name="attention",
    )(*args)


def _gmlp_kernel(p_ref, lng_ref, lnb_ref, ws_ref, bst_ref, o_ref, *, n_chunks, d_g):
    n_groups = d_g // GMLP_GROUP_W
    for b in range(n_chunks):
        rows = slice(b * GMLP_CHUNK, (b + 1) * GMLP_CHUNK)
        uv = jax.nn.gelu(p_ref[rows, :].astype(F32))
        u = uv[:, :d_g]
        v = uv[:, d_g:]
        vc = v - jnp.mean(v, axis=-1, keepdims=True)
        var = jnp.mean(vc * vc, axis=-1, keepdims=True)
        vn = (vc * lax.rsqrt(var + LN_EPS) * lng_ref[...] + lnb_ref[...]).astype(BF16)
        for g in range(n_groups):
            cols = slice(g * GMLP_GROUP_W, (g + 1) * GMLP_GROUP_W)
            vm = jnp.dot(ws_ref[g].astype(BF16), vn[:, cols], preferred_element_type=F32) + bst_ref[:, g:g + 1]
            o_ref[rows, cols] = (u[:, cols] * vm).astype(o_ref.dtype)


def _gmlp(px, col_block, d_g, ln_g, ln_b, ws, bs):
    t = px.shape[0]
    n_chunks = 2
    tr = n_chunks * GMLP_CHUNK
    assert t % tr == 0
    n_groups = d_g // GMLP_GROUP_W
    return pl.pallas_call(
        functools.partial(_gmlp_kernel, n_chunks=n_chunks, d_g=d_g),
        grid=(t // tr,),
        in_specs=[
            pl.BlockSpec((tr, 2 * d_g), lambda i: (i, col_block)),
            pl.BlockSpec((1, d_g), lambda i: (0, 0)),
            pl.BlockSpec((1, d_g), lambda i: (0, 0)),
            pl.BlockSpec((n_groups, GMLP_CHUNK, GMLP_CHUNK), lambda i: (0, 0, 0)),
            pl.BlockSpec((GMLP_CHUNK, n_groups), lambda i: (0, 0)),
        ],
        out_specs=pl.BlockSpec((tr, d_g), lambda i: (i, 0)),
        out_shape=jax.ShapeDtypeStruct((t, d_g), BF16),
        compiler_params=_params(("parallel",)),
        name="gmlp",
    )(px, ln_g.reshape(1, d_g), ln_b.reshape(1, d_g), ws, bs.T)


def _merge_kernel(hy_ref, at_ref, gm_ref, gc_ref, wbr_ref, gup_ref, gb_ref, o_ref, *, d_hy, d_at):
    gc = gc_ref[...]
    bounds = (0, d_hy, d_hy + d_at, wbr_ref.shape[0])
    merged = None
    for b, a_ref in enumerate((hy_ref, at_ref, gm_ref)):
        y = jnp.dot(a_ref[...], wbr_ref[bounds[b]:bounds[b + 1], :], preferred_element_type=F32)
        gate = jax.nn.sigmoid(jnp.dot(gc, gup_ref[b], preferred_element_type=F32) + gb_ref[b:b + 1, :])
        merged = gate * y if merged is None else merged + gate * y
    o_ref[...] = merged.astype(o_ref.dtype)


def _merge(o_hy, o_at, o_gm, px, gc_col_block, gate_rank, w_br, gate_up, gate_b, layer):
    t = px.shape[0]
    d_hy, d_at, d_gm = o_hy.shape[1], o_at.shape[1], o_gm.shape[1]
    d = w_br.shape[2]
    tm = _tile(t, 528, 16)
    tn = _tile(d, 1024, LANES)
    return pl.pallas_call(
        functools.partial(_merge_kernel, d_hy=d_hy, d_at=d_at),
        grid=(t // tm, d // tn),
        in_specs=[
            pl.BlockSpec((tm, d_hy), lambda i, j: (i, 0)),
            pl.BlockSpec((tm, d_at), lambda i, j: (i, 0)),
            pl.BlockSpec((tm, d_gm), lambda i, j: (i, 0)),
            pl.BlockSpec((tm, gate_rank), lambda i, j: (i, gc_col_block)),
            pl.BlockSpec((None, d_hy + d_at + d_gm, tn), lambda i, j: (layer, 0, j)),
            pl.BlockSpec((None, N_BRANCH, gate_rank, tn), lambda i, j: (layer, 0, 0, j)),
            pl.BlockSpec((None, N_BRANCH, tn), lambda i, j: (layer, 0, j)),
        ],
        out_specs=pl.BlockSpec((tm, tn), lambda i, j: (i, j)),
        out_shape=jax.ShapeDtypeStruct((t, d), BF16),
        compiler_params=_params(("parallel", "arbitrary")),
        name="merge",
    )(o_hy, o_at, o_gm, px, w_br, gate_up, gate_b)


def _outproj_kernel(a_ref, w_ref, x_ref, gate_ref, o_ref, *, n_lat, tm):
    y = jnp.dot(a_ref[...], w_ref[...], preferred_element_type=F32)
    rid = pl.program_id(0) * tm + lax.broadcasted_iota(jnp.int32, (tm, 1), 0)
    gate = jnp.where(rid >= n_lat, gate_ref[1:2, :], gate_ref[0:1, :])
    o_ref[...] = x_ref[...] + gate * y


def _outproj(a, w_all, layer, x, gate, n_lat):
    t, k = a.shape
    d = w_all.shape[2]
    tm = _tile(t, 528, 16)
    tn = _tile(d, 1024, LANES)
    return pl.pallas_call(
        functools.partial(_outproj_kernel, n_lat=n_lat, tm=tm),
        grid=(t // tm, d // tn),
        in_specs=[
            pl.BlockSpec((tm, k), lambda i, j: (i, 0)),
            pl.BlockSpec((None, k, tn), lambda i, j: (layer, 0, j)),
            pl.BlockSpec((tm, tn), lambda i, j: (i, j)),
            pl.BlockSpec((2, tn), lambda i, j: (0, j)),
        ],
        out_specs=pl.BlockSpec((tm, tn), lambda i, j: (i, j)),
        out_shape=jax.ShapeDtypeStruct((t, d), F32),
        compiler_params=_params(("parallel", "arbitrary")),
        name="outproj",
    )(a, w_all, x, gate)


def _split_bf16(w, parts):
    out = []
    rest = w.astype(F32)
    for _ in range(parts):
        piece = rest.astype(BF16)
        out.append(piece)
        rest = rest - piece.astype(F32)
    return jnp.stack(out)


def _router_kernel(x_ref, g_ref, sh_ref, sc_ref, rw_ref, rb_ref, h_ref, lg_ref, *, n_lat, tm):
    _modulate_rows(x_ref, g_ref, sh_ref, sc_ref, h_ref, pl.program_id(0) * tm, n_lat, tm)
    h = h_ref[...]
    logits = rb_ref[...]
    for part in range(rw_ref.shape[0]):
        logits = logits + jnp.dot(h, rw_ref[part], preferred_element_type=F32)
    lg_ref[...] = logits


def _router(x, g, sh, sc, rw, rb, n_lat):
    t, d = x.shape
    tm = _tile(t, 264, 16)
    return pl.pallas_call(
        functools.partial(_router_kernel, n_lat=n_lat, tm=tm),
        grid=(t // tm,),
        in_specs=[
            pl.BlockSpec((tm, d), lambda i: (i, 0)),
            pl.BlockSpec((1, d), lambda i: (0, 0)),
            pl.BlockSpec((2, d), lambda i: (0, 0)),
            pl.BlockSpec((2, d), lambda i: (0, 0)),
            pl.BlockSpec((rw.shape[0], d, ROUTER_COLS), lambda i: (0, 0, 0)),
            pl.BlockSpec((1, ROUTER_COLS), lambda i: (0, 0)),
        ],
        out_specs=[
            pl.BlockSpec((tm, d), lambda i: (i, 0)),
            pl.BlockSpec((tm, ROUTER_COLS), lambda i: (i, 0)),
        ],
        out_shape=[
            jax.ShapeDtypeStruct((t, d), BF16),
            jax.ShapeDtypeStruct((t, ROUTER_COLS), F32),
        ],
        compiler_params=_params(("parallel",)),
        name="router",
    )(x, g.reshape(1, d), sh, sc, rw, rb)


def _experts_kernel(te_ref, tb_ref, tf_ref, nu_ref, x_ref, rw_ref, wg_ref, wu_ref, wd_ref, o_ref,
                    wg_s, wu_s, wd_s):
    i = pl.program_id(0)

    @pl.when(i < nu_ref[0])
    def _():
        @pl.when(tf_ref[i] == 1)
        def _():
            wg_s[...] = wg_ref[...].astype(BF16)
            wu_s[...] = wu_ref[...].astype(BF16)
            wd_s[...] = wd_ref[...].astype(BF16)

        xg = x_ref[...]
        hg = jnp.dot(xg, wg_s[...], preferred_element_type=F32)
        hu = jnp.dot(xg, wu_s[...], preferred_element_type=F32)
        hid = (jax.nn.silu(hg) * hu * rw_ref[...]).astype(BF16)
        o_ref[...] = jnp.dot(hid, wd_s[...], preferred_element_type=F32).astype(o_ref.dtype)


def _experts(xg, roww, tile_expert, tile_block, tile_first, n_used, w_gate, w_up, w_down, layer, tme):
    p, d = xg.shape
    f = w_gate.shape[3]
    n_tiles = p // tme
    grid_spec = pltpu.PrefetchScalarGridSpec(
        num_scalar_prefetch=4,
        grid=(n_tiles,),
        in_specs=[
            pl.BlockSpec((tme, d), lambda i, te, tb, tf, nu: (tb[i], 0)),
            pl.BlockSpec((tme, 1), lambda i, te, tb, tf, nu: (tb[i], 0)),
            pl.BlockSpec((None, None, d, f), lambda i, te, tb, tf, nu: (layer, te[i], 0, 0)),
            pl.BlockSpec((None, None, d, f), lambda i, te, tb, tf, nu: (layer, te[i], 0, 0)),
            pl.BlockSpec((None, None, f, d), lambda i, te, tb, tf, nu: (layer, te[i], 0, 0)),
        ],
        out_specs=pl.BlockSpec((tme, d), lambda i, te, tb, tf, nu: (tb[i], 0)),
        scratch_shapes=[
            pltpu.VMEM((d, f), BF16),
            pltpu.VMEM((d, f), BF16),
            pltpu.VMEM((f, d), BF16),
        ],
    )
    return pl.pallas_call(
        _experts_kernel,
        grid_spec=grid_spec,
        out_shape=jax.ShapeDtypeStruct((p, d), BF16),
        compiler_params=_params(("arbitrary",)),
        name="experts",
    )(tile_expert, tile_block, tile_first, n_used, xg, roww, w_gate, w_up, w_down)


def _combine_kernel(x_ref, y0_ref, y1_ref, gate_ref, o_ref, *, n_lat, tm):
    rid = pl.program_id(0) * tm + lax.broadcasted_iota(jnp.int32, (tm, 1), 0)
    gate = jnp.where(rid >= n_lat, gate_ref[1:2, :], gate_ref[0:1, :])
    o_ref[...] = x_ref[...] + gate * (y0_ref[...].astype(F32) + y1_ref[...].astype(F32))


def _combine(x, y0, y1, gate, n_lat):
    t, d = x.shape
    tm = _tile(t, 264, 16)
    row = lambda: pl.BlockSpec((tm, d), lambda i: (i, 0))
    return pl.pallas_call(
        functools.partial(_combine_kernel, n_lat=n_lat, tm=tm),
        grid=(t // tm,),
        in_specs=[row(), row(), row(), pl.BlockSpec((2, d), lambda i: (0, 0))],
        out_specs=row(),
        out_shape=jax.ShapeDtypeStruct((t, d), F32),
        compiler_params=_params(("parallel",)),
        name="moe_combine",
    )(x, y0, y1, gate)


def _route(logits, n_groups, per_group):
    lg = logits[:, :n_groups]
    le_all = logits[:, n_groups:n_groups + n_groups * per_group].reshape(-1, n_groups, per_group)
    grp = jnp.argmax(lg, axis=-1)
    p_grp = jnp.take_along_axis(jax.nn.softmax(lg, axis=-1), grp[:, None], axis=-1)
    le = jnp.take_along_axis(le_all, grp[:, None, None], axis=1)[:, 0]
    top_p, top_i = lax.top_k(jax.nn.softmax(le, axis=-1), TOP_K)
    w = top_p / jnp.sum(top_p, axis=-1, keepdims=True) * p_grp
    eid = grp[:, None] * per_group + top_i
    return eid.astype(jnp.int32), w


def _dispatch(eid, w, n_experts, tme):
    t = eid.shape[0]
    n_pairs = t * TOP_K
    p_rows = n_pairs + n_experts * tme
    p_rows = -(-p_rows // tme) * tme
    flat_e = eid.reshape(-1)
    onehot = (flat_e[:, None] == jnp.arange(n_experts, dtype=jnp.int32)[None, :]).astype(jnp.int32)
    csum = jnp.cumsum(onehot, axis=0)
    rank = jnp.take_along_axis(csum, flat_e[:, None], axis=1)[:, 0] - 1
    counts = csum[-1]
    padded = -(-counts // tme) * tme
    ends = jnp.cumsum(padded)
    starts = ends - padded
    dest = starts[flat_e] + rank
    src_tok = (jnp.arange(p_rows, dtype=jnp.int32) % t).at[dest].set(jnp.arange(n_pairs, dtype=jnp.int32) // TOP_K)
    roww = jnp.zeros((p_rows,), F32).at[dest].set(w.reshape(-1))
    n_used = (ends[-1] // tme).astype(jnp.int32)
    tile_ids = jnp.arange(p_rows // tme, dtype=jnp.int32)
    tile_block = jnp.minimum(tile_ids, n_used - 1)
    tile_expert = jnp.minimum(
        jnp.searchsorted(ends, tile_block * tme, side="right").astype(jnp.int32), n_experts - 1)
    tile_first = (tile_block * tme == starts[tile_expert]).astype(jnp.int32)
    return src_tok, roww.reshape(-1, 1), dest.reshape(t, TOP_K), tile_expert, tile_block, tile_first, n_used.reshape(1)


def _rmsnorm_kernel(x_ref, g_ref, o_ref):
    xf = x_ref[...]
    ms = jnp.mean(xf * xf, axis=-1, keepdims=True)
    o_ref[...] = xf * lax.rsqrt(ms + EPS) * g_ref[...]


def _final_norm(x, g, n_rows):
    d = x.shape[1]
    tr = _tile(n_rows, 256, 8)
    return pl.pallas_call(
        _rmsnorm_kernel,
        grid=(n_rows // tr,),
        in_specs=[pl.BlockSpec((tr, d), lambda i: (i, 0)), pl.BlockSpec((1, d), lambda i: (0, 0))],
        out_specs=pl.BlockSpec((tr, d), lambda i: (i, 0)),
        out_shape=jax.ShapeDtypeStruct((n_rows, d), F32),
        compiler_params=_params(("parallel",)),
        name="final_norm",
    )(x, g.reshape(1, d))


SUBLANES = 8


def _filter_features(length, emb):
    bands = (emb - 1) // 2
    i = np.arange(length, dtype=np.float64)
    t = i / (length - 1)
    f = np.linspace(1e-4, bands - 1, bands)
    w = 2.0 * np.pi * i / length
    z = np.concatenate([t[:, None], np.cos(-f[None, :] * w[:, None]), np.sin(-f[None, :] * w[:, None])], axis=1)
    z = np.concatenate([z, z[::-1]], axis=0)
    emb_pad = -(-emb // SUBLANES) * SUBLANES
    z = np.pad(z, ((0, 0), (0, emb_pad - emb)))
    return jnp.asarray(z.astype(np.float32))


def _filter_kernel(z_ref, w1_ref, b1_ref, w2_ref, b2_ref, w3_ref, b3_ref, w4_ref, fr_ref, dl_ref,
                   k_ref, nrm_ref, *, d_hy):
    dot = functools.partial(jnp.dot, preferred_element_type=F32, precision=HIGHEST)

    @pl.when(jnp.logical_and(pl.program_id(0) == 0, pl.program_id(1) == 0))
    def _():
        nrm_ref[...] = jnp.zeros(nrm_ref.shape, F32)

    z = z_ref[...]
    fr = fr_ref[...]
    h = jnp.sin(fr * (dot(z, w1_ref[...]) + b1_ref[...]))
    h = jnp.sin(fr * (dot(h, w2_ref[...]) + b2_ref[...]))
    h = jnp.sin(fr * (dot(h, w3_ref[...]) + b3_ref[...]))
    h_hi = h.astype(BF16)
    h_lo = (h - h_hi.astype(F32)).astype(BF16)
    hk = (jnp.dot(h_hi, w4_ref[0], preferred_element_type=F32)
          + jnp.dot(h_lo, w4_ref[0], preferred_element_type=F32)
          + jnp.dot(h_hi, w4_ref[1], preferred_element_type=F32))
    window = jnp.exp(-z[:, 0:1] * dl_ref[...]) + HYENA_SHIFT
    for o in range(HYENA_ORDER):
        k = hk[:, o * d_hy:(o + 1) * d_hy] * window
        k_ref[o] = k.astype(k_ref.dtype)
        nrm_ref[o:o + 1, :] += jnp.sum(jnp.abs(k), axis=0, keepdims=True)


def _hyena_filters(length, w1, b1, w2, b2, w3, b3, w4, freq, d_hy, out_dtype):
    emb, hidden = w1.shape
    z = _filter_features(length, emb)
    emb_pad = z.shape[1]
    w1p = jnp.pad(w1, ((0, emb_pad - emb), (0, 0)))
    max_decay = math.log(HYENA_TARGET) / HYENA_FAST_PCT
    min_decay = math.log(HYENA_TARGET) / HYENA_SLOW_PCT
    deltas = jnp.asarray(np.abs(np.linspace(min_decay, max_decay, d_hy)).astype(np.float32)).reshape(1, d_hy)
    tr = _tile(length, 512, SUBLANES)
    per = length // tr
    vec = lambda a: a.reshape(1, hidden)
    full = lambda shape: pl.BlockSpec(shape, lambda d, i: (0,) * len(shape))
    return pl.pallas_call(
        functools.partial(_filter_kernel, d_hy=d_hy),
        grid=(2, per),
        in_specs=[
            pl.BlockSpec((tr, emb_pad), lambda d, i: (d * per + i, 0)),
            full((emb_pad, hidden)), full((1, hidden)),
            full((hidden, hidden)), full((1, hidden)),
            full((hidden, hidden)), full((1, hidden)),
            pl.BlockSpec((2, hidden, HYENA_ORDER * d_hy), lambda d, i: (0, 0, d)),
            full((1, hidden)), full((1, d_hy)),
        ],
        out_specs=[
            pl.BlockSpec((HYENA_ORDER, tr, d_hy), lambda d, i: (0, d * per + i, 0)),
            pl.BlockSpec((HYENA_ORDER, d_hy), lambda d, i: (0, 0)),
        ],
        out_shape=[
            jax.ShapeDtypeStruct((HYENA_ORDER, 2 * length, d_hy), out_dtype),
            jax.ShapeDtypeStruct((HYENA_ORDER, d_hy), F32),
        ],
        compiler_params=_params(("arbitrary", "arbitrary")),
        name="hyena_filter",
    )(z, w1p, vec(b1), w2, vec(b2), w3, vec(b3), _split_bf16(w4, 2), vec(freq), deltas)


CONV_ROWS = 64
CONV_HALO = 8


def _shortconv_kernel(p_ref, w_ref, b_ref, lat_ref, ctx_ref, pad_ref, *, n_lat, n_ctx):
    t = n_lat + n_ctx
    tc = p_ref.shape[1]
    pad_ref[0:CONV_HALO, :] = jnp.zeros((CONV_HALO, tc), F32)
    pad_ref[t + CONV_HALO:t + 2 * CONV_HALO, :] = jnp.zeros((CONV_HALO, tc), F32)

    def fill(r, carry):
        src = pl.ds(pl.multiple_of(r * CONV_ROWS, CONV_ROWS), CONV_ROWS)
        dst = pl.ds(pl.multiple_of(r * CONV_ROWS + CONV_HALO, CONV_HALO), CONV_ROWS)
        pad_ref[dst, :] = p_ref[src, :].astype(F32)
        return carry

    lax.fori_loop(0, t // CONV_ROWS, fill, 0)
    w0, w1, w2, bias = w_ref[0:1, :], w_ref[1:2, :], w_ref[2:3, :], b_ref[...]
    ext_rows = CONV_ROWS + 2 * CONV_HALO
    inner = slice(CONV_HALO, CONV_HALO + CONV_ROWS)

    def segment(out_ref, row0, n_rows):
        def body(r, carry):
            ext = pad_ref[pl.ds(pl.multiple_of(row0 + r * CONV_ROWS, CONV_HALO), ext_rows), :]
            rid = r * CONV_ROWS + lax.broadcasted_iota(jnp.int32, (CONV_ROWS, 1), 0)
            prev = jnp.where(rid == 0, 0.0, pltpu.roll(ext, 1, 0)[inner])
            nxt = jnp.where(rid == n_rows - 1, 0.0, pltpu.roll(ext, ext_rows - 1, 0)[inner])
            y = prev * w0 + ext[inner] * w1 + nxt * w2 + bias
            out_ref[0, pl.ds(pl.multiple_of(r * CONV_ROWS, CONV_ROWS), CONV_ROWS), :] = y.astype(out_ref.dtype)
            return carry

        lax.fori_loop(0, n_rows // CONV_ROWS, body, 0)

    segment(lat_ref, 0, n_lat)
    segment(ctx_ref, n_lat, n_ctx)


def _shortconv(px, col0, d_hy, conv_w, conv_b, n_lat, n_ctx):
    t = px.shape[0]
    tc = _tile(d_hy, 256, LANES)
    assert col0 % tc == 0 and n_lat % CONV_ROWS == 0 and n_ctx % CONV_ROWS == 0
    per = d_hy // tc
    n_grp = HYENA_ORDER + 1
    return pl.pallas_call(
        functools.partial(_shortconv_kernel, n_lat=n_lat, n_ctx=n_ctx),
        grid=(n_grp, per),
        in_specs=[
            pl.BlockSpec((t, tc), lambda g, j: (0, col0 // tc + g * per + j)),
            pl.BlockSpec((3, tc), lambda g, j: (0, g * per + j)),
            pl.BlockSpec((1, tc), lambda g, j: (0, g * per + j)),
        ],
        out_specs=[
            pl.BlockSpec((1, n_lat, tc), lambda g, j: (g, 0, j)),
            pl.BlockSpec((1, n_ctx, tc), lambda g, j: (g, 0, j)),
        ],
        out_shape=[
            jax.ShapeDtypeStruct((n_grp, n_lat, d_hy), BF16),
            jax.ShapeDtypeStruct((n_grp, n_ctx, d_hy), BF16),
        ],
        scratch_shapes=[pltpu.VMEM((t + 2 * CONV_HALO, tc), F32)],
        compiler_params=_params(("parallel", "parallel")),
        name="shortconv",
    )(px, conv_w, conv_b.reshape(1, -1))


DFT_N2 = 128
K1_ALIGN = 16
T2_BLOCK = 16
STAGE2_UNROLL = 4


def _dft_constants(length):
    n = 2 * length
    n1 = n // DFT_N2
    assert n1 * DFT_N2 == n and n1 % 2 == 0
    n_t1 = n1 // 2
    k1_used = n1 // 2 + 1
    k1p = -(-k1_used // K1_ALIGN) * K1_ALIGN
    k1 = np.arange(k1p, dtype=np.float64)
    live = (k1 < k1_used).astype(np.float64)
    t1 = np.arange(n1, dtype=np.float64)
    ang1 = -2.0 * np.pi * k1[:, None] * t1[None, :] / n1
    f1 = np.stack([np.cos(ang1), np.sin(ang1)]) * live[None, :, None]
    k2 = np.arange(DFT_N2, dtype=np.float64)
    t2 = np.arange(DFT_N2, dtype=np.float64)
    ang2 = -2.0 * np.pi * t2[None, None, :] * (k1[:, None, None] + n1 * k2[None, :, None]) / n
    gr = np.cos(ang2) * live[:, None, None]
    gi = np.sin(ang2) * live[:, None, None]
    g = np.stack([gr, gi, gr.transpose(0, 2, 1), gi.transpose(0, 2, 1)])
    weight = np.where((k1 == 0) | (k1 == n1 // 2), 1.0, 2.0) * live
    angc = 2.0 * np.pi * np.arange(n_t1, dtype=np.float64)[:, None] * k1[None, :] / n1
    cinv = np.stack([np.cos(angc), np.sin(angc)]) * weight[None, None, :] / n
    as_bf16 = lambda a: jnp.asarray(a.astype(np.float32)).astype(BF16)
    eye = np.eye(T2_BLOCK)
    kron = lambda m: np.stack([np.kron(m[0], eye), np.kron(m[1], eye)])
    return dict(n1=n1, n_t1=n_t1, k1p=k1p, f1=as_bf16(kron(f1)), f1_sig=as_bf16(kron(f1[:, :, :n_t1])),
                g=as_bf16(g), cinv=as_bf16(kron(cinv)))


def _dft_stage1_kernel(f_ref, u_ref, a_ref):
    k, r, tc = u_ref.shape
    u = u_ref[...].reshape(k * r, tc)
    for part in range(2):
        a = jnp.dot(f_ref[part], u, preferred_element_type=F32).astype(a_ref.dtype)
        a_ref[part] = a.reshape(a_ref.shape[1:])


def _dft_stage1(f, u4, group):
    _, k, n2, c = u4.shape
    k1p = f.shape[1] // T2_BLOCK
    tc = _tile(c, 256, LANES)
    return pl.pallas_call(
        _dft_stage1_kernel,
        grid=(n2 // T2_BLOCK, c // tc),
        in_specs=[
            pl.BlockSpec((2, k1p * T2_BLOCK, k * T2_BLOCK), lambda i, j: (0, 0, 0)),
            pl.BlockSpec((None, k, T2_BLOCK, tc), lambda i, j: (group, 0, i, j)),
        ],
        out_specs=pl.BlockSpec((2, k1p, T2_BLOCK, tc), lambda i, j: (0, 0, i, j)),
        out_shape=jax.ShapeDtypeStruct((2, k1p, n2, c), BF16),
        compiler_params=_params(("parallel", "parallel")),
        name="dft_stage1",
    )(f, u4)


def _dft_stage2_filter_kernel(a_ref, g_ref, scale_ref, kf_ref, *, k1c):
    scale = scale_ref[...]

    def body(i, carry):
        ar, ai = a_ref[0, i], a_ref[1, i]
        gr, gi = g_ref[0, i], g_ref[1, i]
        dot = functools.partial(jnp.dot, preferred_element_type=F32)
        kf_ref[0, i] = (dot(gr, ar) - dot(gi, ai)) * scale
        kf_ref[1, i] = (dot(gr, ai) + dot(gi, ar)) * scale
        return carry

    lax.fori_loop(0, k1c, body, 0, unroll=STAGE2_UNROLL)


def _dft_stage2_conv_kernel(a_ref, g_ref, kf_ref, b_ref, *, k1c):
    def body(i, carry):
        ar, ai = a_ref[0, i], a_ref[1, i]
        gr, gi = g_ref[0, i], g_ref[1, i]
        dot = functools.partial(jnp.dot, preferred_element_type=F32)
        xr = dot(gr, ar) - dot(gi, ai)
        xi = dot(gr, ai) + dot(gi, ar)
        kr, ki = kf_ref[0, i], kf_ref[1, i]
        zr = (xr * kr - xi * ki).astype(BF16)
        zi = (xr * ki + xi * kr).astype(BF16)
        grt, git = g_ref[2, i], g_ref[3, i]
        b_ref[0, i] = (dot(grt, zr) + dot(git, zi)).astype(b_ref.dtype)
        b_ref[1, i] = (dot(grt, zi) - dot(git, zr)).astype(b_ref.dtype)
        return carry

    lax.fori_loop(0, k1c, body, 0, unroll=STAGE2_UNROLL)


def _dft_stage2(a4, g, kf=None, scale=None):
    _, k1p, n2, c = a4.shape
    k1c = K1_ALIGN
    tc = _tile(c, 256, LANES)
    a_spec = pl.BlockSpec((2, k1c, n2, tc), lambda i, j: (0, i, 0, j))
    g_spec = pl.BlockSpec((4, k1c, n2, n2), lambda i, j: (0, i, 0, 0))
    if kf is None:
        s_spec = pl.BlockSpec((1, tc), lambda i, j: (0, j))
        body, args, in_specs, dtype = _dft_stage2_filter_kernel, (a4, g, scale), [a_spec, g_spec, s_spec], F32
    else:
        body, args, in_specs, dtype = _dft_stage2_conv_kernel, (a4, g, kf), [a_spec, g_spec, a_spec], BF16
    return pl.pallas_call(
        functools.partial(body, k1c=k1c),
        grid=(k1p // k1c, c // tc),
        in_specs=in_specs,
        out_specs=a_spec,
        out_shape=jax.ShapeDtypeStruct(a4.shape, dtype),
        compiler_params=_params(("parallel", "arbitrary")),
        name="dft_stage2",
    )(*args)


def _dft_stage3_kernel(c_ref, b_ref, u_ref, m_ref, skip_ref, o_ref):
    _, k1p, r, tc = b_ref.shape
    rows = u_ref.shape[0] * r
    y = (jnp.dot(c_ref[0], b_ref[0].reshape(k1p * r, tc), preferred_element_type=F32)
         - jnp.dot(c_ref[1], b_ref[1].reshape(k1p * r, tc), preferred_element_type=F32))
    u = u_ref[...].reshape(rows, tc).astype(F32)
    out = m_ref[...].reshape(rows, tc).astype(F32) * (y + u * skip_ref[...])
    o_ref[0] = out.astype(o_ref.dtype).reshape(o_ref.shape[1:])


def _dft_stage3(cinv, b4, u4, u_group, m4, m_group, skip):
    _, k1p, n2, c = b4.shape
    n_t1 = cinv.shape[1] // T2_BLOCK
    tc = _tile(c, 256, LANES)
    sig_spec = lambda grp: pl.BlockSpec((None, n_t1, T2_BLOCK, tc), lambda i, j: (grp, 0, i, j))
    return pl.pallas_call(
        _dft_stage3_kernel,
        grid=(n2 // T2_BLOCK, c // tc),
        in_specs=[
            pl.BlockSpec((2, n_t1 * T2_BLOCK, k1p * T2_BLOCK), lambda i, j: (0, 0, 0)),
            pl.BlockSpec((2, k1p, T2_BLOCK, tc), lambda i, j: (0, 0, i, j)),
            sig_spec(u_group),
            sig_spec(m_group),
            pl.BlockSpec((1, tc), lambda i, j: (0, j)),
        ],
        out_specs=pl.BlockSpec((1, n_t1, T2_BLOCK, tc), lambda i, j: (0, 0, i, j)),
        out_shape=jax.ShapeDtypeStruct((1, n_t1, n2, c), BF16),
        compiler_params=_params(("parallel", "parallel")),
        name="dft_stage3",
    )(cinv, b4, u4, m4, skip.reshape(1, c))


def _hyena_latent(p_lat, kern, inv_norm, skip, consts):
    _, length, c = p_lat.shape
    n1, n_t1 = consts["n1"], consts["n_t1"]
    p4 = p_lat.reshape(HYENA_ORDER + 1, n_t1, DFT_N2, c)
    k4 = kern.reshape(HYENA_ORDER, n1, DFT_N2, c)
    u4, u_group = p4, 0
    for order in range(HYENA_ORDER):
        kf = _dft_stage2(_dft_stage1(consts["f1"], k4, order), consts["g"], scale=inv_norm[order:order + 1])
        b = _dft_stage2(_dft_stage1(consts["f1_sig"], u4, u_group), consts["g"], kf)
        u4 = _dft_stage3(consts["cinv"], b, u4, u_group, p4, order + 1, skip[order])
        u_group = 0
    return u4.reshape(length, c)


def _ctx_dft_constants(length):
    n = 2 * length
    k = np.arange(n, dtype=np.float64)
    ang = -2.0 * np.pi * k[:, None] * k[None, :] / n
    fwd = np.stack([np.cos(ang), np.sin(ang)])
    inv = np.stack([np.cos(ang[:length]), -np.sin(ang[:length])]) / n
    as_f32 = lambda a: jnp.asarray(a.astype(np.float32))
    return dict(fwd=as_f32(fwd), fwd_sig=as_f32(fwd[:, :, :length]), inv=as_f32(inv))


def _ctxconv_kernel(ff_ref, fs_ref, fi_ref, k_ref, scale_ref, u_ref, m_ref, skip_ref, o_ref):
    dot = functools.partial(jnp.dot, preferred_element_type=F32, precision=HIGHEST)
    k = k_ref[...] * scale_ref[...]
    u = u_ref[...].astype(F32)
    kr, ki = dot(ff_ref[0], k), dot(ff_ref[1], k)
    xr, xi = dot(fs_ref[0], u), dot(fs_ref[1], u)
    zr = xr * kr - xi * ki
    zi = xr * ki + xi * kr
    y = dot(fi_ref[0], zr) - dot(fi_ref[1], zi)
    o_ref[0] = (m_ref[...].astype(F32) * (y + u * skip_ref[...])).astype(o_ref.dtype)


def _ctxconv(consts, kern, order, inv_norm, u3, u_group, m3, m_group, skip):
    _, length, c = u3.shape
    n = 2 * length
    tc = _tile(c, 256, LANES)
    return pl.pallas_call(
        _ctxconv_kernel,
        grid=(c // tc,),
        in_specs=[
            pl.BlockSpec((2, n, n), lambda j: (0, 0, 0)),
            pl.BlockSpec((2, n, length), lambda j: (0, 0, 0)),
            pl.BlockSpec((2, length, n), lambda j: (0, 0, 0)),
            pl.BlockSpec((None, n, tc), lambda j: (order, 0, j)),
            pl.BlockSpec((1, tc), lambda j: (0, j)),
            pl.BlockSpec((None, length, tc), lambda j: (u_group, 0, j)),
            pl.BlockSpec((None, length, tc), lambda j: (m_group, 0, j)),
            pl.BlockSpec((1, tc), lambda j: (0, j)),
        ],
        out_specs=pl.BlockSpec((1, length, tc), lambda j: (0, 0, j)),
        out_shape=jax.ShapeDtypeStruct((1, length, c), BF16),
        compiler_params=_params(("parallel",)),
        name="ctxconv",
    )(consts["fwd"], consts["fwd_sig"], consts["inv"], kern, inv_norm[order:order + 1], u3, m3, skip.reshape(1, c))


def _hyena_context(p_ctx, kern, inv_norm, skip, consts):
    u3, u_group = p_ctx, 0
    for order in range(HYENA_ORDER):
        u3 = _ctxconv(consts, kern, order, inv_norm, u3, u_group, p_ctx, order + 1, skip[order])
        u_group = 0
    return u3[0]


def kernel(x, c, ctx, c_ctx, norm1_g, norm2_g, mod_a, mod_b, mod_bias, w_in, q_norm_g, k_norm_g, hy_conv_w, hy_conv_b, hy_w1, hy_b1, hy_w2, hy_b2, hy_w3, hy_b3, hy_w4, hy_freq, hy_skip, gm_ln_g, gm_ln_b, gm_ws, gm_bs, gate_up, gate_b, w_br, w_out, rt_group_w, rt_group_b, rt_expert_w, rt_expert_b, ex_w_gate, ex_w_up, ex_w_down, final_norm_g):
    bsz, seq, d = x.shape
    assert bsz == 1 and c.shape[0] == 1
    n_ctx = ctx.shape[1]
    depth = w_in.shape[0]
    t_all = seq + n_ctx
    d_hy = hy_skip.shape[2]
    d_gm = gm_ln_g.shape[1]
    d_at = w_br.shape[1] - d_hy - d_gm
    gate_rank = gate_up.shape[2]
    n_q = d_at // HEAD_DIM
    n_kv = n_q // KV_REP
    q_end = d_at
    v_start = q_end + n_kv * HEAD_DIM
    v_end = v_start + n_kv * HEAD_DIM
    hy_end = v_end + (HYENA_ORDER + 1) * d_hy
    gm_end = hy_end + 2 * d_gm
    n_groups = rt_group_w.shape[2]
    per_group = rt_expert_w.shape[3]
    n_experts = n_groups * per_group
    assert n_groups + n_experts <= ROUTER_COLS
    assert hy_end % (2 * d_gm) == 0 and gm_end % gate_rank == 0 and seq % ROW_CHUNK == 0
    tme = 256

    xs = jnp.concatenate([x[0], ctx[0]], axis=0)
    cond = jnp.stack([c[0], c_ctx], axis=0)
    cos, sin = _rope_tables(seq, n_ctx)

    tq = _tile(seq, 256, 16)
    tk = _tile(t_all, 2816, 256)
    tq_c = _tile(n_ctx, 256, 16)
    dft_lat = _dft_constants(seq)
    dft_ctx = _ctx_dft_constants(n_ctx)
    w_in_b, w_br_b, gate_up_b, w_out_b = (w.astype(BF16) for w in (w_in, w_br, gate_up, w_out))

    for l in range(depth):
        m = (jnp.dot(jnp.dot(jax.nn.silu(cond), mod_a[l], precision=HIGHEST), mod_b[l], precision=HIGHEST)
             + mod_bias[l]).reshape(2, N_MOD, d)

        px = _modproj(xs, norm1_g[l], m[:, 0], m[:, 1], w_in_b, l, seq)
        q, k = _qkprep(px, cos, sin, q_norm_g[l], k_norm_g[l], n_q, n_kv)
        o_at = _attention(q, k, px, v_start // HEAD_DIM, 0, seq, 0, t_all, tq, tk)
        o_at = _attention(q, k, px, v_start // HEAD_DIM, seq, n_ctx, seq, n_ctx, tq_c, n_ctx, out=o_at)

        hy_w = (hy_w1[l], hy_b1[l], hy_w2[l], hy_b2[l], hy_w3[l], hy_b3[l], hy_w4[l], hy_freq[l])
        p_lat, p_ctx = _shortconv(px, v_end, d_hy, hy_conv_w[l], hy_conv_b[l], seq, n_ctx)
        kern_x, norm_x = _hyena_filters(seq, *hy_w, d_hy, BF16)
        kern_c, norm_c = _hyena_filters(n_ctx, *hy_w, d_hy, F32)
        o_hy_x = _hyena_latent(p_lat, kern_x, 1.0 / norm_x, hy_skip[l], dft_lat)
        o_hy_c = _hyena_context(p_ctx, kern_c, 1.0 / norm_c, hy_skip[l], dft_ctx)
        o_hy = jnp.concatenate([o_hy_x, o_hy_c], axis=0)

        o_gm = _gmlp(px, hy_end // (2 * d_gm), d_gm, gm_ln_g[l], gm_ln_b[l], gm_ws[l], gm_bs[l])

        merged = _merge(o_hy, o_at, o_gm, px, gm_end // gate_rank, gate_rank, w_br_b, gate_up_b, gate_b, l)
        xs = _outproj(merged, w_out_b, l, xs, m[:, 2], seq)

        rw = jnp.concatenate([rt_group_w[l]] + [rt_expert_w[l, g] for g in range(n_groups)], axis=1)
        rw = jnp.pad(rw, ((0, 0), (0, ROUTER_COLS - rw.shape[1])))
        rb = jnp.concatenate([rt_group_b[l], rt_expert_b[l].reshape(-1)])
        rb = jnp.pad(rb, (0, ROUTER_COLS - rb.shape[0])).reshape(1, ROUTER_COLS)
        h2, logits = _router(xs, norm2_g[l], m[:, 3], m[:, 4], _split_bf16(rw, 3), rb, seq)
        eid, wts = _route(logits, n_groups, per_group)
        src_tok, roww, pos, tile_expert, tile_block, tile_first, n_used = _dispatch(eid, wts, n_experts, tme)
        take_rows = lambda a, idx: a.at[idx].get(mode="promise_in_bounds")
        xg = take_rows(h2, src_tok)
        yo = _experts(xg, roww, tile_expert, tile_block, tile_first, n_used,
                      ex_w_gate, ex_w_up, ex_w_down, l, tme)
        xs = _combine(xs, take_rows(yo, pos[:, 0]), take_rows(yo, pos[:, 1]), m[:, 5], seq)

    return _final_norm(xs, final_norm_g, seq).reshape(1, seq, d)
```

```python
import functools
import math

import numpy as np
import jax
import jax.numpy as jnp
from jax import lax
from jax.experimental import pallas as pl
from jax.experimental.pallas import tpu as pltpu

F32 = jnp.float32
BF16 = jnp.bfloat16
HIGHEST = lax.Precision.HIGHEST

HEAD_DIM = 128
KV_REP = 4
GRID_W = 64
ROPE_THETA = 10000.0
ATTN_SCALE = HEAD_DIM ** -0.5
LOG2_E = math.log2(math.e)
HYENA_ORDER = 2
HYENA_SHIFT = 0.05
HYENA_FAST_PCT = 0.3
HYENA_SLOW_PCT = 1.5
HYENA_TARGET = 1e-2
GMLP_CHUNK = 128
GMLP_GROUP_W = 128
N_BRANCH = 3
N_MOD = 6
TOP_K = 2
EPS = 1e-6
LN_EPS = 1e-5

LANES = 128
ROUTER_COLS = LANES
VMEM_LIMIT_MB = 56


def _params(semantics, vmem_mb=VMEM_LIMIT_MB):
    return pltpu.CompilerParams(dimension_semantics=semantics, vmem_limit_bytes=vmem_mb * 1024 * 1024)


def _tile(n, target, mult):
    best = None
    for d in range(mult, min(n, target) + 1, mult):
        if n % d == 0:
            best = d
    assert best is not None, (n, target, mult)
    return best


ROW_CHUNK = 16


def _modulate_rows(x_ref, g_ref, sh_ref, sc_ref, h_ref, row0, n_lat, n_rows):
    n_chunks = n_rows // ROW_CHUNK
    lat_chunks = jnp.clip((n_lat - row0) // ROW_CHUNK, 0, n_chunks)

    def segment(seg, lo, hi):
        gain = g_ref[...] * (1.0 + sc_ref[seg:seg + 1, :])
        shift = sh_ref[seg:seg + 1, :]

        def body(r, carry):
            rows = pl.ds(pl.multiple_of(r * ROW_CHUNK, ROW_CHUNK), ROW_CHUNK)
            xf = x_ref[rows, :]
            ms = jnp.mean(xf * xf, axis=-1, keepdims=True)
            h_ref[rows, :] = (xf * lax.rsqrt(ms + EPS) * gain + shift).astype(h_ref.dtype)
            return carry

        lax.fori_loop(lo, hi, body, 0)

    segment(0, 0, lat_chunks)
    segment(1, lat_chunks, n_chunks)


def _modproj_kernel(x_ref, g_ref, sh_ref, sc_ref, w_ref, o_ref, h_ref, *, n_lat, tm):
    i = pl.program_id(0)

    @pl.when(pl.program_id(1) == 0)
    def _():
        _modulate_rows(x_ref, g_ref, sh_ref, sc_ref, h_ref, i * tm, n_lat, tm)

    o_ref[...] = jnp.dot(h_ref[...], w_ref[...], preferred_element_type=F32).astype(o_ref.dtype)


def _modproj(x, g, sh, sc, w_all, layer, n_lat):
    t, d = x.shape
    n = w_all.shape[2]
    tm = _tile(t, 528, 16)
    tn = _tile(n, 512, LANES)
    return pl.pallas_call(
        functools.partial(_modproj_kernel, n_lat=n_lat, tm=tm),
        grid=(t // tm, n // tn),
        in_specs=[
            pl.BlockSpec((tm, d), lambda i, j: (i, 0)),
            pl.BlockSpec((1, d), lambda i, j: (0, 0)),
            pl.BlockSpec((2, d), lambda i, j: (0, 0)),
            pl.BlockSpec((2, d), lambda i, j: (0, 0)),
            pl.BlockSpec((None, d, tn), lambda i, j: (layer, 0, j)),
        ],
        out_specs=pl.BlockSpec((tm, tn), lambda i, j: (i, j)),
        out_shape=jax.ShapeDtypeStruct((t, n), BF16),
        scratch_shapes=[pltpu.VMEM((tm, d), BF16)],
        compiler_params=_params(("parallel", "arbitrary")),
        name="modproj",
    )(x, g.reshape(1, d), sh, sc, w_all)


def _qkprep_kernel(p_ref, cos_ref, sin_ref, qg_ref, kg_ref, q_ref, k_ref, *, n_q, n_kv):
    cos = cos_ref[...]
    sin = sin_ref[...]
    lane = lax.broadcasted_iota(jnp.int32, (1, HEAD_DIM), 1)
    first_half = (lane % (HEAD_DIM // 2)) < (HEAD_DIM // 4)

    def head(h, gain, scale):
        xh = p_ref[:, h * HEAD_DIM:(h + 1) * HEAD_DIM].astype(F32)
        ms = jnp.mean(xh * xh, axis=-1, keepdims=True)
        y = xh * lax.rsqrt(ms + EPS) * gain
        partner = jnp.where(first_half,
                            pltpu.roll(y, HEAD_DIM - HEAD_DIM // 4, 1),
                            pltpu.roll(y, HEAD_DIM // 4, 1))
        return (y * cos + partner * sin) * scale

    for h in range(n_q):
        q_ref[h] = head(h, qg_ref[...], ATTN_SCALE * LOG2_E).astype(q_ref.dtype)
    for h in range(n_kv):
        k_ref[h] = head(n_q + h, kg_ref[...], 1.0).astype(k_ref.dtype)


def _qkprep(px, cos, sin, qg, kg, n_q, n_kv):
    t = px.shape[0]
    tr = _tile(t, 264, 8)
    width = (n_q + n_kv) * HEAD_DIM
    return pl.pallas_call(
        functools.partial(_qkprep_kernel, n_q=n_q, n_kv=n_kv),
        grid=(t // tr,),
        in_specs=[
            pl.BlockSpec((tr, width), lambda i: (i, 0)),
            pl.BlockSpec((tr, HEAD_DIM), lambda i: (i, 0)),
            pl.BlockSpec((tr, HEAD_DIM), lambda i: (i, 0)),
            pl.BlockSpec((1, HEAD_DIM), lambda i: (0, 0)),
            pl.BlockSpec((1, HEAD_DIM), lambda i: (0, 0)),
        ],
        out_specs=[
            pl.BlockSpec((n_q, tr, HEAD_DIM), lambda i: (0, i, 0)),
            pl.BlockSpec((n_kv, tr, HEAD_DIM), lambda i: (0, i, 0)),
        ],
        out_shape=[
            jax.ShapeDtypeStruct((n_q, t, HEAD_DIM), BF16),
            jax.ShapeDtypeStruct((n_kv, t, HEAD_DIM), BF16),
        ],
        compiler_params=_params(("parallel",)),
        name="qkprep",
    )(px, cos, sin, qg.reshape(1, HEAD_DIM), kg.reshape(1, HEAD_DIM))


def _rope_tables(n_lat, n_ctx):
    n = HEAD_DIM // 4
    inv = ROPE_THETA ** (-np.arange(n, dtype=np.float32) / n)
    t = np.arange(n_lat)
    rows = (t // GRID_W).astype(np.float32)
    cols = (t % GRID_W).astype(np.float32)
    ang = np.concatenate([rows[:, None] * inv[None, :]] * 2 + [cols[:, None] * inv[None, :]] * 2, axis=1)
    ang = jnp.asarray(ang.astype(np.float32))
    sign = np.tile(np.concatenate([-np.ones(n, np.float32), np.ones(n, np.float32)]), 2)
    cos = jnp.concatenate([jnp.cos(ang), jnp.ones((n_ctx, HEAD_DIM), F32)], axis=0)
    sin = jnp.concatenate([jnp.sin(ang) * sign[None, :], jnp.zeros((n_ctx, HEAD_DIM), F32)], axis=0)
    return cos, sin


def _attn_kernel(*refs, tq, tk, n_kv, aliased):
    if aliased:
        q_ref, k_ref, v_ref, _, o_ref, s_ref, m_ref, l_ref, acc_ref = refs
    else:
        q_ref, k_ref, v_ref, o_ref, s_ref, m_ref, l_ref, acc_ref = refs
    n_chunks = n_kv // tk
    m_ref[...] = jnp.full(m_ref.shape, -jnp.inf, F32)
    l_ref[...] = jnp.zeros(l_ref.shape, F32)
    acc_ref[...] = jnp.zeros(acc_ref.shape, F32)

    def scores(r, ks):
        return lax.dot_general(q_ref[r], k_ref[0, ks, :], (((1,), (1,)), ((), ())), preferred_element_type=F32)

    s_ref[0] = scores(0, pl.ds(0, tk))

    def body(c, carry):
        ks = pl.ds(pl.multiple_of(c * tk, tk), tk)
        kn = pl.ds(pl.multiple_of(jnp.minimum(c + 1, n_chunks - 1) * tk, tk), tk)
        v = v_ref[ks, :]
        for r in range(KV_REP):
            cur, nxt = r % 2, (r + 1) % 2
            s_ref[nxt] = scores(r + 1, ks) if r + 1 < KV_REP else scores(0, kn)
            s = s_ref[cur]
            m_prev = m_ref[r]
            m_new = jnp.maximum(m_prev, jnp.max(s, axis=-1, keepdims=True))
            alpha = jnp.exp2(m_prev - m_new)
            p = jnp.exp2(s - m_new)
            l_ref[r] = alpha * l_ref[r] + jnp.sum(p, axis=-1, keepdims=True)
            acc_ref[r] = alpha * acc_ref[r] + jnp.dot(p.astype(BF16), v, preferred_element_type=F32)
            m_ref[r] = m_new
        return carry

    lax.fori_loop(0, n_chunks, body, 0)
    for r in range(KV_REP):
        o_ref[:, r * HEAD_DIM:(r + 1) * HEAD_DIM] = (acc_ref[r] / l_ref[r]).astype(o_ref.dtype)


def _attention(q, k, px, v_col_block, q_row0, n_q_rows, kv_row0, n_kv, tq, tk, out=None):
    n_heads, t, _ = q.shape
    n_kv_heads = k.shape[0]
    assert q_row0 % tq == 0 and n_q_rows % tq == 0 and kv_row0 % n_kv == 0 and n_kv % tk == 0
    qb0 = q_row0 // tq
    kb0 = kv_row0 // n_kv
    aliased = out is not None
    in_specs = [
        pl.BlockSpec((KV_REP, tq, HEAD_DIM), lambda g, i: (g, qb0 + i, 0)),
        pl.BlockSpec((1, n_kv, HEAD_DIM), lambda g, i: (g, kb0, 0)),
        pl.BlockSpec((n_kv, HEAD_DIM), lambda g, i: (kb0, v_col_block + g)),
    ]
    args = [q, k, px]
    if aliased:
        in_specs.append(pl.BlockSpec(memory_space=pl.ANY))
        args.append(out)
    return pl.pallas_call(
        functools.partial(_attn_kernel, tq=tq, tk=tk, n_kv=n_kv, aliased=aliased),
        grid=(n_kv_heads, n_q_rows // tq),
        in_specs=in_specs,
        out_specs=pl.BlockSpec((tq, KV_REP * HEAD_DIM), lambda g, i: (qb0 + i, g)),
        out_shape=jax.ShapeDtypeStruct((t, n_heads * HEAD_DIM), BF16),
        scratch_shapes=[
            pltpu.VMEM((2, tq, tk), F32),
            pltpu.VMEM((KV_REP, tq, 1), F32),
            pltpu.VMEM((KV_REP, tq, 1), F32),
            pltpu.VMEM((KV_REP, tq, HEAD_DIM), F32),
        ],
        input_output_aliases={3: 0} if aliased else {},
        compiler_params=_params(("parallel", "arbitrary")),
        name="attention",
    )(*args)


def _gmlp_kernel(p_ref, lng_ref, lnb_ref, ws_ref, bst_ref, o_ref, *, n_chunks, d_g):
    n_groups = d_g // GMLP_GROUP_W
    for b in range(n_chunks):
        rows = slice(b * GMLP_CHUNK, (b + 1) * GMLP_CHUNK)
        uv = jax.nn.gelu(p_ref[rows, :].astype(F32))
        u = uv[:, :d_g]
        v = uv[:, d_g:]
        vc = v - jnp.mean(v, axis=-1, keepdims=True)
        var = jnp.mean(vc * vc, axis=-1, keepdims=True)
        vn = (vc * lax.rsqrt(var + LN_EPS) * lng_ref[...] + lnb_ref[...]).astype(BF16)
        for g in range(n_groups):
            cols = slice(g * GMLP_GROUP_W, (g + 1) * GMLP_GROUP_W)
            vm = jnp.dot(ws_ref[g].astype(BF16), vn[:, cols], preferred_element_type=F32) + bst_ref[:, g:g + 1]
            o_ref[rows, cols] = (u[:, cols] * vm).astype(o_ref.dtype)


def _gmlp(px, col_block, d_g, ln_g, ln_b, ws, bs):
    t = px.shape[0]
    n_chunks = 2
    tr = n_chunks * GMLP_CHUNK
    assert t % tr == 0
    n_groups = d_g // GMLP_GROUP_W
    return pl.pallas_call(
        functools.partial(_gmlp_kernel, n_chunks=n_chunks, d_g=d_g),
        grid=(t // tr,),
        in_specs=[
            pl.BlockSpec((tr, 2 * d_g), lambda i: (i, col_block)),
            pl.BlockSpec((1, d_g), lambda i: (0, 0)),
            pl.BlockSpec((1, d_g), lambda i: (0, 0)),
            pl.BlockSpec((n_groups, GMLP_CHUNK, GMLP_CHUNK), lambda i: (0, 0, 0)),
            pl.BlockSpec((GMLP_CHUNK, n_groups), lambda i: (0, 0)),
        ],
        out_specs=pl.BlockSpec((tr, d_g), lambda i: (i, 0)),
        out_shape=jax.ShapeDtypeStruct((t, d_g), BF16),
        compiler_params=_params(("parallel",)),
        name="gmlp",
    )(px, ln_g.reshape(1, d_g), ln_b.reshape(1, d_g), ws, bs.T)


def _merge_kernel(hy_ref, at_ref, gm_ref, gc_ref, wbr_ref, gup_ref, gb_ref, o_ref, *, d_hy, d_at):
    gc = gc_ref[...]
    bounds = (0, d_hy, d_hy + d_at, wbr_ref.shape[0])
    merged = None
    for b, a_ref in enumerate((hy_ref, at_ref, gm_ref)):
        y = jnp.dot(a_ref[...], wbr_ref[bounds[b]:bounds[b + 1], :], preferred_element_type=F32)
        gate = jax.nn.sigmoid(jnp.dot(gc, gup_ref[b], preferred_element_type=F32) + gb_ref[b:b + 1, :])
        merged = gate * y if merged is None else merged + gate * y
    o_ref[...] = merged.astype(o_ref.dtype)


def _merge(o_hy, o_at, o_gm, px, gc_col_block, gate_rank, w_br, gate_up, gate_b, layer):
    t = px.shape[0]
    d_hy, d_at, d_gm = o_hy.shape[1], o_at.shape[1], o_gm.shape[1]
    d = w_br.shape[2]
    tm = _tile(t, 528, 16)
    tn = _tile(d, 1024, LANES)
    return pl.pallas_call(
        functools.partial(_merge_kernel, d_hy=d_hy, d_at=d_at),
        grid=(t // tm, d // tn),
        in_specs=[
            pl.BlockSpec((tm, d_hy), lambda i, j: (i, 0)),
            pl.BlockSpec((tm, d_at), lambda i, j: (i, 0)),
            pl.BlockSpec((tm, d_gm), lambda i, j: (i, 0)),
            pl.BlockSpec((tm, gate_rank), lambda i, j: (i, gc_col_block)),
            pl.BlockSpec((None, d_hy + d_at + d_gm, tn), lambda i, j: (layer, 0, j)),
            pl.BlockSpec((None, N_BRANCH, gate_rank, tn), lambda i, j: (layer, 0, 0, j)),
            pl.BlockSpec((None, N_BRANCH, tn), lambda i, j: (layer, 0, j)),
        ],
        out_specs=pl.BlockSpec((tm, tn), lambda i, j: (i, j)),
        out_shape=jax.ShapeDtypeStruct((t, d), BF16),
        compiler_params=_params(("parallel", "arbitrary")),
        name="merge",
    )(o_hy, o_at, o_gm, px, w_br, gate_up, gate_b)


def _outproj_kernel(a_ref, w_ref, x_ref, gate_ref, o_ref, *, n_lat, tm):
    y = jnp.dot(a_ref[...], w_ref[...], preferred_element_type=F32)
    rid = pl.program_id(0) * tm + lax.broadcasted_iota(jnp.int32, (tm, 1), 0)
    gate = jnp.where(rid >= n_lat, gate_ref[1:2, :], gate_ref[0:1, :])
    o_ref[...] = x_ref[...] + gate * y


def _outproj(a, w_all, layer, x, gate, n_lat):
    t, k = a.shape
    d = w_all.shape[2]
    tm = _tile(t, 528, 16)
    tn = _tile(d, 1024, LANES)
    return pl.pallas_call(
        functools.partial(_outproj_kernel, n_lat=n_lat, tm=tm),
        grid=(t // tm, d // tn),
        in_specs=[
            pl.BlockSpec((tm, k), lambda i, j: (i, 0)),
            pl.BlockSpec((None, k, tn), lambda i, j: (layer, 0, j)),
            pl.BlockSpec((tm, tn), lambda i, j: (i, j)),
            pl.BlockSpec((2, tn), lambda i, j: (0, j)),
        ],
        out_specs=pl.BlockSpec((tm, tn), lambda i, j: (i, j)),
        out_shape=jax.ShapeDtypeStruct((t, d), F32),
        compiler_params=_params(("parallel", "arbitrary")),
        name="outproj",
    )(a, w_all, x, gate)


def _split_bf16(w, parts):
    out = []
    rest = w.astype(F32)
    for _ in range(parts):
        piece = rest.astype(BF16)
        out.append(piece)
        rest = rest - piece.astype(F32)
    return jnp.stack(out)


def _router_kernel(x_ref, g_ref, sh_ref, sc_ref, rw_ref, rb_ref, h_ref, lg_ref, *, n_lat, tm):
    _modulate_rows(x_ref, g_ref, sh_ref, sc_ref, h_ref, pl.program_id(0) * tm, n_lat, tm)
    h = h_ref[...]
    logits = rb_ref[...]
    for part in range(rw_ref.shape[0]):
        logits = logits + jnp.dot(h, rw_ref[part], preferred_element_type=F32)
    lg_ref[...] = logits


def _router(x, g, sh, sc, rw, rb, n_lat):
    t, d = x.shape
    tm = _tile(t, 264, 16)
    return pl.pallas_call(
        functools.partial(_router_kernel, n_lat=n_lat, tm=tm),
        grid=(t // tm,),
        in_specs=[
            pl.BlockSpec((tm, d), lambda i: (i, 0)),
            pl.BlockSpec((1, d), lambda i: (0, 0)),
            pl.BlockSpec((2, d), lambda i: (0, 0)),
            pl.BlockSpec((2, d), lambda i: (0, 0)),
            pl.BlockSpec((rw.shape[0], d, ROUTER_COLS), lambda i: (0, 0, 0)),
            pl.BlockSpec((1, ROUTER_COLS), lambda i: (0, 0)),
        ],
        out_specs=[
            pl.BlockSpec((tm, d), lambda i: (i, 0)),
            pl.BlockSpec((tm, ROUTER_COLS), lambda i: (i, 0)),
        ],
        out_shape=[
            jax.ShapeDtypeStruct((t, d), BF16),
            jax.ShapeDtypeStruct((t, ROUTER_COLS), F32),
        ],
        compiler_params=_params(("parallel",)),
        name="router",
    )(x, g.reshape(1, d), sh, sc, rw, rb)


def _experts_kernel(te_ref, tb_ref, tf_ref, nu_ref, x_ref, rw_ref, wg_ref, wu_ref, wd_ref, o_ref,
                    wg_s, wu_s, wd_s):
    i = pl.program_id(0)

    @pl.when(i < nu_ref[0])
    def _():
        @pl.when(tf_ref[i] == 1)
        def _():
            wg_s[...] = wg_ref[...].astype(BF16)
            wu_s[...] = wu_ref[...].astype(BF16)
            wd_s[...] = wd_ref[...].astype(BF16)

        xg = x_ref[...]
        hg = jnp.dot(xg, wg_s[...], preferred_element_type=F32)
        hu = jnp.dot(xg, wu_s[...], preferred_element_type=F32)
        hid = (jax.nn.silu(hg) * hu * rw_ref[...]).astype(BF16)
        o_ref[...] = jnp.dot(hid, wd_s[...], preferred_element_type=F32).astype(o_ref.dtype)


def _experts(xg, roww, tile_expert, tile_block, tile_first, n_used, w_gate, w_up, w_down, layer, tme):
    p, d = xg.shape
    f = w_gate.shape[3]
    n_tiles = p // tme
    grid_spec = pltpu.PrefetchScalarGridSpec(
        num_scalar_prefetch=4,
        grid=(n_tiles,),
        in_specs=[
            pl.BlockSpec((tme, d), lambda i, te, tb, tf, nu: (tb[i], 0)),
            pl.BlockSpec((tme, 1), lambda i, te, tb, tf, nu: (tb[i], 0)),
            pl.BlockSpec((None, None, d, f), lambda i, te, tb, tf, nu: (layer, te[i], 0, 0)),
            pl.BlockSpec((None, None, d, f), lambda i, te, tb, tf, nu: (layer, te[i], 0, 0)),
            pl.BlockSpec((None, None, f, d), lambda i, te, tb, tf, nu: (layer, te[i], 0, 0)),
        ],
        out_specs=pl.BlockSpec((tme, d), lambda i, te, tb, tf, nu: (tb[i], 0)),
        scratch_shapes=[
            pltpu.VMEM((d, f), BF16),
            pltpu.VMEM((d, f), BF16),
            pltpu.VMEM((f, d), BF16),
        ],
    )
    return pl.pallas_call(
        _experts_kernel,
        grid_spec=grid_spec,
        out_shape=jax.ShapeDtypeStruct((p, d), BF16),
        compiler_params=_params(("arbitrary",)),
        name="experts",
    )(tile_expert, tile_block, tile_first, n_used, xg, roww, w_gate, w_up, w_down)


def _combine_kernel(x_ref, y0_ref, y1_ref, gate_ref, o_ref, *, n_lat, tm):
    rid = pl.program_id(0) * tm + lax.broadcasted_iota(jnp.int32, (tm, 1), 0)
    gate = jnp.where(rid >= n_lat, gate_ref[1:2, :], gate_ref[0:1, :])
    o_ref[...] = x_ref[...] + gate * (y0_ref[...].astype(F32) + y1_ref[...].astype(F32))


def _combine(x, y0, y1, gate, n_lat):
    t, d = x.shape
    tm = _tile(t, 264, 16)
    row = lambda: pl.BlockSpec((tm, d), lambda i: (i, 0))
    return pl.pallas_call(
        functools.partial(_combine_kernel, n_lat=n_lat, tm=tm),
        grid=(t // tm,),
        in_specs=[row(), row(), row(), pl.BlockSpec((2, d), lambda i: (0, 0))],
        out_specs=row(),
        out_shape=jax.ShapeDtypeStruct((t, d), F32),
        compiler_params=_params(("parallel",)),
        name="moe_combine",
    )(x, y0, y1, gate)


def _route(logits, n_groups, per_group):
    lg = logits[:, :n_groups]
    le_all = logits[:, n_groups:n_groups + n_groups * per_group].reshape(-1, n_groups, per_group)
    grp = jnp.argmax(lg, axis=-1)
    p_grp = jnp.take_along_axis(jax.nn.softmax(lg, axis=-1), grp[:, None], axis=-1)
    le = jnp.take_along_axis(le_all, grp[:, None, None], axis=1)[:, 0]
    top_p, top_i = lax.top_k(jax.nn.softmax(le, axis=-1), TOP_K)
    w = top_p / jnp.sum(top_p, axis=-1, keepdims=True) * p_grp
    eid = grp[:, None] * per_group + top_i
    return eid.astype(jnp.int32), w


def _dispatch(eid, w, n_experts, tme):
    t = eid.shape[0]
    n_pairs = t * TOP_K
    p_rows = n_pairs + n_experts * tme
    p_rows = -(-p_rows // tme) * tme
    flat_e = eid.reshape(-1)
    onehot = (flat_e[:, None] == jnp.arange(n_experts, dtype=jnp.int32)[None, :]).astype(jnp.int32)
    csum = jnp.cumsum(onehot, axis=0)
    rank = jnp.take_along_axis(csum, flat_e[:, None], axis=1)[:, 0] - 1
    counts = csum[-1]
    padded = -(-counts // tme) * tme
    ends = jnp.cumsum(padded)
    starts = ends - padded
    dest = starts[flat_e] + rank
    src_tok = (jnp.arange(p_rows, dtype=jnp.int32) % t).at[dest].set(jnp.arange(n_pairs, dtype=jnp.int32) // TOP_K)
    roww = jnp.zeros((p_rows,), F32).at[dest].set(w.reshape(-1))
    n_used = (ends[-1] // tme).astype(jnp.int32)
    tile_ids = jnp.arange(p_rows // tme, dtype=jnp.int32)
    tile_block = jnp.minimum(tile_ids, n_used - 1)
    tile_expert = jnp.minimum(
        jnp.searchsorted(ends, tile_block * tme, side="right").astype(jnp.int32), n_experts - 1)
    tile_first = (tile_block * tme == starts[tile_expert]).astype(jnp.int32)
    return src_tok, roww.reshape(-1, 1), dest.reshape(t, TOP_K), tile_expert, tile_block, tile_first, n_used.reshape(1)


def _rmsnorm_kernel(x_ref, g_ref, o_ref):
    xf = x_ref[...]
    ms = jnp.mean(xf * xf, axis=-1, keepdims=True)
    o_ref[...] = xf * lax.rsqrt(ms + EPS) * g_ref[...]


def _final_norm(x, g, n_rows):
    d = x.shape[1]
    tr = _tile(n_rows, 256, 8)
    return pl.pallas_call(
        _rmsnorm_kernel,
        grid=(n_rows // tr,),
        in_specs=[pl.BlockSpec((tr, d), lambda i: (i, 0)), pl.BlockSpec((1, d), lambda i: (0, 0))],
        out_specs=pl.BlockSpec((tr, d), lambda i: (i, 0)),
        out_shape=jax.ShapeDtypeStruct((n_rows, d), F32),
        compiler_params=_params(("parallel",)),
        name="final_norm",
    )(x, g.reshape(1, d))


SUBLANES = 8


def _filter_features(length, emb):
    bands = (emb - 1) // 2
    i = np.arange(length, dtype=np.float64)
    t = i / (length - 1)
    f = np.linspace(1e-4, bands - 1, bands)
    w = 2.0 * np.pi * i / length
    z = np.concatenate([t[:, None], np.cos(-f[None, :] * w[:, None]), np.sin(-f[None, :] * w[:, None])], axis=1)
    z = np.concatenate([z, z[::-1]], axis=0)
    emb_pad = -(-emb // SUBLANES) * SUBLANES
    z = np.pad(z, ((0, 0), (0, emb_pad - emb)))
    return jnp.asarray(z.astype(np.float32))


def _filter_kernel(z_ref, w1_ref, b1_ref, w2_ref, b2_ref, w3_ref, b3_ref, w4_ref, fr_ref, dl_ref,
                   k_ref, nrm_ref, *, d_hy):
    dot = functools.partial(jnp.dot, preferred_element_type=F32, precision=HIGHEST)

    @pl.when(jnp.logical_and(pl.program_id(0) == 0, pl.program_id(1) == 0))
    def _():
        nrm_ref[...] = jnp.zeros(nrm_ref.shape, F32)

    z = z_ref[...]
    fr = fr_ref[...]
    h = jnp.sin(fr * (dot(z, w1_ref[...]) + b1_ref[...]))
    h = jnp.sin(fr * (dot(h, w2_ref[...]) + b2_ref[...]))
    h = jnp.sin(fr * (dot(h, w3_ref[...]) + b3_ref[...]))
    h_hi = h.astype(BF16)
    h_lo = (h - h_hi.astype(F32)).astype(BF16)
    hk = (jnp.dot(h_hi, w4_ref[0], preferred_element_type=F32)
          + jnp.dot(h_lo, w4_ref[0], preferred_element_type=F32)
          + jnp.dot(h_hi, w4_ref[1], preferred_element_type=F32))
    window = jnp.exp(-z[:, 0:1] * dl_ref[...]) + HYENA_SHIFT
    for o in range(HYENA_ORDER):
        k = hk[:, o * d_hy:(o + 1) * d_hy] * window
        k_ref[o] = k.astype(k_ref.dtype)
        nrm_ref[o:o + 1, :] += jnp.sum(jnp.abs(k), axis=0, keepdims=True)


def _hyena_filters(length, w1, b1, w2, b2, w3, b3, w4, freq, d_hy, out_dtype):
    emb, hidden = w1.shape
    z = _filter_features(length, emb)
    emb_pad = z.shape[1]
    w1p = jnp.pad(w1, ((0, emb_pad - emb), (0, 0)))
    max_decay = math.log(HYENA_TARGET) / HYENA_FAST_PCT
    min_decay = math.log(HYENA_TARGET) / HYENA_SLOW_PCT
    deltas = jnp.asarray(np.abs(np.linspace(min_decay, max_decay, d_hy)).astype(np.float32)).reshape(1, d_hy)
    tr = _tile(length, 512, SUBLANES)
    per = length // tr
    vec = lambda a: a.reshape(1, hidden)
    full = lambda shape: pl.BlockSpec(shape, lambda d, i: (0,) * len(shape))
    return pl.pallas_call(
        functools.partial(_filter_kernel, d_hy=d_hy),
        grid=(2, per),
        in_specs=[
            pl.BlockSpec((tr, emb_pad), lambda d, i: (d * per + i, 0)),
            full((emb_pad, hidden)), full((1, hidden)),
            full((hidden, hidden)), full((1, hidden)),
            full((hidden, hidden)), full((1, hidden)),
            pl.BlockSpec((2, hidden, HYENA_ORDER * d_hy), lambda d, i: (0, 0, d)),
            full((1, hidden)), full((1, d_hy)),
        ],
        out_specs=[
            pl.BlockSpec((HYENA_ORDER, tr, d_hy), lambda d, i: (0, d * per + i, 0)),
            pl.BlockSpec((HYENA_ORDER, d_hy), lambda d, i: (0, 0)),
        ],
        out_shape=[
            jax.ShapeDtypeStruct((HYENA_ORDER, 2 * length, d_hy), out_dtype),
            jax.ShapeDtypeStruct((HYENA_ORDER, d_hy), F32),
        ],
        compiler_params=_params(("arbitrary", "arbitrary")),
        name="hyena_filter",
    )(z, w1p, vec(b1), w2, vec(b2), w3, vec(b3), _split_bf16(w4, 2), vec(freq), deltas)


CONV_ROWS = 64
CONV_HALO = 8


def _shortconv_kernel(p_ref, w_ref, b_ref, lat_ref, ctx_ref, pad_ref, *, n_lat, n_ctx):
    t = n_lat + n_ctx
    tc = p_ref.shape[1]
    pad_ref[0:CONV_HALO, :] = jnp.zeros((CONV_HALO, tc), F32)
    pad_ref[t + CONV_HALO:t + 2 * CONV_HALO, :] = jnp.zeros((CONV_HALO, tc), F32)

    def fill(r, carry):
        src = pl.ds(pl.multiple_of(r * CONV_ROWS, CONV_ROWS), CONV_ROWS)
        dst = pl.ds(pl.multiple_of(r * CONV_ROWS + CONV_HALO, CONV_HALO), CONV_ROWS)
        pad_ref[dst, :] = p_ref[src, :].astype(F32)
        return carry

    lax.fori_loop(0, t // CONV_ROWS, fill, 0)
    w0, w1, w2, bias = w_ref[0:1, :], w_ref[1:2, :], w_ref[2:3, :], b_ref[...]
    ext_rows = CONV_ROWS + 2 * CONV_HALO
    inner = slice(CONV_HALO, CONV_HALO + CONV_ROWS)

    def segment(out_ref, row0, n_rows):
        def body(r, carry):
            ext = pad_ref[pl.ds(pl.multiple_of(row0 + r * CONV_ROWS, CONV_HALO), ext_rows), :]
            rid = r * CONV_ROWS + lax.broadcasted_iota(jnp.int32, (CONV_ROWS, 1), 0)
            prev = jnp.where(rid == 0, 0.0, pltpu.roll(ext, 1, 0)[inner])
            nxt = jnp.where(rid == n_rows - 1, 0.0, pltpu.roll(ext, ext_rows - 1, 0)[inner])
            y = prev * w0 + ext[inner] * w1 + nxt * w2 + bias
            out_ref[0, pl.ds(pl.multiple_of(r * CONV_ROWS, CONV_ROWS), CONV_ROWS), :] = y.astype(out_ref.dtype)
            return carry

        lax.fori_loop(0, n_rows // CONV_ROWS, body, 0)

    segment(lat_ref, 0, n_lat)
    segment(ctx_ref, n_lat, n_ctx)


def _shortconv(px, col0, d_hy, conv_w, conv_b, n_lat, n_ctx):
    t = px.shape[0]
    tc = _tile(d_hy, 256, LANES)
    assert col0 % tc == 0 and n_lat % CONV_ROWS == 0 and n_ctx % CONV_ROWS == 0
    per = d_hy // tc
    n_grp = HYENA_ORDER + 1
    return pl.pallas_call(
        functools.partial(_shortconv_kernel, n_lat=n_lat, n_ctx=n_ctx),
        grid=(n_grp, per),
        in_specs=[
            pl.BlockSpec((t, tc), lambda g, j: (0, col0 // tc + g * per + j)),
            pl.BlockSpec((3, tc), lambda g, j: (0, g * per + j)),
            pl.BlockSpec((1, tc), lambda g, j: (0, g * per + j)),
        ],
        out_specs=[
            pl.BlockSpec((1, n_lat, tc), lambda g, j: (g, 0, j)),
            pl.BlockSpec((1, n_ctx, tc), lambda g, j: (g, 0, j)),
        ],
        out_shape=[
            jax.ShapeDtypeStruct((n_grp, n_lat, d_hy), BF16),
            jax.ShapeDtypeStruct((n_grp, n_ctx, d_hy), BF16),
        ],
        scratch_shapes=[pltpu.VMEM((t + 2 * CONV_HALO, tc), F32)],
        compiler_params=_params(("parallel", "parallel")),
        name="shortconv",
    )(px, conv_w, conv_b.reshape(1, -1))


DFT_N2 = 256
K1_ALIGN = 16
STAGE2_K1 = 8
T2_BLOCK = 16
STAGE2_UNROLL = 4


def _dft_constants(length):
    n = 2 * length
    n1 = n // DFT_N2
    assert n1 * DFT_N2 == n and n1 % 2 == 0
    n_t1 = n1 // 2
    k1_used = n1 // 2 + 1
    k1p = -(-k1_used // K1_ALIGN) * K1_ALIGN
    k1 = np.arange(k1p, dtype=np.float64)
    live = (k1 < k1_used).astype(np.float64)
    t1 = np.arange(n1, dtype=np.float64)
    ang1 = -2.0 * np.pi * k1[:, None] * t1[None, :] / n1
    f1 = np.stack([np.cos(ang1), np.sin(ang1)]) * live[None, :, None]
    k2 = np.arange(DFT_N2, dtype=np.float64)
    t2 = np.arange(DFT_N2, dtype=np.float64)
    ang2 = -2.0 * np.pi * t2[None, None, :] * (k1[:, None, None] + n1 * k2[None, :, None]) / n
    gr = np.cos(ang2) * live[:, None, None]
    gi = np.sin(ang2) * live[:, None, None]
    g = np.stack([gr, gi, gr.transpose(0, 2, 1), gi.transpose(0, 2, 1)])
    weight = np.where((k1 == 0) | (k1 == n1 // 2), 1.0, 2.0) * live
    angc = 2.0 * np.pi * np.arange(n_t1, dtype=np.float64)[:, None] * k1[None, :] / n1
    cinv = np.stack([np.cos(angc), np.sin(angc)]) * weight[None, None, :] / n
    as_bf16 = lambda a: jnp.asarray(a.astype(np.float32)).astype(BF16)
    eye = np.eye(T2_BLOCK)
    kron = lambda m: np.stack([np.kron(m[0], eye), np.kron(m[1], eye)])
    return dict(n1=n1, n_t1=n_t1, k1p=k1p, f1=as_bf16(kron(f1)), f1_sig=as_bf16(kron(f1[:, :, :n_t1])),
                g=as_bf16(g), cinv=as_bf16(kron(cinv)))


def _dft_stage1_kernel(f_ref, u_ref, a_ref):
    k, r, tc = u_ref.shape
    u = u_ref[...].reshape(k * r, tc)
    for part in range(2):
        a = jnp.dot(f_ref[part], u, preferred_element_type=F32).astype(a_ref.dtype)
        a_ref[part] = a.reshape(a_ref.shape[1:])


def _dft_stage1(f, u4, group):
    _, k, n2, c = u4.shape
    k1p = f.shape[1] // T2_BLOCK
    tc = _tile(c, 256, LANES)
    return pl.pallas_call(
        _dft_stage1_kernel,
        grid=(n2 // T2_BLOCK, c // tc),
        in_specs=[
            pl.BlockSpec((2, k1p * T2_BLOCK, k * T2_BLOCK), lambda i, j: (0, 0, 0)),
            pl.BlockSpec((None, k, T2_BLOCK, tc), lambda i, j: (group, 0, i, j)),
        ],
        out_specs=pl.BlockSpec((2, k1p, T2_BLOCK, tc), lambda i, j: (0, 0, i, j)),
        out_shape=jax.ShapeDtypeStruct((2, k1p, n2, c), BF16),
        compiler_params=_params(("parallel", "parallel")),
        name="dft_stage1",
    )(f, u4)


def _dft_stage2_filter_kernel(a_ref, g_ref, scale_ref, kf_ref, *, k1c):
    scale = scale_ref[...]

    def body(i, carry):
        ar, ai = a_ref[0, i], a_ref[1, i]
        gr, gi = g_ref[0, i], g_ref[1, i]
        dot = functools.partial(jnp.dot, preferred_element_type=F32)
        kf_ref[0, i] = (dot(gr, ar) - dot(gi, ai)) * scale
        kf_ref[1, i] = (dot(gr, ai) + dot(gi, ar)) * scale
        return carry

    lax.fori_loop(0, k1c, body, 0, unroll=STAGE2_UNROLL)


def _dft_stage2_conv_kernel(a_ref, g_ref, kf_ref, b_ref, *, k1c):
    def body(i, carry):
        ar, ai = a_ref[0, i], a_ref[1, i]
        gr, gi = g_ref[0, i], g_ref[1, i]
        dot = functools.partial(jnp.dot, preferred_element_type=F32)
        xr = dot(gr, ar) - dot(gi, ai)
        xi = dot(gr, ai) + dot(gi, ar)
        kr, ki = kf_ref[0, i], kf_ref[1, i]
        zr = (xr * kr - xi * ki).astype(BF16)
        zi = (xr * ki + xi * kr).astype(BF16)
        grt, git = g_ref[2, i], g_ref[3, i]
        b_ref[0, i] = (dot(grt, zr) + dot(git, zi)).astype(b_ref.dtype)
        b_ref[1, i] = (dot(grt, zi) - dot(git, zr)).astype(b_ref.dtype)
        return carry

    lax.fori_loop(0, k1c, body, 0, unroll=STAGE2_UNROLL)


def _dft_stage2(a4, g, kf=None, scale=None):
    _, k1p, n2, c = a4.shape
    k1c = STAGE2_K1
    tc = _tile(c, 256, LANES)
    a_spec = pl.BlockSpec((2, k1c, n2, tc), lambda i, j: (0, i, 0, j))
    g_spec = pl.BlockSpec((4, k1c, n2, n2), lambda i, j: (0, i, 0, 0))
    if kf is None:
        s_spec = pl.BlockSpec((1, tc), lambda i, j: (0, j))
        body, args, in_specs, dtype = _dft_stage2_filter_kernel, (a4, g, scale), [a_spec, g_spec, s_spec], F32
    else:
        body, args, in_specs, dtype = _dft_stage2_conv_kernel, (a4, g, kf), [a_spec, g_spec, a_spec], BF16
    return pl.pallas_call(
        functools.partial(body, k1c=k1c),
        grid=(k1p // k1c, c // tc),
        in_specs=in_specs,
        out_specs=a_spec,
        out_shape=jax.ShapeDtypeStruct(a4.shape, dtype),
        compiler_params=_params(("parallel", "arbitrary")),
        name="dft_stage2",
    )(*args)


def _dft_stage3_kernel(c_ref, b_ref, u_ref, m_ref, skip_ref, o_ref):
    _, k1p, r, tc = b_ref.shape
    rows = u_ref.shape[0] * r
    y = (jnp.dot(c_ref[0], b_ref[0].reshape(k1p * r, tc), preferred_element_type=F32)
         - jnp.dot(c_ref[1], b_ref[1].reshape(k1p * r, tc), preferred_element_type=F32))
    u = u_ref[...].reshape(rows, tc).astype(F32)
    out = m_ref[...].reshape(rows, tc).astype(F32) * (y + u * skip_ref[...])
    o_ref[0] = out.astype(o_ref.dtype).reshape(o_ref.shape[1:])


def _dft_stage3(cinv, b4, u4, u_group, m4, m_group, skip):
    _, k1p, n2, c = b4.shape
    n_t1 = cinv.shape[1] // T2_BLOCK
    tc = _tile(c, 256, LANES)
    sig_spec = lambda grp: pl.BlockSpec((None, n_t1, T2_BLOCK, tc), lambda i, j: (grp, 0, i, j))
    return pl.pallas_call(
        _dft_stage3_kernel,
        grid=(n2 // T2_BLOCK, c // tc),
        in_specs=[
            pl.BlockSpec((2, n_t1 * T2_BLOCK, k1p * T2_BLOCK), lambda i, j: (0, 0, 0)),
            pl.BlockSpec((2, k1p, T2_BLOCK, tc), lambda i, j: (0, 0, i, j)),
            sig_spec(u_group),
            sig_spec(m_group),
            pl.BlockSpec((1, tc), lambda i, j: (0, j)),
        ],
        out_specs=pl.BlockSpec((1, n_t1, T2_BLOCK, tc), lambda i, j: (0, 0, i, j)),
        out_shape=jax.ShapeDtypeStruct((1, n_t1, n2, c), BF16),
        compiler_params=_params(("parallel", "parallel")),
        name="dft_stage3",
    )(cinv, b4, u4, m4, skip.reshape(1, c))


def _hyena_latent(p_lat, kern, inv_norm, skip, consts):
    _, length, c = p_lat.shape
    n1, n_t1 = consts["n1"], consts["n_t1"]
    p4 = p_lat.reshape(HYENA_ORDER + 1, n_t1, DFT_N2, c)
    k4 = kern.reshape(HYENA_ORDER, n1, DFT_N2, c)
    u4, u_group = p4, 0
    for order in range(HYENA_ORDER):
        kf = _dft_stage2(_dft_stage1(consts["f1"], k4, order), consts["g"], scale=inv_norm[order:order + 1])
        b = _dft_stage2(_dft_stage1(consts["f1_sig"], u4, u_group), consts["g"], kf)
        u4 = _dft_stage3(consts["cinv"], b, u4, u_group, p4, order + 1, skip[order])
        u_group = 0
    return u4.reshape(length, c)


def _ctx_dft_constants(length):
    n = 2 * length
    k = np.arange(n, dtype=np.float64)
    ang = -2.0 * np.pi * k[:, None] * k[None, :] / n
    fwd = np.stack([np.cos(ang), np.sin(ang)])
    inv = np.stack([np.cos(ang[:length]), -np.sin(ang[:length])]) / n
    as_f32 = lambda a: jnp.asarray(a.astype(np.float32))
    return dict(fwd=as_f32(fwd), fwd_sig=as_f32(fwd[:, :, :length]), inv=as_f32(inv))


def _ctxconv_kernel(ff_ref, fs_ref, fi_ref, k_ref, scale_ref, u_ref, m_ref, skip_ref, o_ref):
    dot = functools.partial(jnp.dot, preferred_element_type=F32, precision=HIGHEST)
    k = k_ref[...] * scale_ref[...]
    u = u_ref[...].astype(F32)
    kr, ki = dot(ff_ref[0], k), dot(ff_ref[1], k)
    xr, xi = dot(fs_ref[0], u), dot(fs_ref[1], u)
    zr = xr * kr - xi * ki
    zi = xr * ki + xi * kr
    y = dot(fi_ref[0], zr) - dot(fi_ref[1], zi)
    o_ref[0] = (m_ref[...].astype(F32) * (y + u * skip_ref[...])).astype(o_ref.dtype)


def _ctxconv(consts, kern, order, inv_norm, u3, u_group, m3, m_group, skip):
    _, length, c = u3.shape
    n = 2 * length
    tc = _tile(c, 256, LANES)
    return pl.pallas_call(
        _ctxconv_kernel,
        grid=(c // tc,),
        in_specs=[
            pl.BlockSpec((2, n, n), lambda j: (0, 0, 0)),
            pl.BlockSpec((2, n, length), lambda j: (0, 0, 0)),
            pl.BlockSpec((2, length, n), lambda j: (0, 0, 0)),
            pl.BlockSpec((None, n, tc), lambda j: (order, 0, j)),
            pl.BlockSpec((1, tc), lambda j: (0, j)),
            pl.BlockSpec((None, length, tc), lambda j: (u_group, 0, j)),
            pl.BlockSpec((None, length, tc), lambda j: (m_group, 0, j)),
            pl.BlockSpec((1, tc), lambda j: (0, j)),
        ],
        out_specs=pl.BlockSpec((1, length, tc), lambda j: (0, 0, j)),
        out_shape=jax.ShapeDtypeStruct((1, length, c), BF16),
        compiler_params=_params(("parallel",)),
        name="ctxconv",
    )(consts["fwd"], consts["fwd_sig"], consts["inv"], kern, inv_norm[order:order + 1], u3, m3, skip.reshape(1, c))


def _hyena_context(p_ctx, kern, inv_norm, skip, consts):
    u3, u_group = p_ctx, 0
    for order in range(HYENA_ORDER):
        u3 = _ctxconv(consts, kern, order, inv_norm, u3, u_group, p_ctx, order + 1, skip[order])
        u_group = 0
    return u3[0]


def kernel(x, c, ctx, c_ctx, norm1_g, norm2_g, mod_a, mod_b, mod_bias, w_in, q_norm_g, k_norm_g, hy_conv_w, hy_conv_b, hy_w1, hy_b1, hy_w2, hy_b2, hy_w3, hy_b3, hy_w4, hy_freq, hy_skip, gm_ln_g, gm_ln_b, gm_ws, gm_bs, gate_up, gate_b, w_br, w_out, rt_group_w, rt_group_b, rt_expert_w, rt_expert_b, ex_w_gate, ex_w_up, ex_w_down, final_norm_g):
    bsz, seq, d = x.shape
    assert bsz == 1 and c.shape[0] == 1
    n_ctx = ctx.shape[1]
    depth = w_in.shape[0]
    t_all = seq + n_ctx
    d_hy = hy_skip.shape[2]
    d_gm = gm_ln_g.shape[1]
    d_at = w_br.shape[1] - d_hy - d_gm
    gate_rank = gate_up.shape[2]
    n_q = d_at // HEAD_DIM
    n_kv = n_q // KV_REP
    q_end = d_at
    v_start = q_end + n_kv * HEAD_DIM
    v_end = v_start + n_kv * HEAD_DIM
    hy_end = v_end + (HYENA_ORDER + 1) * d_hy
    gm_end = hy_end + 2 * d_gm
    n_groups = rt_group_w.shape[2]
    per_group = rt_expert_w.shape[3]
    n_experts = n_groups * per_group
    assert n_groups + n_experts <= ROUTER_COLS
    assert hy_end % (2 * d_gm) == 0 and gm_end % gate_rank == 0 and seq % ROW_CHUNK == 0
    tme = 256

    xs = jnp.concatenate([x[0], ctx[0]], axis=0)
    cond = jnp.stack([c[0], c_ctx], axis=0)
    cos, sin = _rope_tables(seq, n_ctx)

    tq = _tile(seq, 512, 16)
    tk = _tile(t_all, 2816, 256)
    tq_c = _tile(n_ctx, 256, 16)
    dft_lat = _dft_constants(seq)
    dft_ctx = _ctx_dft_constants(n_ctx)
    w_in_b, w_br_b, gate_up_b, w_out_b = (w.astype(BF16) for w in (w_in, w_br, gate_up, w_out))

    for l in range(depth):
        m = (jnp.dot(jnp.dot(jax.nn.silu(cond), mod_a[l], precision=HIGHEST), mod_b[l], precision=HIGHEST)
             + mod_bias[l]).reshape(2, N_MOD, d)

        px = _modproj(xs, norm1_g[l], m[:, 0], m[:, 1], w_in_b, l, seq)
        q, k = _qkprep(px, cos, sin, q_norm_g[l], k_norm_g[l], n_q, n_kv)
        o_at = _attention(q, k, px, v_start // HEAD_DIM, 0, seq, 0, t_all, tq, tk)
        o_at = _attention(q, k, px, v_start // HEAD_DIM, seq, n_ctx, seq, n_ctx, tq_c, n_ctx, out=o_at)

        hy_w = (hy_w1[l], hy_b1[l], hy_w2[l], hy_b2[l], hy_w3[l], hy_b3[l], hy_w4[l], hy_freq[l])
        p_lat, p_ctx = _shortconv(px, v_end, d_hy, hy_conv_w[l], hy_conv_b[l], seq, n_ctx)
        kern_x, norm_x = _hyena_filters(seq, *hy_w, d_hy, BF16)
        kern_c, norm_c = _hyena_filters(n_ctx, *hy_w, d_hy, F32)
        o_hy_x = _hyena_latent(p_lat, kern_x, 1.0 / norm_x, hy_skip[l], dft_lat)
        o_hy_c = _hyena_context(p_ctx, kern_c, 1.0 / norm_c, hy_skip[l], dft_ctx)
        o_hy = jnp.concatenate([o_hy_x, o_hy_c], axis=0)

        o_gm = _gmlp(px, hy_end // (2 * d_gm), d_gm, gm_ln_g[l], gm_ln_b[l], gm_ws[l], gm_bs[l])

        merged = _merge(o_hy, o_at, o_gm, px, gm_end // gate_rank, gate_rank, w_br_b, gate_up_b, gate_b, l)
        xs = _outproj(merged, w_out_b, l, xs, m[:, 2], seq)

        rw = jnp.concatenate([rt_group_w[l]] + [rt_expert_w[l, g] for g in range(n_groups)], axis=1)
        rw = jnp.pad(rw, ((0, 0), (0, ROUTER_COLS - rw.shape[1])))
        rb = jnp.concatenate([rt_group_b[l], rt_expert_b[l].reshape(-1)])
        rb = jnp.pad(rb, (0, ROUTER_COLS - rb.shape[0])).reshape(1, ROUTER_COLS)
        h2, logits = _router(xs, norm2_g[l], m[:, 3], m[:, 4], _split_bf16(rw, 3), rb, seq)
        eid, wts = _route(logits, n_groups, per_group)
        src_tok, roww, pos, tile_expert, tile_block, tile_first, n_used = _dispatch(eid, wts, n_experts, tme)
        take_rows = lambda a, idx: a.at[idx].get(mode="promise_in_bounds")
        xg = take_rows(h2, src_tok)
        yo = _experts(xg, roww, tile_expert, tile_block, tile_first, n_used,
                      ex_w_gate, ex_w_up, ex_w_down, l, tme)
        xs = _combine(xs, take_rows(yo, pos[:, 0]), take_rows(yo, pos[:, 1]), m[:, 5], seq)

    return _final_norm(xs, final_norm_g, seq).reshape(1, seq, d)
```

```python
import functools
import math

import numpy as np
import jax
import jax.numpy as jnp
from jax import lax
from jax.experimental import pallas as pl
from jax.experimental.pallas import tpu as pltpu

F32 = jnp.float32
BF16 = jnp.bfloat16
HIGHEST = lax.Precision.HIGHEST

HEAD_DIM = 128
KV_REP = 4
GRID_W = 64
ROPE_THETA = 10000.0
ATTN_SCALE = HEAD_DIM ** -0.5
LOG2_E = math.log2(math.e)
HYENA_ORDER = 2
HYENA_SHIFT = 0.05
HYENA_FAST_PCT = 0.3
HYENA_SLOW_PCT = 1.5
HYENA_TARGET = 1e-2
GMLP_CHUNK = 128
GMLP_GROUP_W = 128
N_BRANCH = 3
N_MOD = 6
TOP_K = 2
EPS = 1e-6
LN_EPS = 1e-5

LANES = 128
ROUTER_COLS = LANES
VMEM_LIMIT_MB = 56


def _params(semantics, vmem_mb=VMEM_LIMIT_MB):
    return pltpu.CompilerParams(dimension_semantics=semantics, vmem_limit_bytes=vmem_mb * 1024 * 1024)


def _tile(n, target, mult):
    best = None
    for d in range(mult, min(n, target) + 1, mult):
        if n % d == 0:
            best = d
    assert best is not None, (n, target, mult)
    return best


ROW_CHUNK = 16


def _modulate_rows(x_ref, g_ref, sh_ref, sc_ref, h_ref, row0, n_lat, n_rows):
    n_chunks = n_rows // ROW_CHUNK
    lat_chunks = jnp.clip((n_lat - row0) // ROW_CHUNK, 0, n_chunks)

    def segment(seg, lo, hi):
        gain = g_ref[...] * (1.0 + sc_ref[seg:seg + 1, :])
        shift = sh_ref[seg:seg + 1, :]

        def body(r, carry):
            rows = pl.ds(pl.multiple_of(r * ROW_CHUNK, ROW_CHUNK), ROW_CHUNK)
            xf = x_ref[rows, :]
            ms = jnp.mean(xf * xf, axis=-1, keepdims=True)
            h_ref[rows, :] = (xf * lax.rsqrt(ms + EPS) * gain + shift).astype(h_ref.dtype)
            return carry

        lax.fori_loop(lo, hi, body, 0)

    segment(0, 0, lat_chunks)
    segment(1, lat_chunks, n_chunks)


def _modproj_kernel(x_ref, g_ref, sh_ref, sc_ref, w_ref, o_ref, h_ref, *, n_lat, tm):
    i = pl.program_id(0)

    @pl.when(pl.program_id(1) == 0)
    def _():
        _modulate_rows(x_ref, g_ref, sh_ref, sc_ref, h_ref, i * tm, n_lat, tm)

    o_ref[...] = jnp.dot(h_ref[...], w_ref[...], preferred_element_type=F32).astype(o_ref.dtype)


def _modproj(x, g, sh, sc, w_all, layer, n_lat):
    t, d = x.shape
    n = w_all.shape[2]
    tm = _tile(t, 528, 16)
    tn = _tile(n, 512, LANES)
    return pl.pallas_call(
        functools.partial(_modproj_kernel, n_lat=n_lat, tm=tm),
        grid=(t // tm, n // tn),
        in_specs=[
            pl.BlockSpec((tm, d), lambda i, j: (i, 0)),
            pl.BlockSpec((1, d), lambda i, j: (0, 0)),
            pl.BlockSpec((2, d), lambda i, j: (0, 0)),
            pl.BlockSpec((2, d), lambda i, j: (0, 0)),
            pl.BlockSpec((None, d, tn), lambda i, j: (layer, 0, j)),
        ],
        out_specs=pl.BlockSpec((tm, tn), lambda i, j: (i, j)),
        out_shape=jax.ShapeDtypeStruct((t, n), BF16),
        scratch_shapes=[pltpu.VMEM((tm, d), BF16)],
        compiler_params=_params(("parallel", "arbitrary")),
        name="modproj",
    )(x, g.reshape(1, d), sh, sc, w_all)


def _qkprep_kernel(p_ref, cos_ref, sin_ref, qg_ref, kg_ref, q_ref, k_ref, *, n_q, n_kv):
    cos = cos_ref[...]
    sin = sin_ref[...]
    lane = lax.broadcasted_iota(jnp.int32, (1, HEAD_DIM), 1)
    first_half = (lane % (HEAD_DIM // 2)) < (HEAD_DIM // 4)

    def head(h, gain, scale):
        xh = p_ref[:, h * HEAD_DIM:(h + 1) * HEAD_DIM].astype(F32)
        ms = jnp.mean(xh * xh, axis=-1, keepdims=True)
        y = xh * lax.rsqrt(ms + EPS) * gain
        partner = jnp.where(first_half,
                            pltpu.roll(y, HEAD_DIM - HEAD_DIM // 4, 1),
                            pltpu.roll(y, HEAD_DIM // 4, 1))
        return (y * cos + partner * sin) * scale

    for h in range(n_q):
        q_ref[h] = head(h, qg_ref[...], ATTN_SCALE * LOG2_E).astype(q_ref.dtype)
    for h in range(n_kv):
        k_ref[h] = head(n_q + h, kg_ref[...], 1.0).astype(k_ref.dtype)


def _qkprep(px, cos, sin, qg, kg, n_q, n_kv):
    t = px.shape[0]
    tr = _tile(t, 264, 8)
    width = (n_q + n_kv) * HEAD_DIM
    return pl.pallas_call(
        functools.partial(_qkprep_kernel, n_q=n_q, n_kv=n_kv),
        grid=(t // tr,),
        in_specs=[
            pl.BlockSpec((tr, width), lambda i: (i, 0)),
            pl.BlockSpec((tr, HEAD_DIM), lambda i: (i, 0)),
            pl.BlockSpec((tr, HEAD_DIM), lambda i: (i, 0)),
            pl.BlockSpec((1, HEAD_DIM), lambda i: (0, 0)),
            pl.BlockSpec((1, HEAD_DIM), lambda i: (0, 0)),
        ],
        out_specs=[
            pl.BlockSpec((n_q, tr, HEAD_DIM), lambda i: (0, i, 0)),
            pl.BlockSpec((n_kv, tr, HEAD_DIM), lambda i: (0, i, 0)),
        ],
        out_shape=[
            jax.ShapeDtypeStruct((n_q, t, HEAD_DIM), BF16),
            jax.ShapeDtypeStruct((n_kv, t, HEAD_DIM), BF16),
        ],
        compiler_params=_params(("parallel",)),
        name="qkprep",
    )(px, cos, sin, qg.reshape(1, HEAD_DIM), kg.reshape(1, HEAD_DIM))


def _rope_tables(n_lat, n_ctx):
    n = HEAD_DIM // 4
    inv = ROPE_THETA ** (-np.arange(n, dtype=np.float32) / n)
    t = np.arange(n_lat)
    rows = (t // GRID_W).astype(np.float32)
    cols = (t % GRID_W).astype(np.float32)
    ang = np.concatenate([rows[:, None] * inv[None, :]] * 2 + [cols[:, None] * inv[None, :]] * 2, axis=1)
    ang = jnp.asarray(ang.astype(np.float32))
    sign = np.tile(np.concatenate([-np.ones(n, np.float32), np.ones(n, np.float32)]), 2)
    cos = jnp.concatenate([jnp.cos(ang), jnp.ones((n_ctx, HEAD_DIM), F32)], axis=0)
    sin = jnp.concatenate([jnp.sin(ang) * sign[None, :], jnp.zeros((n_ctx, HEAD_DIM), F32)], axis=0)
    return cos, sin


def _attn_kernel(*refs, tq, tk, n_kv, aliased):
    if aliased:
        q_ref, qn_ref, k_ref, v_ref, _, o_ref, s_ref, m_ref, l_ref, acc_ref = refs
    else:
        q_ref, qn_ref, k_ref, v_ref, o_ref, s_ref, m_ref, l_ref, acc_ref = refs
    n_chunks = n_kv // tk
    assert (n_chunks * KV_REP) % 2 == 0
    m_ref[...] = jnp.full(m_ref.shape, -jnp.inf, F32)
    l_ref[...] = jnp.zeros(l_ref.shape, F32)
    acc_ref[...] = jnp.zeros(acc_ref.shape, F32)

    def scores(q, ks):
        return lax.dot_general(q, k_ref[0, ks, :], (((1,), (1,)), ((), ())), preferred_element_type=F32)

    @pl.when(pl.program_id(1) == 0)
    def _():
        s_ref[0] = scores(q_ref[0], pl.ds(0, tk))

    def body(c, carry):
        ks = pl.ds(pl.multiple_of(c * tk, tk), tk)
        last = c + 1 == n_chunks
        kn = pl.ds(pl.multiple_of(jnp.where(last, 0, c + 1) * tk, tk), tk)
        v = v_ref[ks, :]
        for r in range(KV_REP):
            cur, nxt = r % 2, (r + 1) % 2
            if r + 1 < KV_REP:
                s_ref[nxt] = scores(q_ref[r + 1], ks)
            else:
                s_ref[nxt] = scores(jnp.where(last, qn_ref[0], q_ref[0]), kn)
            s = s_ref[cur]
            m_prev = m_ref[r]
            m_new = jnp.maximum(m_prev, jnp.max(s, axis=-1, keepdims=True))
            alpha = jnp.exp2(m_prev - m_new)
            p = jnp.exp2(s - m_new)
            l_ref[r] = alpha * l_ref[r] + jnp.sum(p, axis=-1, keepdims=True)
            acc_ref[r] = alpha * acc_ref[r] + jnp.dot(p.astype(BF16), v, preferred_element_type=F32)
            m_ref[r] = m_new
        return carry

    lax.fori_loop(0, n_chunks, body, 0)
    for r in range(KV_REP):
        o_ref[:, r * HEAD_DIM:(r + 1) * HEAD_DIM] = (acc_ref[r] / l_ref[r]).astype(o_ref.dtype)


def _attention(q, k, px, v_col_block, q_row0, n_q_rows, kv_row0, n_kv, tq, tk, out=None):
    n_heads, t, _ = q.shape
    n_kv_heads = k.shape[0]
    assert q_row0 % tq == 0 and n_q_rows % tq == 0 and kv_row0 % n_kv == 0 and n_kv % tk == 0
    qb0 = q_row0 // tq
    kb0 = kv_row0 // n_kv
    aliased = out is not None
    n_tiles = n_q_rows // tq
    in_specs = [
        pl.BlockSpec((KV_REP, tq, HEAD_DIM), lambda g, i: (g, qb0 + i, 0)),
        pl.BlockSpec((1, tq, HEAD_DIM), lambda g, i: (g * KV_REP, qb0 + jnp.minimum(i + 1, n_tiles - 1), 0)),
        pl.BlockSpec((1, n_kv, HEAD_DIM), lambda g, i: (g, kb0, 0)),
        pl.BlockSpec((n_kv, HEAD_DIM), lambda g, i: (kb0, v_col_block + g)),
    ]
    args = [q, q, k, px]
    if aliased:
        in_specs.append(pl.BlockSpec(memory_space=pl.ANY))
        args.append(out)
    return pl.pallas_call(
        functools.partial(_attn_kernel, tq=tq, tk=tk, n_kv=n_kv, aliased=aliased),
        grid=(n_kv_heads, n_q_rows // tq),
        in_specs=in_specs,
        out_specs=pl.BlockSpec((tq, KV_REP * HEAD_DIM), lambda g, i: (qb0 + i, g)),
        out_shape=jax.ShapeDtypeStruct((t, n_heads * HEAD_DIM), BF16),
        scratch_shapes=[
            pltpu.VMEM((2, tq, tk), F32),
            pltpu.VMEM((KV_REP, tq, 1), F32),
            pltpu.VMEM((KV_REP, tq, 1), F32),
            pltpu.VMEM((KV_REP, tq, HEAD_DIM), F32),
        ],
        input_output_aliases={4: 0} if aliased else {},
        compiler_params=_params(("parallel", "arbitrary")),
        name="attention",
    )(*args)


def _gmlp_kernel(p_ref, lng_ref, lnb_ref, ws_ref, bst_ref, o_ref, *, n_chunks, d_g):
    n_groups = d_g // GMLP_GROUP_W
    for b in range(n_chunks):
        rows = slice(b * GMLP_CHUNK, (b + 1) * GMLP_CHUNK)
        uv = jax.nn.gelu(p_ref[rows, :].astype(F32))
        u = uv[:, :d_g]
        v = uv[:, d_g:]
        vc = v - jnp.mean(v, axis=-1, keepdims=True)
        var = jnp.mean(vc * vc, axis=-1, keepdims=True)
        vn = (vc * lax.rsqrt(var + LN_EPS) * lng_ref[...] + lnb_ref[...]).astype(BF16)
        for g in range(n_groups):
            cols = slice(g * GMLP_GROUP_W, (g + 1) * GMLP_GROUP_W)
            vm = jnp.dot(ws_ref[g].astype(BF16), vn[:, cols], preferred_element_type=F32) + bst_ref[:, g:g + 1]
            o_ref[rows, cols] = (u[:, cols] * vm).astype(o_ref.dtype)


def _gmlp(px, col_block, d_g, ln_g, ln_b, ws, bs):
    t = px.shape[0]
    n_chunks = 2
    tr = n_chunks * GMLP_CHUNK
    assert t % tr == 0
    n_groups = d_g // GMLP_GROUP_W
    return pl.pallas_call(
        functools.partial(_gmlp_kernel, n_chunks=n_chunks, d_g=d_g),
        grid=(t // tr,),
        in_specs=[
            pl.BlockSpec((tr, 2 * d_g), lambda i: (i, col_block)),
            pl.BlockSpec((1, d_g), lambda i: (0, 0)),
            pl.BlockSpec((1, d_g), lambda i: (0, 0)),
            pl.BlockSpec((n_groups, GMLP_CHUNK, GMLP_CHUNK), lambda i: (0, 0, 0)),
            pl.BlockSpec((GMLP_CHUNK, n_groups), lambda i: (0, 0)),
        ],
        out_specs=pl.BlockSpec((tr, d_g), lambda i: (i, 0)),
        out_shape=jax.ShapeDtypeStruct((t, d_g), BF16),
        compiler_params=_params(("parallel",)),
        name="gmlp",
    )(px, ln_g.reshape(1, d_g), ln_b.reshape(1, d_g), ws, bs.T)


def _merge_kernel(hy_ref, at_ref, gm_ref, gc_ref, wbr_ref, gup_ref, gb_ref, o_ref, *, d_hy, d_at):
    gc = gc_ref[...]
    bounds = (0, d_hy, d_hy + d_at, wbr_ref.shape[0])
    merged = None
    for b, a_ref in enumerate((hy_ref, at_ref, gm_ref)):
        y = jnp.dot(a_ref[...], wbr_ref[bounds[b]:bounds[b + 1], :], preferred_element_type=F32)
        gate = jax.nn.sigmoid(jnp.dot(gc, gup_ref[b], preferred_element_type=F32) + gb_ref[b:b + 1, :])
        merged = gate * y if merged is None else merged + gate * y
    o_ref[...] = merged.astype(o_ref.dtype)


def _merge(o_hy, o_at, o_gm, px, gc_col_block, gate_rank, w_br, gate_up, gate_b, layer):
    t = px.shape[0]
    d_hy, d_at, d_gm = o_hy.shape[1], o_at.shape[1], o_gm.shape[1]
    d = w_br.shape[2]
    tm = _tile(t, 528, 16)
    tn = _tile(d, 1024, LANES)
    return pl.pallas_call(
        functools.partial(_merge_kernel, d_hy=d_hy, d_at=d_at),
        grid=(t // tm, d // tn),
        in_specs=[
            pl.BlockSpec((tm, d_hy), lambda i, j: (i, 0)),
            pl.BlockSpec((tm, d_at), lambda i, j: (i, 0)),
            pl.BlockSpec((tm, d_gm), lambda i, j: (i, 0)),
            pl.BlockSpec((tm, gate_rank), lambda i, j: (i, gc_col_block)),
            pl.BlockSpec((None, d_hy + d_at + d_gm, tn), lambda i, j: (layer, 0, j)),
            pl.BlockSpec((None, N_BRANCH, gate_rank, tn), lambda i, j: (layer, 0, 0, j)),
            pl.BlockSpec((None, N_BRANCH, tn), lambda i, j: (layer, 0, j)),
        ],
        out_specs=pl.BlockSpec((tm, tn), lambda i, j: (i, j)),
        out_shape=jax.ShapeDtypeStruct((t, d), BF16),
        compiler_params=_params(("parallel", "arbitrary")),
        name="merge",
    )(o_hy, o_at, o_gm, px, w_br, gate_up, gate_b)


def _outproj_kernel(a_ref, w_ref, x_ref, gate_ref, o_ref, *, n_lat, tm):
    y = jnp.dot(a_ref[...], w_ref[...], preferred_element_type=F32)
    rid = pl.program_id(0) * tm + lax.broadcasted_iota(jnp.int32, (tm, 1), 0)
    gate = jnp.where(rid >= n_lat, gate_ref[1:2, :], gate_ref[0:1, :])
    o_ref[...] = x_ref[...] + gate * y


def _outproj(a, w_all, layer, x, gate, n_lat):
    t, k = a.shape
    d = w_all.shape[2]
    tm = _tile(t, 528, 16)
    tn = _tile(d, 1024, LANES)
    return pl.pallas_call(
        functools.partial(_outproj_kernel, n_lat=n_lat, tm=tm),
        grid=(t // tm, d // tn),
        in_specs=[
            pl.BlockSpec((tm, k), lambda i, j: (i, 0)),
            pl.BlockSpec((None, k, tn), lambda i, j: (layer, 0, j)),
            pl.BlockSpec((tm, tn), lambda i, j: (i, j)),
            pl.BlockSpec((2, tn), lambda i, j: (0, j)),
        ],
        out_specs=pl.BlockSpec((tm, tn), lambda i, j: (i, j)),
        out_shape=jax.ShapeDtypeStruct((t, d), F32),
        compiler_params=_params(("parallel", "arbitrary")),
        name="outproj",
    )(a, w_all, x, gate)


def _split_bf16(w, parts):
    out = []
    rest = w.astype(F32)
    for _ in range(parts):
        piece = rest.astype(BF16)
        out.append(piece)
        rest = rest - piece.astype(F32)
    return jnp.stack(out)


def _router_kernel(x_ref, g_ref, sh_ref, sc_ref, rw_ref, rb_ref, h_ref, lg_ref, *, n_lat, tm):
    _modulate_rows(x_ref, g_ref, sh_ref, sc_ref, h_ref, pl.program_id(0) * tm, n_lat, tm)
    h = h_ref[...]
    logits = rb_ref[...]
    for part in range(rw_ref.shape[0]):
        logits = logits + jnp.dot(h, rw_ref[part], preferred_element_type=F32)
    lg_ref[...] = logits


def _router(x, g, sh, sc, rw, rb, n_lat):
    t, d = x.shape
    tm = _tile(t, 264, 16)
    return pl.pallas_call(
        functools.partial(_router_kernel, n_lat=n_lat, tm=tm),
        grid=(t // tm,),
        in_specs=[
            pl.BlockSpec((tm, d), lambda i: (i, 0)),
            pl.BlockSpec((1, d), lambda i: (0, 0)),
            pl.BlockSpec((2, d), lambda i: (0, 0)),
            pl.BlockSpec((2, d), lambda i: (0, 0)),
            pl.BlockSpec((rw.shape[0], d, ROUTER_COLS), lambda i: (0, 0, 0)),
            pl.BlockSpec((1, ROUTER_COLS), lambda i: (0, 0)),
        ],
        out_specs=[
            pl.BlockSpec((tm, d), lambda i: (i, 0)),
            pl.BlockSpec((tm, ROUTER_COLS), lambda i: (i, 0)),
        ],
        out_shape=[
            jax.ShapeDtypeStruct((t, d), BF16),
            jax.ShapeDtypeStruct((t, ROUTER_COLS), F32),
        ],
        compiler_params=_params(("parallel",)),
        name="router",
    )(x, g.reshape(1, d), sh, sc, rw, rb)


def _experts_kernel(te_ref, tb_ref, tf_ref, nu_ref, x_ref, rw_ref, wg_ref, wu_ref, wd_ref, o_ref,
                    wg_s, wu_s, wd_s):
    i = pl.program_id(0)

    @pl.when(i < nu_ref[0])
    def _():
        @pl.when(tf_ref[i] == 1)
        def _():
            wg_s[...] = wg_ref[...].astype(BF16)
            wu_s[...] = wu_ref[...].astype(BF16)
            wd_s[...] = wd_ref[...].astype(BF16)

        xg = x_ref[...]
        hg = jnp.dot(xg, wg_s[...], preferred_element_type=F32)
        hu = jnp.dot(xg, wu_s[...], preferred_element_type=F32)
        hid = (jax.nn.silu(hg) * hu * rw_ref[...]).astype(BF16)
        o_ref[...] = jnp.dot(hid, wd_s[...], preferred_element_type=F32).astype(o_ref.dtype)


def _experts(xg, roww, tile_expert, tile_block, tile_first, n_used, w_gate, w_up, w_down, layer, tme):
    p, d = xg.shape
    f = w_gate.shape[3]
    n_tiles = p // tme
    grid_spec = pltpu.PrefetchScalarGridSpec(
        num_scalar_prefetch=4,
        grid=(n_tiles,),
        in_specs=[
            pl.BlockSpec((tme, d), lambda i, te, tb, tf, nu: (tb[i], 0)),
            pl.BlockSpec((tme, 1), lambda i, te, tb, tf, nu: (tb[i], 0)),
            pl.BlockSpec((None, None, d, f), lambda i, te, tb, tf, nu: (layer, te[i], 0, 0)),
            pl.BlockSpec((None, None, d, f), lambda i, te, tb, tf, nu: (layer, te[i], 0, 0)),
            pl.BlockSpec((None, None, f, d), lambda i, te, tb, tf, nu: (layer, te[i], 0, 0)),
        ],
        out_specs=pl.BlockSpec((tme, d), lambda i, te, tb, tf, nu: (tb[i], 0)),
        scratch_shapes=[
            pltpu.VMEM((d, f), BF16),
            pltpu.VMEM((d, f), BF16),
            pltpu.VMEM((f, d), BF16),
        ],
    )
    return pl.pallas_call(
        _experts_kernel,
        grid_spec=grid_spec,
        out_shape=jax.ShapeDtypeStruct((p, d), BF16),
        compiler_params=_params(("arbitrary",)),
        name="experts",
    )(tile_expert, tile_block, tile_first, n_used, xg, roww, w_gate, w_up, w_down)


def _combine_kernel(x_ref, y0_ref, y1_ref, gate_ref, o_ref, *, n_lat, tm):
    rid = pl.program_id(0) * tm + lax.broadcasted_iota(jnp.int32, (tm, 1), 0)
    gate = jnp.where(rid >= n_lat, gate_ref[1:2, :], gate_ref[0:1, :])
    o_ref[...] = x_ref[...] + gate * (y0_ref[...].astype(F32) + y1_ref[...].astype(F32))


def _combine(x, y0, y1, gate, n_lat):
    t, d = x.shape
    tm = _tile(t, 264, 16)
    row = lambda: pl.BlockSpec((tm, d), lambda i: (i, 0))
    return pl.pallas_call(
        functools.partial(_combine_kernel, n_lat=n_lat, tm=tm),
        grid=(t // tm,),
        in_specs=[row(), row(), row(), pl.BlockSpec((2, d), lambda i: (0, 0))],
        out_specs=row(),
        out_shape=jax.ShapeDtypeStruct((t, d), F32),
        compiler_params=_params(("parallel",)),
        name="moe_combine",
    )(x, y0, y1, gate)


def _route(logits, n_groups, per_group):
    lg = logits[:, :n_groups]
    le_all = logits[:, n_groups:n_groups + n_groups * per_group].reshape(-1, n_groups, per_group)
    grp = jnp.argmax(lg, axis=-1)
    p_grp = jnp.take_along_axis(jax.nn.softmax(lg, axis=-1), grp[:, None], axis=-1)
    le = jnp.take_along_axis(le_all, grp[:, None, None], axis=1)[:, 0]
    top_p, top_i = lax.top_k(jax.nn.softmax(le, axis=-1), TOP_K)
    w = top_p / jnp.sum(top_p, axis=-1, keepdims=True) * p_grp
    eid = grp[:, None] * per_group + top_i
    return eid.astype(jnp.int32), w


def _dispatch(eid, w, n_experts, tme):
    t = eid.shape[0]
    n_pairs = t * TOP_K
    p_rows = n_pairs + n_experts * tme
    p_rows = -(-p_rows // tme) * tme
    flat_e = eid.reshape(-1)
    onehot = (flat_e[:, None] == jnp.arange(n_experts, dtype=jnp.int32)[None, :]).astype(jnp.int32)
    csum = jnp.cumsum(onehot, axis=0)
    rank = jnp.take_along_axis(csum, flat_e[:, None], axis=1)[:, 0] - 1
    counts = csum[-1]
    padded = -(-counts // tme) * tme
    ends = jnp.cumsum(padded)
    starts = ends - padded
    dest = starts[flat_e] + rank
    src_tok = (jnp.arange(p_rows, dtype=jnp.int32) % t).at[dest].set(jnp.arange(n_pairs, dtype=jnp.int32) // TOP_K)
    roww = jnp.zeros((p_rows,), F32).at[dest].set(w.reshape(-1))
    n_used = (ends[-1] // tme).astype(jnp.int32)
    tile_ids = jnp.arange(p_rows // tme, dtype=jnp.int32)
    tile_block = jnp.minimum(tile_ids, n_used - 1)
    tile_expert = jnp.minimum(
        jnp.searchsorted(ends, tile_block * tme, side="right").astype(jnp.int32), n_experts - 1)
    tile_first = (tile_block * tme == starts[tile_expert]).astype(jnp.int32)
    return src_tok, roww.reshape(-1, 1), dest.reshape(t, TOP_K), tile_expert, tile_block, tile_first, n_used.reshape(1)


def _rmsnorm_kernel(x_ref, g_ref, o_ref):
    xf = x_ref[...]
    ms = jnp.mean(xf * xf, axis=-1, keepdims=True)
    o_ref[...] = xf * lax.rsqrt(ms + EPS) * g_ref[...]


def _final_norm(x, g, n_rows):
    d = x.shape[1]
    tr = _tile(n_rows, 256, 8)
    return pl.pallas_call(
        _rmsnorm_kernel,
        grid=(n_rows // tr,),
        in_specs=[pl.BlockSpec((tr, d), lambda i: (i, 0)), pl.BlockSpec((1, d), lambda i: (0, 0))],
        out_specs=pl.BlockSpec((tr, d), lambda i: (i, 0)),
        out_shape=jax.ShapeDtypeStruct((n_rows, d), F32),
        compiler_params=_params(("parallel",)),
        name="final_norm",
    )(x, g.reshape(1, d))


SUBLANES = 8


def _filter_features(length, emb):
    bands = (emb - 1) // 2
    i = np.arange(length, dtype=np.float64)
    t = i / (length - 1)
    f = np.linspace(1e-4, bands - 1, bands)
    w = 2.0 * np.pi * i / length
    z = np.concatenate([t[:, None], np.cos(-f[None, :] * w[:, None]), np.sin(-f[None, :] * w[:, None])], axis=1)
    z = np.concatenate([z, z[::-1]], axis=0)
    emb_pad = -(-emb // SUBLANES) * SUBLANES
    z = np.pad(z, ((0, 0), (0, emb_pad - emb)))
    return jnp.asarray(z.astype(np.float32))


def _filter_kernel(z_ref, w1_ref, b1_ref, w2_ref, b2_ref, w3_ref, b3_ref, w4_ref, fr_ref, dl_ref,
                   k_ref, nrm_ref, *, d_hy):
    dot = functools.partial(jnp.dot, preferred_element_type=F32, precision=HIGHEST)

    @pl.when(jnp.logical_and(pl.program_id(0) == 0, pl.program_id(1) == 0))
    def _():
        nrm_ref[...] = jnp.zeros(nrm_ref.shape, F32)

    z = z_ref[...]
    fr = fr_ref[...]
    h = jnp.sin(fr * (dot(z, w1_ref[...]) + b1_ref[...]))
    h = jnp.sin(fr * (dot(h, w2_ref[...]) + b2_ref[...]))
    h = jnp.sin(fr * (dot(h, w3_ref[...]) + b3_ref[...]))
    h_hi = h.astype(BF16)
    h_lo = (h - h_hi.astype(F32)).astype(BF16)
    hk = (jnp.dot(h_hi, w4_ref[0], preferred_element_type=F32)
          + jnp.dot(h_lo, w4_ref[0], preferred_element_type=F32)
          + jnp.dot(h_hi, w4_ref[1], preferred_element_type=F32))
    window = jnp.exp(-z[:, 0:1] * dl_ref[...]) + HYENA_SHIFT
    for o in range(HYENA_ORDER):
        k = hk[:, o * d_hy:(o + 1) * d_hy] * window
        k_ref[o] = k.astype(k_ref.dtype)
        nrm_ref[o:o + 1, :] += jnp.sum(jnp.abs(k), axis=0, keepdims=True)


def _hyena_filters(length, w1, b1, w2, b2, w3, b3, w4, freq, d_hy, out_dtype):
    emb, hidden = w1.shape
    z = _filter_features(length, emb)
    emb_pad = z.shape[1]
    w1p = jnp.pad(w1, ((0, emb_pad - emb), (0, 0)))
    max_decay = math.log(HYENA_TARGET) / HYENA_FAST_PCT
    min_decay = math.log(HYENA_TARGET) / HYENA_SLOW_PCT
    deltas = jnp.asarray(np.abs(np.linspace(min_decay, max_decay, d_hy)).astype(np.float32)).reshape(1, d_hy)
    tr = _tile(length, 512, SUBLANES)
    per = length // tr
    vec = lambda a: a.reshape(1, hidden)
    full = lambda shape: pl.BlockSpec(shape, lambda d, i: (0,) * len(shape))
    return pl.pallas_call(
        functools.partial(_filter_kernel, d_hy=d_hy),
        grid=(2, per),
        in_specs=[
            pl.BlockSpec((tr, emb_pad), lambda d, i: (d * per + i, 0)),
            full((emb_pad, hidden)), full((1, hidden)),
            full((hidden, hidden)), full((1, hidden)),
            full((hidden, hidden)), full((1, hidden)),
            pl.BlockSpec((2, hidden, HYENA_ORDER * d_hy), lambda d, i: (0, 0, d)),
            full((1, hidden)), full((1, d_hy)),
        ],
        out_specs=[
            pl.BlockSpec((HYENA_ORDER, tr, d_hy), lambda d, i: (0, d * per + i, 0)),
            pl.BlockSpec((HYENA_ORDER, d_hy), lambda d, i: (0, 0)),
        ],
        out_shape=[
            jax.ShapeDtypeStruct((HYENA_ORDER, 2 * length, d_hy), out_dtype),
            jax.ShapeDtypeStruct((HYENA_ORDER, d_hy), F32),
        ],
        compiler_params=_params(("arbitrary", "arbitrary")),
        name="hyena_filter",
    )(z, w1p, vec(b1), w2, vec(b2), w3, vec(b3), _split_bf16(w4, 2), vec(freq), deltas)


CONV_ROWS = 64
CONV_HALO = 8


def _shortconv_kernel(p_ref, w_ref, b_ref, lat_ref, ctx_ref, pad_ref, *, n_lat, n_ctx):
    t = n_lat + n_ctx
    tc = p_ref.shape[1]
    pad_ref[0:CONV_HALO, :] = jnp.zeros((CONV_HALO, tc), F32)
    pad_ref[t + CONV_HALO:t + 2 * CONV_HALO, :] = jnp.zeros((CONV_HALO, tc), F32)

    def fill(r, carry):
        src = pl.ds(pl.multiple_of(r * CONV_ROWS, CONV_ROWS), CONV_ROWS)
        dst = pl.ds(pl.multiple_of(r * CONV_ROWS + CONV_HALO, CONV_HALO), CONV_ROWS)
        pad_ref[dst, :] = p_ref[src, :].astype(F32)
        return carry

    lax.fori_loop(0, t // CONV_ROWS, fill, 0)
    w0, w1, w2, bias = w_ref[0:1, :], w_ref[1:2, :], w_ref[2:3, :], b_ref[...]
    ext_rows = CONV_ROWS + 2 * CONV_HALO
    inner = slice(CONV_HALO, CONV_HALO + CONV_ROWS)

    def segment(out_ref, row0, n_rows):
        def body(r, carry):
            ext = pad_ref[pl.ds(pl.multiple_of(row0 + r * CONV_ROWS, CONV_HALO), ext_rows), :]
            rid = r * CONV_ROWS + lax.broadcasted_iota(jnp.int32, (CONV_ROWS, 1), 0)
            prev = jnp.where(rid == 0, 0.0, pltpu.roll(ext, 1, 0)[inner])
            nxt = jnp.where(rid == n_rows - 1, 0.0, pltpu.roll(ext, ext_rows - 1, 0)[inner])
            y = prev * w0 + ext[inner] * w1 + nxt * w2 + bias
            out_ref[0, pl.ds(pl.multiple_of(r * CONV_ROWS, CONV_ROWS), CONV_ROWS), :] = y.astype(out_ref.dtype)
            return carry

        lax.fori_loop(0, n_rows // CONV_ROWS, body, 0)

    segment(lat_ref, 0, n_lat)
    segment(ctx_ref, n_lat, n_ctx)


def _shortconv(px, col0, d_hy, conv_w, conv_b, n_lat, n_ctx):
    t = px.shape[0]
    tc = _tile(d_hy, 256, LANES)
    assert col0 % tc == 0 and n_lat % CONV_ROWS == 0 and n_ctx % CONV_ROWS == 0
    per = d_hy // tc
    n_grp = HYENA_ORDER + 1
    return pl.pallas_call(
        functools.partial(_shortconv_kernel, n_lat=n_lat, n_ctx=n_ctx),
        grid=(n_grp, per),
        in_specs=[
            pl.BlockSpec((t, tc), lambda g, j: (0, col0 // tc + g * per + j)),
            pl.BlockSpec((3, tc), lambda g, j: (0, g * per + j)),
            pl.BlockSpec((1, tc), lambda g, j: (0, g * per + j)),
        ],
        out_specs=[
            pl.BlockSpec((1, n_lat, tc), lambda g, j: (g, 0, j)),
            pl.BlockSpec((1, n_ctx, tc), lambda g, j: (g, 0, j)),
        ],
        out_shape=[
            jax.ShapeDtypeStruct((n_grp, n_lat, d_hy), BF16),
            jax.ShapeDtypeStruct((n_grp, n_ctx, d_hy), BF16),
        ],
        scratch_shapes=[pltpu.VMEM((t + 2 * CONV_HALO, tc), F32)],
        compiler_params=_params(("parallel", "parallel")),
        name="shortconv",
    )(px, conv_w, conv_b.reshape(1, -1))


DFT_N2 = 256
K1_ALIGN = 8
STAGE2_K1 = 8
T2_BLOCK = 16
STAGE2_UNROLL = 4


def _dft_constants(length):
    n = 2 * length
    n1 = n // DFT_N2
    assert n1 * DFT_N2 == n and n1 % 2 == 0
    n_t1 = n1 // 2
    k1_used = n1 // 2 + 1
    k1p = -(-k1_used // K1_ALIGN) * K1_ALIGN
    k1 = np.arange(k1p, dtype=np.float64)
    live = (k1 < k1_used).astype(np.float64)
    t1 = np.arange(n1, dtype=np.float64)
    ang1 = -2.0 * np.pi * k1[:, None] * t1[None, :] / n1
    f1 = np.stack([np.cos(ang1), np.sin(ang1)]) * live[None, :, None]
    k2 = np.arange(DFT_N2, dtype=np.float64)
    t2 = np.arange(DFT_N2, dtype=np.float64)
    ang2 = -2.0 * np.pi * t2[None, None, :] * (k1[:, None, None] + n1 * k2[None, :, None]) / n
    gr = np.cos(ang2) * live[:, None, None]
    gi = np.sin(ang2) * live[:, None, None]
    g = np.stack([gr, gi, gr.transpose(0, 2, 1), gi.transpose(0, 2, 1)])
    weight = np.where((k1 == 0) | (k1 == n1 // 2), 1.0, 2.0) * live
    angc = 2.0 * np.pi * np.arange(n_t1, dtype=np.float64)[:, None] * k1[None, :] / n1
    cinv = np.stack([np.cos(angc), np.sin(angc)]) * weight[None, None, :] / n
    as_bf16 = lambda a: jnp.asarray(a.astype(np.float32)).astype(BF16)
    eye = np.eye(T2_BLOCK)
    kron = lambda m: np.stack([np.kron(m[0], eye), np.kron(m[1], eye)])
    return dict(n1=n1, n_t1=n_t1, k1p=k1p, f1=as_bf16(kron(f1)), f1_sig=as_bf16(kron(f1[:, :, :n_t1])),
                g=as_bf16(g), cinv=as_bf16(kron(cinv)))


def _dft_stage1_kernel(f_ref, u_ref, a_ref):
    k, r, tc = u_ref.shape
    u = u_ref[...].reshape(k * r, tc)
    for part in range(2):
        a = jnp.dot(f_ref[part], u, preferred_element_type=F32).astype(a_ref.dtype)
        a_ref[part] = a.reshape(a_ref.shape[1:])


def _dft_stage1(f, u4, group):
    _, k, n2, c = u4.shape
    k1p = f.shape[1] // T2_BLOCK
    tc = _tile(c, 256, LANES)
    return pl.pallas_call(
        _dft_stage1_kernel,
        grid=(n2 // T2_BLOCK, c // tc),
        in_specs=[
            pl.BlockSpec((2, k1p * T2_BLOCK, k * T2_BLOCK), lambda i, j: (0, 0, 0)),
            pl.BlockSpec((None, k, T2_BLOCK, tc), lambda i, j: (group, 0, i, j)),
        ],
        out_specs=pl.BlockSpec((2, k1p, T2_BLOCK, tc), lambda i, j: (0, 0, i, j)),
        out_shape=jax.ShapeDtypeStruct((2, k1p, n2, c), BF16),
        compiler_params=_params(("parallel", "parallel")),
        name="dft_stage1",
    )(f, u4)


def _dft_stage2_filter_kernel(a_ref, g_ref, scale_ref, kf_ref, *, k1c):
    scale = scale_ref[...]

    def body(i, carry):
        ar, ai = a_ref[0, i], a_ref[1, i]
        gr, gi = g_ref[0, i], g_ref[1, i]
        dot = functools.partial(jnp.dot, preferred_element_type=F32)
        kf_ref[0, i] = ((dot(gr, ar) - dot(gi, ai)) * scale).astype(kf_ref.dtype)
        kf_ref[1, i] = ((dot(gr, ai) + dot(gi, ar)) * scale).astype(kf_ref.dtype)
        return carry

    lax.fori_loop(0, k1c, body, 0, unroll=STAGE2_UNROLL)


def _dft_stage2_conv_kernel(a_ref, g_ref, kf_ref, b_ref, *, k1c):
    def body(i, carry):
        ar, ai = a_ref[0, i], a_ref[1, i]
        gr, gi = g_ref[0, i], g_ref[1, i]
        dot = functools.partial(jnp.dot, preferred_element_type=F32)
        xr = dot(gr, ar) - dot(gi, ai)
        xi = dot(gr, ai) + dot(gi, ar)
        kr, ki = kf_ref[0, i].astype(F32), kf_ref[1, i].astype(F32)
        zr =(xr * kr - xi * ki).astype(BF16)
        zi = (xr * ki + xi * kr).astype(BF16)
        grt, git = g_ref[2, i], g_ref[3, i]
        b_ref[0, i] = (dot(grt, zr) + dot(git, zi)).astype(b_ref.dtype)
        b_ref[1, i] = (dot(grt, zi) - dot(git, zr)).astype(b_ref.dtype)
        return carry

    lax.fori_loop(0, k1c, body, 0, unroll=STAGE2_UNROLL)


def _dft_stage2(a4, g, kf=None, scale=None):
    _, k1p, n2, c = a4.shape
    k1c = STAGE2_K1
    tc = _tile(c, 256, LANES)
    a_spec = pl.BlockSpec((2, k1c, n2, tc), lambda i, j: (0, i, 0, j))
    g_spec = pl.BlockSpec((4, k1c, n2, n2), lambda i, j: (0, i, 0, 0))
    if kf is None:
        s_spec = pl.BlockSpec((1, tc), lambda i, j: (0, j))
        body, args, in_specs, dtype = _dft_stage2_filter_kernel, (a4, g, scale), [a_spec, g_spec, s_spec], BF16
    else:
        body, args, in_specs, dtype = _dft_stage2_conv_kernel, (a4, g, kf), [a_spec, g_spec, a_spec], BF16
    return pl.pallas_call(
        functools.partial(body, k1c=k1c),
        grid=(k1p // k1c, c // tc),
        in_specs=in_specs,
        out_specs=a_spec,
        out_shape=jax.ShapeDtypeStruct(a4.shape, dtype),
        compiler_params=_params(("parallel", "arbitrary")),
        name="dft_stage2",
    )(*args)


def _dft_stage3_kernel(c_ref, b_ref, u_ref, m_ref, skip_ref, o_ref):
    _, k1p, r, tc = b_ref.shape
    rows = u_ref.shape[0] * r
    y = (jnp.dot(c_ref[0], b_ref[0].reshape(k1p * r, tc), preferred_element_type=F32)
         - jnp.dot(c_ref[1], b_ref[1].reshape(k1p * r, tc), preferred_element_type=F32))
    u = u_ref[...].reshape(rows, tc).astype(F32)
    out = m_ref[...].reshape(rows, tc).astype(F32) * (y + u * skip_ref[...])
    o_ref[0] = out.astype(o_ref.dtype).reshape(o_ref.shape[1:])


def _dft_stage3(cinv, b4, u4, u_group, m4, m_group, skip):
    _, k1p, n2, c = b4.shape
    n_t1 = cinv.shape[1] // T2_BLOCK
    tc = _tile(c, 256, LANES)
    sig_spec = lambda grp: pl.BlockSpec((None, n_t1, T2_BLOCK, tc), lambda i, j: (grp, 0, i, j))
    return pl.pallas_call(
        _dft_stage3_kernel,
        grid=(n2 // T2_BLOCK, c // tc),
        in_specs=[
            pl.BlockSpec((2, n_t1 * T2_BLOCK, k1p * T2_BLOCK), lambda i, j: (0, 0, 0)),
            pl.BlockSpec((2, k1p, T2_BLOCK, tc), lambda i, j: (0, 0, i, j)),
            sig_spec(u_group),
            sig_spec(m_group),
            pl.BlockSpec((1, tc), lambda i, j: (0, j)),
        ],
        out_specs=pl.BlockSpec((1, n_t1, T2_BLOCK, tc), lambda i, j: (0, 0, i, j)),
        out_shape=jax.ShapeDtypeStruct((1, n_t1, n2, c), BF16),
        compiler_params=_params(("parallel", "parallel")),
        name="dft_stage3",
    )(cinv, b4, u4, m4, skip.reshape(1, c))


def _hyena_latent(p_lat, kern, inv_norm, skip, consts):
    _, length, c = p_lat.shape
    n1, n_t1 = consts["n1"], consts["n_t1"]
    p4 = p_lat.reshape(HYENA_ORDER + 1, n_t1, DFT_N2, c)
    k4 = kern.reshape(HYENA_ORDER, n1, DFT_N2, c)
    u4, u_group = p4, 0
    for order in range(HYENA_ORDER):
        kf = _dft_stage2(_dft_stage1(consts["f1"], k4, order), consts["g"], scale=inv_norm[order:order + 1])
        b = _dft_stage2(_dft_stage1(consts["f1_sig"], u4, u_group), consts["g"], kf)
        u4 = _dft_stage3(consts["cinv"], b, u4, u_group, p4, order + 1, skip[order])
        u_group = 0
    return u4.reshape(length, c)


def _ctx_dft_constants(length):
    n = 2 * length
    k = np.arange(n, dtype=np.float64)
    ang = -2.0 * np.pi * k[:, None] * k[None, :] / n
    fwd = np.stack([np.cos(ang), np.sin(ang)])
    inv = np.stack([np.cos(ang[:length]), -np.sin(ang[:length])]) / n
    as_f32 = lambda a: jnp.asarray(a.astype(np.float32))
    return dict(fwd=as_f32(fwd), fwd_sig=as_f32(fwd[:, :, :length]), inv=as_f32(inv))


def _ctxconv_kernel(ff_ref, fs_ref, fi_ref, k_ref, scale_ref, u_ref, m_ref, skip_ref, o_ref):
    dot = functools.partial(jnp.dot, preferred_element_type=F32, precision=HIGHEST)
    k = k_ref[...] * scale_ref[...]
    u = u_ref[...].astype(F32)
    kr, ki = dot(ff_ref[0], k), dot(ff_ref[1], k)
    xr, xi = dot(fs_ref[0], u), dot(fs_ref[1], u)
    zr = xr * kr - xi * ki
    zi = xr * ki + xi * kr
    y = dot(fi_ref[0], zr) - dot(fi_ref[1], zi)
    o_ref[0] = (m_ref[...].astype(F32) * (y + u * skip_ref[...])).astype(o_ref.dtype)


def _ctxconv(consts, kern, order, inv_norm, u3, u_group, m3, m_group, skip):
    _, length, c = u3.shape
    n = 2 * length
    tc = _tile(c, 256, LANES)
    return pl.pallas_call(
        _ctxconv_kernel,
        grid=(c // tc,),
        in_specs=[
            pl.BlockSpec((2, n, n), lambda j: (0, 0, 0)),
            pl.BlockSpec((2, n, length), lambda j: (0, 0, 0)),
            pl.BlockSpec((2, length, n), lambda j: (0, 0, 0)),
            pl.BlockSpec((None, n, tc), lambda j: (order, 0, j)),
            pl.BlockSpec((1, tc), lambda j: (0, j)),
            pl.BlockSpec((None, length, tc), lambda j: (u_group, 0, j)),
            pl.BlockSpec((None, length, tc), lambda j: (m_group, 0, j)),
            pl.BlockSpec((1, tc), lambda j: (0, j)),
        ],
        out_specs=pl.BlockSpec((1, length, tc), lambda j: (0, 0, j)),
        out_shape=jax.ShapeDtypeStruct((1, length, c), BF16),
        compiler_params=_params(("parallel",)),
        name="ctxconv",
    )(consts["fwd"], consts["fwd_sig"], consts["inv"], kern, inv_norm[order:order + 1], u3, m3, skip.reshape(1, c))


def _hyena_context(p_ctx, kern, inv_norm, skip, consts):
    u3, u_group = p_ctx, 0
    for order in range(HYENA_ORDER):
        u3 = _ctxconv(consts, kern, order, inv_norm, u3, u_group, p_ctx, order + 1, skip[order])
        u_group = 0
    return u3[0]


def kernel(x, c, ctx, c_ctx, norm1_g, norm2_g, mod_a, mod_b, mod_bias, w_in, q_norm_g, k_norm_g, hy_conv_w, hy_conv_b, hy_w1, hy_b1, hy_w2, hy_b2, hy_w3, hy_b3, hy_w4, hy_freq, hy_skip, gm_ln_g, gm_ln_b, gm_ws, gm_bs, gate_up, gate_b, w_br, w_out, rt_group_w, rt_group_b, rt_expert_w, rt_expert_b, ex_w_gate, ex_w_up, ex_w_down, final_norm_g):
    bsz, seq, d = x.shape
    assert bsz == 1 and c.shape[0] == 1
    n_ctx = ctx.shape[1]
    depth = w_in.shape[0]
    t_all = seq + n_ctx
    d_hy = hy_skip.shape[2]
    d_gm = gm_ln_g.shape[1]
    d_at = w_br.shape[1] - d_hy - d_gm
    gate_rank = gate_up.shape[2]
    n_q = d_at // HEAD_DIM
    n_kv = n_q // KV_REP
    q_end = d_at
    v_start = q_end + n_kv * HEAD_DIM
    v_end = v_start + n_kv * HEAD_DIM
    hy_end = v_end + (HYENA_ORDER + 1) * d_hy
    gm_end = hy_end + 2 * d_gm
    n_groups = rt_group_w.shape[2]
    per_group = rt_expert_w.shape[3]
    n_experts = n_groups * per_group
    assert n_groups + n_experts <= ROUTER_COLS
    assert hy_end % (2 * d_gm) == 0 and gm_end % gate_rank == 0 and seq % ROW_CHUNK == 0
    tme = 256

    xs = jnp.concatenate([x[0], ctx[0]], axis=0)
    cond = jnp.stack([c[0], c_ctx], axis=0)
    cos, sin = _rope_tables(seq, n_ctx)

    tq = _tile(seq, 512, 16)
    tk = _tile(t_all, 2816, 256)
    tq_c = _tile(n_ctx, 256, 16)
    dft_lat = _dft_constants(seq)
    dft_ctx = _ctx_dft_constants(n_ctx)
    w_in_b, w_br_b, gate_up_b, w_out_b = (w.astype(BF16) for w in (w_in, w_br, gate_up, w_out))

    for l in range(depth):
        m = (jnp.dot(jnp.dot(jax.nn.silu(cond), mod_a[l], precision=HIGHEST), mod_b[l], precision=HIGHEST)
             + mod_bias[l]).reshape(2, N_MOD, d)

        px = _modproj(xs, norm1_g[l], m[:, 0], m[:, 1], w_in_b, l, seq)
        q, k = _qkprep(px, cos, sin, q_norm_g[l], k_norm_g[l], n_q, n_kv)
        o_at = _attention(q, k, px, v_start // HEAD_DIM, 0, seq, 0, t_all, tq, tk)
        o_at = _attention(q, k, px, v_start // HEAD_DIM, seq, n_ctx, seq, n_ctx, tq_c, n_ctx, out=o_at)

        hy_w = (hy_w1[l], hy_b1[l], hy_w2[l], hy_b2[l], hy_w3[l], hy_b3[l], hy_w4[l], hy_freq[l])
        p_lat, p_ctx = _shortconv(px, v_end, d_hy, hy_conv_w[l], hy_conv_b[l], seq, n_ctx)
        kern_x, norm_x = _hyena_filters(seq, *hy_w, d_hy, BF16)
        kern_c, norm_c = _hyena_filters(n_ctx, *hy_w, d_hy, F32)
        o_hy_x = _hyena_latent(p_lat, kern_x, 1.0 / norm_x, hy_skip[l], dft_lat)
        o_hy_c = _hyena_context(p_ctx, kern_c, 1.0 / norm_c, hy_skip[l], dft_ctx)
        o_hy = jnp.concatenate([o_hy_x, o_hy_c], axis=0)

        o_gm = _gmlp(px, hy_end // (2 * d_gm), d_gm, gm_ln_g[l], gm_ln_b[l], gm_ws[l], gm_bs[l])

        merged = _merge(o_hy, o_at, o_gm, px, gm_end // gate_rank, gate_rank, w_br_b, gate_up_b, gate_b, l)
        xs = _outproj(merged, w_out_b, l, xs, m[:, 2], seq)

        rw = jnp.concatenate([rt_group_w[l]] + [rt_expert_w[l, g] for g in range(n_groups)], axis=1)
        rw = jnp.pad(rw, ((0, 0), (0, ROUTER_COLS - rw.shape[1])))
        rb = jnp.concatenate([rt_group_b[l], rt_expert_b[l].reshape(-1)])
        rb = jnp.pad(rb, (0, ROUTER_COLS - rb.shape[0])).reshape(1, ROUTER_COLS)
        h2, logits = _router(xs, norm2_g[l], m[:, 3], m[:, 4], _split_bf16(rw, 3), rb, seq)
        eid, wts = _route(logits, n_groups, per_group)
        src_tok, roww, pos, tile_expert, tile_block, tile_first, n_used = _dispatch(eid, wts, n_experts, tme)
        take_rows = lambda a, idx: a.at[idx].get(mode="promise_in_bounds")
        xg = take_rows(h2, src_tok)
        yo = _experts(xg, roww, tile_expert, tile_block, tile_first, n_used,
                      ex_w_gate, ex_w_up, ex_w_down, l, tme)
        xs = _combine(xs, take_rows(yo, pos[:, 0]), take_rows(yo, pos[:, 1]), m[:, 5], seq)

    return _final_norm(xs, final_norm_g, seq).reshape(1, seq, d)
```

```python
import functools
import math

import numpy as np
import jax
import jax.numpy as jnp
from jax import lax
from jax.experimental import pallas as pl
from jax.experimental.pallas import tpu as pltpu

F32 = jnp.float32
BF16 = jnp.bfloat16
HIGHEST = lax.Precision.HIGHEST

HEAD_DIM = 128
KV_REP = 4
GRID_W = 64
ROPE_THETA = 10000.0
ATTN_SCALE = HEAD_DIM ** -0.5
LOG2_E = math.log2(math.e)
HYENA_ORDER = 2
HYENA_SHIFT = 0.05
HYENA_FAST_PCT = 0.3
HYENA_SLOW_PCT = 1.5
HYENA_TARGET = 1e-2
GMLP_CHUNK = 128
GMLP_GROUP_W = 128
N_BRANCH = 3
N_MOD = 6
TOP_K = 2
EPS = 1e-6
LN_EPS = 1e-5

LANES = 128
ROUTER_COLS = LANES
VMEM_LIMIT_MB = 56


def _params(semantics, vmem_mb=VMEM_LIMIT_MB):
    return pltpu.CompilerParams(dimension_semantics=semantics, vmem_limit_bytes=vmem_mb * 1024 * 1024)


def _tile(n, target, mult):
    best = None
    for d in range(mult, min(n, target) + 1, mult):
        if n % d == 0:
            best = d
    assert best is not None, (n, target, mult)
    return best


ROW_CHUNK = 16


def _modulate_rows(x_ref, g_ref, sh_ref, sc_ref, h_ref, row0, n_lat, n_rows):
    n_chunks = n_rows // ROW_CHUNK
    lat_chunks = jnp.clip((n_lat - row0) // ROW_CHUNK, 0, n_chunks)
    unroll = next(u for u in (4, 3, 2, 1) if n_chunks % u == 0)

    def segment(seg, lo, hi, **loop_kw):
        gain = g_ref[...] * (1.0 + sc_ref[seg:seg + 1, :])
        shift = sh_ref[seg:seg + 1, :]

        def body(r, carry):
            rows = pl.ds(pl.multiple_of(r * ROW_CHUNK, ROW_CHUNK), ROW_CHUNK)
            xf = x_ref[rows, :]
            ms = jnp.mean(xf * xf, axis=-1, keepdims=True)
            h_ref[rows, :] = (xf * lax.rsqrt(ms + EPS) * gain + shift).astype(h_ref.dtype)
            return carry

        lax.fori_loop(lo, hi, body, 0, **loop_kw)

    @pl.when(lat_chunks == n_chunks)
    def _():
        segment(0, 0, n_chunks, unroll=unroll)

    @pl.when(lat_chunks != n_chunks)
    def _():
        segment(0, 0, lat_chunks)
        segment(1, lat_chunks, n_chunks)


def _modproj_kernel(x_ref, g_ref, sh_ref, sc_ref, w_ref, o_ref, h_ref, *, n_lat, tm):
    i = pl.program_id(0)

    @pl.when(pl.program_id(1) == 0)
    def _():
        _modulate_rows(x_ref, g_ref, sh_ref, sc_ref, h_ref, i * tm, n_lat, tm)

    o_ref[...] = jnp.dot(h_ref[...], w_ref[...], preferred_element_type=F32).astype(o_ref.dtype)


def _modproj(x, g, sh, sc, w_all, layer, n_lat):
    t, d = x.shape
    n = w_all.shape[2]
    tm = _tile(t, 528, 16)
    tn = _tile(n, 512, LANES)
    return pl.pallas_call(
        functools.partial(_modproj_kernel, n_lat=n_lat, tm=tm),
        grid=(t // tm, n // tn),
        in_specs=[
            pl.BlockSpec((tm, d), lambda i, j: (i, 0)),
            pl.BlockSpec((1, d), lambda i, j: (0, 0)),
            pl.BlockSpec((2, d), lambda i, j: (0, 0)),
            pl.BlockSpec((2, d), lambda i, j: (0, 0)),
            pl.BlockSpec((None, d, tn), lambda i, j: (layer, 0, j)),
        ],
        out_specs=pl.BlockSpec((tm, tn), lambda i, j: (i, j)),
        out_shape=jax.ShapeDtypeStruct((t, n), BF16),
        scratch_shapes=[pltpu.VMEM((tm, d), BF16)],
        compiler_params=_params(("parallel", "arbitrary")),
        name="modproj",
    )(x, g.reshape(1, d), sh, sc, w_all)


def _qkprep_kernel(p_ref, cos_ref, sin_ref, qg_ref, kg_ref, q_ref, k_ref, *, n_q, n_kv):
    cos = cos_ref[...]
    sin = sin_ref[...]
    lane = lax.broadcasted_iota(jnp.int32, (1, HEAD_DIM), 1)
    first_half = (lane % (HEAD_DIM // 2)) < (HEAD_DIM // 4)

    def head(h, gain, scale):
        xh = p_ref[:, h * HEAD_DIM:(h + 1) * HEAD_DIM].astype(F32)
        ms = jnp.mean(xh * xh, axis=-1, keepdims=True)
        y = xh * lax.rsqrt(ms + EPS) * gain
        partner = jnp.where(first_half,
                            pltpu.roll(y, HEAD_DIM - HEAD_DIM // 4, 1),
                            pltpu.roll(y, HEAD_DIM // 4, 1))
        return (y * cos + partner * sin) * scale

    for h in range(n_q):
        q_ref[h] = head(h, qg_ref[...], ATTN_SCALE * LOG2_E).astype(q_ref.dtype)
    for h in range(n_kv):
        k_ref[h] = head(n_q + h, kg_ref[...], 1.0).astype(k_ref.dtype)


def _qkprep(px, cos, sin, qg, kg, n_q, n_kv):
    t = px.shape[0]
    tr = _tile(t, 264, 8)
    width = (n_q + n_kv) * HEAD_DIM
    return pl.pallas_call(
        functools.partial(_qkprep_kernel, n_q=n_q, n_kv=n_kv),
        grid=(t // tr,),
        in_specs=[
            pl.BlockSpec((tr, width), lambda i: (i, 0)),
            pl.BlockSpec((tr, HEAD_DIM), lambda i: (i, 0)),
            pl.BlockSpec((tr, HEAD_DIM), lambda i: (i, 0)),
            pl.BlockSpec((1, HEAD_DIM), lambda i: (0, 0)),
            pl.BlockSpec((1, HEAD_DIM), lambda i: (0, 0)),
        ],
        out_specs=[
            pl.BlockSpec((n_q, tr, HEAD_DIM), lambda i: (0, i, 0)),
            pl.BlockSpec((n_kv, tr, HEAD_DIM), lambda i: (0, i, 0)),
        ],
        out_shape=[
            jax.ShapeDtypeStruct((n_q, t, HEAD_DIM), BF16),
            jax.ShapeDtypeStruct((n_kv, t, HEAD_DIM), BF16),
        ],
        compiler_params=_params(("parallel",)),
        name="qkprep",
    )(px, cos, sin, qg.reshape(1, HEAD_DIM), kg.reshape(1, HEAD_DIM))


def _rope_tables(n_lat, n_ctx):
    n = HEAD_DIM // 4
    inv = ROPE_THETA ** (-np.arange(n, dtype=np.float32) / n)
    t = np.arange(n_lat)
    rows = (t // GRID_W).astype(np.float32)
    cols = (t % GRID_W).astype(np.float32)
    ang = np.concatenate([rows[:, None] * inv[None, :]] * 2 + [cols[:, None] * inv[None, :]] * 2, axis=1)
    ang = jnp.asarray(ang.astype(np.float32))
    sign = np.tile(np.concatenate([-np.ones(n, np.float32), np.ones(n, np.float32)]), 2)
    cos = jnp.concatenate([jnp.cos(ang), jnp.ones((n_ctx, HEAD_DIM), F32)], axis=0)
    sin = jnp.concatenate([jnp.sin(ang) * sign[None, :], jnp.zeros((n_ctx, HEAD_DIM), F32)], axis=0)
    return cos, sin


def _attn_kernel(*refs, tq, tk, n_kv, aliased):
    if aliased:
        q_ref, qn_ref, k_ref, v_ref, _, o_ref, s_ref, m_ref, l_ref, acc_ref = refs
    else:
        q_ref, qn_ref, k_ref, v_ref, o_ref, s_ref, m_ref, l_ref, acc_ref = refs
    n_chunks = n_kv // tk
    assert (n_chunks * KV_REP) % 2 == 0
    m_ref[...] = jnp.full(m_ref.shape, -jnp.inf, F32)
    l_ref[...] = jnp.zeros(l_ref.shape, F32)
    acc_ref[...] = jnp.zeros(acc_ref.shape, F32)

    def scores(q, ks):
        return lax.dot_general(q, k_ref[0, ks, :], (((1,), (1,)), ((), ())), preferred_element_type=F32)

    @pl.when(pl.program_id(1) == 0)
    def _():
        s_ref[0] = scores(q_ref[0], pl.ds(0, tk))

    def body(c, carry):
        ks = pl.ds(pl.multiple_of(c * tk, tk), tk)
        last = c + 1 == n_chunks
        kn = pl.ds(pl.multiple_of(jnp.where(last, 0, c + 1) * tk, tk), tk)
        v = v_ref[ks, :]
        for r in range(KV_REP):
            cur, nxt = r % 2, (r + 1) % 2
            if r + 1 < KV_REP:
                s_ref[nxt] = scores(q_ref[r + 1], ks)
            else:
                s_ref[nxt] = scores(jnp.where(last, qn_ref[0], q_ref[0]), kn)
            s = s_ref[cur]
            m_prev = m_ref[r]
            m_new = jnp.maximum(m_prev, jnp.max(s, axis=-1, keepdims=True))
            alpha = jnp.exp2(m_prev - m_new)
            p = jnp.exp2(s - m_new)
            l_ref[r] = alpha * l_ref[r] + jnp.sum(p, axis=-1, keepdims=True)
            acc_ref[r] = alpha * acc_ref[r] + jnp.dot(p.astype(BF16), v, preferred_element_type=F32)
            m_ref[r] = m_new
        return carry

    lax.fori_loop(0, n_chunks, body, 0)
    for r in range(KV_REP):
        o_ref[:, r * HEAD_DIM:(r + 1) * HEAD_DIM] = (acc_ref[r] / l_ref[r]).astype(o_ref.dtype)


def _attention(q, k, px, v_col_block, q_row0, n_q_rows, kv_row0, n_kv, tq, tk, out=None):
    n_heads, t, _ = q.shape
    n_kv_heads = k.shape[0]
    assert q_row0 % tq == 0 and n_q_rows % tq == 0 and kv_row0 % n_kv == 0 and n_kv % tk == 0
    qb0 = q_row0 // tq
    kb0 = kv_row0 // n_kv
    aliased = out is not None
    n_tiles = n_q_rows // tq
    in_specs = [
        pl.BlockSpec((KV_REP, tq, HEAD_DIM), lambda g, i: (g, qb0 + i, 0)),
        pl.BlockSpec((1, tq, HEAD_DIM), lambda g, i: (g * KV_REP, qb0 + jnp.minimum(i + 1, n_tiles - 1), 0)),
        pl.BlockSpec((1, n_kv, HEAD_DIM), lambda g, i: (g, kb0, 0)),
        pl.BlockSpec((n_kv, HEAD_DIM), lambda g, i: (kb0, v_col_block + g)),
    ]
    args = [q, q, k, px]
    if aliased:
        in_specs.append(pl.BlockSpec(memory_space=pl.ANY))
        args.append(out)
    return pl.pallas_call(
        functools.partial(_attn_kernel, tq=tq, tk=tk, n_kv=n_kv, aliased=aliased),
        grid=(n_kv_heads, n_q_rows // tq),
        in_specs=in_specs,
        out_specs=pl.BlockSpec((tq, KV_REP * HEAD_DIM), lambda g, i: (qb0 + i, g)),
        out_shape=jax.ShapeDtypeStruct((t, n_heads * HEAD_DIM), BF16),
        scratch_shapes=[
            pltpu.VMEM((2, tq, tk), F32),
            pltpu.VMEM((KV_REP, tq, 1), F32),
            pltpu.VMEM((KV_REP, tq, 1), F32),
            pltpu.VMEM((KV_REP, tq, HEAD_DIM), F32),
        ],
        input_output_aliases={4: 0} if aliased else {},
        compiler_params=_params(("parallel", "arbitrary")),
        name="attention",
    )(*args)


def _gmlp_kernel(p_ref, lng_ref, lnb_ref, ws_ref, bst_ref, o_ref, *, n_chunks, d_g):
    n_groups = d_g // GMLP_GROUP_W
    for b in range(n_chunks):
        rows = slice(b * GMLP_CHUNK, (b + 1) * GMLP_CHUNK)
        uv = jax.nn.gelu(p_ref[rows, :].astype(F32))
        u = uv[:, :d_g]
        v = uv[:, d_g:]
        vc = v - jnp.mean(v, axis=-1, keepdims=True)
        var = jnp.mean(vc * vc, axis=-1, keepdims=True)
        vn = (vc * lax.rsqrt(var + LN_EPS) * lng_ref[...] + lnb_ref[...]).astype(BF16)
        for g in range(n_groups):
            cols = slice(g * GMLP_GROUP_W, (g + 1) * GMLP_GROUP_W)
            vm = jnp.dot(ws_ref[g].astype(BF16), vn[:, cols], preferred_element_type=F32) + bst_ref[:, g:g + 1]
            o_ref[rows, cols] = (u[:, cols] * vm).astype(o_ref.dtype)


def _gmlp(px, col_block, d_g, ln_g, ln_b, ws, bs):
    t = px.shape[0]
    n_chunks = 2
    tr = n_chunks * GMLP_CHUNK
    assert t % tr == 0
    n_groups = d_g // GMLP_GROUP_W
    return pl.pallas_call(
        functools.partial(_gmlp_kernel, n_chunks=n_chunks, d_g=d_g),
        grid=(t // tr,),
        in_specs=[
            pl.BlockSpec((tr, 2 * d_g), lambda i: (i, col_block)),
            pl.BlockSpec((1, d_g), lambda i: (0, 0)),
            pl.BlockSpec((1, d_g), lambda i: (0, 0)),
            pl.BlockSpec((n_groups, GMLP_CHUNK, GMLP_CHUNK), lambda i: (0, 0, 0)),
            pl.BlockSpec((GMLP_CHUNK, n_groups), lambda i: (0, 0)),
        ],
        out_specs=pl.BlockSpec((tr, d_g), lambda i: (i, 0)),
        out_shape=jax.ShapeDtypeStruct((t, d_g), BF16),
        compiler_params=_params(("parallel",)),
        name="gmlp",
    )(px, ln_g.reshape(1, d_g), ln_b.reshape(1, d_g), ws, bs.T)


def _merge_kernel(hy_ref, at_ref, gm_ref, gc_ref, wbr_ref, gup_ref, gb_ref, o_ref, *, d_hy, d_at):
    gc = gc_ref[...]
    bounds = (0, d_hy, d_hy + d_at, wbr_ref.shape[0])
    merged = None
    for b, a_ref in enumerate((hy_ref, at_ref, gm_ref)):
        y = jnp.dot(a_ref[...], wbr_ref[bounds[b]:bounds[b + 1], :], preferred_element_type=F32)
        gate = jax.nn.sigmoid(jnp.dot(gc, gup_ref[b], preferred_element_type=F32) + gb_ref[b:b + 1, :])
        merged = gate * y if merged is None else merged + gate * y
    o_ref[...] = merged.astype(o_ref.dtype)


def _merge(o_hy, o_at, o_gm, px, gc_col_block, gate_rank, w_br, gate_up, gate_b, layer):
    t = px.shape[0]
    d_hy, d_at, d_gm = o_hy.shape[1], o_at.shape[1], o_gm.shape[1]
    d = w_br.shape[2]
    tm = _tile(t, 528, 16)
    tn = _tile(d, 1024, LANES)
    return pl.pallas_call(
        functools.partial(_merge_kernel, d_hy=d_hy, d_at=d_at),
        grid=(t // tm, d // tn),
        in_specs=[
            pl.BlockSpec((tm, d_hy), lambda i, j: (i, 0)),
            pl.BlockSpec((tm, d_at), lambda i, j: (i, 0)),
            pl.BlockSpec((tm, d_gm), lambda i, j: (i, 0)),
            pl.BlockSpec((tm, gate_rank), lambda i, j: (i, gc_col_block)),
            pl.BlockSpec((None, d_hy + d_at + d_gm, tn), lambda i, j: (layer, 0, j)),
            pl.BlockSpec((None, N_BRANCH, gate_rank, tn), lambda i, j: (layer, 0, 0, j)),
            pl.BlockSpec((None, N_BRANCH, tn), lambda i, j: (layer, 0, j)),
        ],
        out_specs=pl.BlockSpec((tm, tn), lambda i, j: (i, j)),
        out_shape=jax.ShapeDtypeStruct((t, d), BF16),
        compiler_params=_params(("parallel", "arbitrary")),
        name="merge",
    )(o_hy, o_at, o_gm, px, w_br, gate_up, gate_b)


def _outproj_kernel(a_ref, w_ref, x_ref, gate_ref, o_ref, *, n_lat, tm):
    y = jnp.dot(a_ref[...], w_ref[...], preferred_element_type=F32)
    rid = pl.program_id(0) * tm + lax.broadcasted_iota(jnp.int32, (tm, 1), 0)
    gate = jnp.where(rid >= n_lat, gate_ref[1:2, :], gate_ref[0:1, :])
    o_ref[...] = x_ref[...] + gate * y


def _outproj(a, w_all, layer, x, gate, n_lat):
    t, k = a.shape
    d = w_all.shape[2]
    tm = _tile(t, 528, 16)
    tn = _tile(d, 1024, LANES)
    return pl.pallas_call(
        functools.partial(_outproj_kernel, n_lat=n_lat, tm=tm),
        grid=(t // tm, d // tn),
        in_specs=[
            pl.BlockSpec((tm, k), lambda i, j: (i, 0)),
            pl.BlockSpec((None, k, tn), lambda i, j: (layer, 0, j)),
            pl.BlockSpec((tm, tn), lambda i, j: (i, j)),
            pl.BlockSpec((2, tn), lambda i, j: (0, j)),
        ],
        out_specs=pl.BlockSpec((tm, tn), lambda i, j: (i, j)),
        out_shape=jax.ShapeDtypeStruct((t, d), F32),
        compiler_params=_params(("parallel", "arbitrary")),
        name="outproj",
    )(a, w_all, x, gate)


def _split_bf16(w, parts):
    out = []
    rest = w.astype(F32)
    for _ in range(parts):
        piece = rest.astype(BF16)
        out.append(piece)
        rest = rest - piece.astype(F32)
    return jnp.stack(out)


def _router_kernel(x_ref, g_ref, sh_ref, sc_ref, rw_ref, rb_ref, h_ref, lg_ref, *, n_lat, tm):
    _modulate_rows(x_ref, g_ref, sh_ref, sc_ref, h_ref, pl.program_id(0) * tm, n_lat, tm)
    h = h_ref[...]
    logits = rb_ref[...]
    for part in range(rw_ref.shape[0]):
        logits = logits + jnp.dot(h, rw_ref[part], preferred_element_type=F32)
    lg_ref[...] = logits


def _router(x, g, sh, sc, rw, rb, n_lat):
    t, d = x.shape
    tm = _tile(t, 264, 16)
    return pl.pallas_call(
        functools.partial(_router_kernel, n_lat=n_lat, tm=tm),
        grid=(t // tm,),
        in_specs=[
            pl.BlockSpec((tm, d), lambda i: (i, 0)),
            pl.BlockSpec((1, d), lambda i: (0, 0)),
            pl.BlockSpec((2, d), lambda i: (0, 0)),
            pl.BlockSpec((2, d), lambda i: (0, 0)),
            pl.BlockSpec((rw.shape[0], d, ROUTER_COLS), lambda i: (0, 0, 0)),
            pl.BlockSpec((1, ROUTER_COLS), lambda i: (0, 0)),
        ],
        out_specs=[
            pl.BlockSpec((tm, d), lambda i: (i, 0)),
            pl.BlockSpec((tm, ROUTER_COLS), lambda i: (i, 0)),
        ],
        out_shape=[
            jax.ShapeDtypeStruct((t, d), BF16),
            jax.ShapeDtypeStruct((t, ROUTER_COLS), F32),
        ],
        compiler_params=_params(("parallel",)),
        name="router",
    )(x, g.reshape(1, d), sh, sc, rw, rb)


def _experts_kernel(te_ref, tb_ref, tf_ref, ts_ref, ne_ref, nu_ref, x_ref, rw_ref, wg_hbm, wu_hbm, wd_hbm, o_ref,
                    wg_f, wu_f, wd_f, wg_s, wu_s, wd_s, sem, *, layer):
    i = pl.program_id(0)

    def weight_copies(expert, slot):
        return [pltpu.make_async_copy(hbm.at[layer, expert], stage.at[slot], sem.at[slot, n])
                for n, (hbm, stage) in enumerate(((wg_hbm, wg_f), (wu_hbm, wu_f), (wd_hbm, wd_f)))]

    @pl.when(i < nu_ref[0])
    def _():
        slot = ts_ref[i]

        @pl.when(tf_ref[i] == 1)
        def _():
            @pl.when(i == 0)
            def _():
                for cp in weight_copies(te_ref[0], 0):
                    cp.start()

            for cp in weight_copies(te_ref[i], slot):
                cp.wait()
            wg_s[...] = wg_f[slot].astype(BF16)
            wu_s[...] = wu_f[slot].astype(BF16)
            wd_s[...] = wd_f[slot].astype(BF16)

            @pl.when(ne_ref[i] >= 0)
            def _():
                for cp in weight_copies(ne_ref[i], 1 - slot):
                    cp.start()

        xg = x_ref[...]
        hg = jnp.dot(xg, wg_s[...], preferred_element_type=F32)
        hu = jnp.dot(xg, wu_s[...], preferred_element_type=F32)
        hid = (jax.nn.silu(hg) * hu * rw_ref[...]).astype(BF16)
        o_ref[...] = jnp.dot(hid, wd_s[...], preferred_element_type=F32).astype(o_ref.dtype)


def _experts(xg, roww, tiles, w_gate, w_up, w_down, layer, tme):
    p, d = xg.shape
    f = w_gate.shape[3]
    n_tiles = p // tme
    row_block = lambda i, te, tb, tf, ts, ne, nu: (tb[i], 0)
    grid_spec = pltpu.PrefetchScalarGridSpec(
        num_scalar_prefetch=6,
        grid=(n_tiles,),
        in_specs=[
            pl.BlockSpec((tme, d), row_block),
            pl.BlockSpec((tme, 1), row_block),
            pl.BlockSpec(memory_space=pl.ANY),
            pl.BlockSpec(memory_space=pl.ANY),
            pl.BlockSpec(memory_space=pl.ANY),
        ],
        out_specs=pl.BlockSpec((tme, d), row_block),
        scratch_shapes=[
            pltpu.VMEM((2, d, f), F32),
            pltpu.VMEM((2, d, f), F32),
            pltpu.VMEM((2, f, d), F32),
            pltpu.VMEM((d, f), BF16),
            pltpu.VMEM((d, f), BF16),
            pltpu.VMEM((f, d), BF16),
            pltpu.SemaphoreType.DMA((2, 3)),
        ],
    )
    return pl.pallas_call(
        functools.partial(_experts_kernel, layer=layer),
        grid_spec=grid_spec,
        out_shape=jax.ShapeDtypeStruct((p, d), BF16),
        compiler_params=_params(("arbitrary",)),
        name="experts",
    )(*tiles, xg, roww, w_gate, w_up, w_down)


def _combine_kernel(x_ref, y0_ref, y1_ref, gate_ref, o_ref, *, n_lat, tm):
    rid = pl.program_id(0) * tm + lax.broadcasted_iota(jnp.int32, (tm, 1), 0)
    gate = jnp.where(rid >= n_lat, gate_ref[1:2, :], gate_ref[0:1, :])
    o_ref[...] = x_ref[...] + gate * (y0_ref[...].astype(F32) + y1_ref[...].astype(F32))


def _combine(x, y0, y1, gate, n_lat):
    t, d = x.shape
    tm = _tile(t, 264, 16)
    row = lambda: pl.BlockSpec((tm, d), lambda i: (i, 0))
    return pl.pallas_call(
        functools.partial(_combine_kernel, n_lat=n_lat, tm=tm),
        grid=(t // tm,),
        in_specs=[row(), row(), row(), pl.BlockSpec((2, d), lambda i: (0, 0))],
        out_specs=row(),
        out_shape=jax.ShapeDtypeStruct((t, d), F32),
        compiler_params=_params(("parallel",)),
        name="moe_combine",
    )(x, y0, y1, gate)


def _route(logits, n_groups, per_group):
    lg = logits[:, :n_groups]
    le_all = logits[:, n_groups:n_groups + n_groups * per_group].reshape(-1, n_groups, per_group)
    grp = jnp.argmax(lg, axis=-1)
    p_grp = jnp.take_along_axis(jax.nn.softmax(lg, axis=-1), grp[:, None], axis=-1)
    le = jnp.take_along_axis(le_all, grp[:, None, None], axis=1)[:, 0]
    top_p, top_i = lax.top_k(jax.nn.softmax(le, axis=-1), TOP_K)
    w = top_p / jnp.sum(top_p, axis=-1, keepdims=True) * p_grp
    eid = grp[:, None] * per_group + top_i
    return eid.astype(jnp.int32), w


def _dispatch(eid, w, n_experts, tme):
    t = eid.shape[0]
    n_pairs = t * TOP_K
    p_rows = n_pairs + n_experts * tme
    p_rows = -(-p_rows // tme) * tme
    flat_e = eid.reshape(-1)
    onehot = (flat_e[:, None] == jnp.arange(n_experts, dtype=jnp.int32)[None, :]).astype(jnp.int32)
    csum = jnp.cumsum(onehot, axis=0)
    rank = jnp.take_along_axis(csum, flat_e[:, None], axis=1)[:, 0] - 1
    counts = csum[-1]
    padded = -(-counts // tme) * tme
    ends = jnp.cumsum(padded)
    starts = ends - padded
    dest = starts[flat_e] + rank
    assert t < 2 ** 24
    tok = (jnp.arange(n_pairs, dtype=jnp.int32) // TOP_K).astype(F32)
    slots = jnp.full((p_rows, 2), -1.0, F32).at[dest].set(jnp.stack([tok, w.reshape(-1)], axis=1))
    is_pad = slots[:, 0] < 0
    src_tok = jnp.where(is_pad, jnp.arange(p_rows, dtype=jnp.int32) % t, slots[:, 0].astype(jnp.int32))
    roww = jnp.where(is_pad, 0.0, slots[:, 1])
    n_used = (ends[-1] // tme).astype(jnp.int32)
    tile_ids = jnp.arange(p_rows // tme, dtype=jnp.int32)
    tile_block = jnp.minimum(tile_ids, n_used - 1)
    tile_expert = jnp.minimum(
        jnp.searchsorted(ends, tile_block * tme, side="right").astype(jnp.int32), n_experts - 1)
    tile_first = jnp.logical_and(tile_block * tme == starts[tile_expert], tile_ids < n_used).astype(jnp.int32)
    ordinal = jnp.maximum(jnp.cumsum(tile_first) - 1, 0)
    used = counts > 0
    used_experts = jnp.argsort(jnp.logical_not(used), stable=True).astype(jnp.int32)
    next_expert = jnp.where(ordinal + 1 < jnp.sum(used),
                            used_experts[jnp.minimum(ordinal + 1, n_experts - 1)], -1).astype(jnp.int32)
    tiles = (tile_expert, tile_block, tile_first, (ordinal % 2).astype(jnp.int32), next_expert, n_used.reshape(1))
    return src_tok, roww.reshape(-1, 1), dest.reshape(t, TOP_K), tiles


def _rmsnorm_kernel(x_ref, g_ref, o_ref):
    xf = x_ref[...]
    ms = jnp.mean(xf * xf, axis=-1, keepdims=True)
    o_ref[...] = xf * lax.rsqrt(ms + EPS) * g_ref[...]


def _final_norm(x, g, n_rows):
    d = x.shape[1]
    tr = _tile(n_rows, 256, 8)
    return pl.pallas_call(
        _rmsnorm_kernel,
        grid=(n_rows // tr,),
        in_specs=[pl.BlockSpec((tr, d), lambda i: (i, 0)), pl.BlockSpec((1, d), lambda i: (0, 0))],
        out_specs=pl.BlockSpec((tr, d), lambda i: (i, 0)),
        out_shape=jax.ShapeDtypeStruct((n_rows, d), F32),
        compiler_params=_params(("parallel",)),
        name="final_norm",
    )(x, g.reshape(1, d))


SUBLANES = 8


def _filter_features(length, emb):
    bands = (emb - 1) // 2
    i = np.arange(length, dtype=np.float64)
    t = i / (length - 1)
    f = np.linspace(1e-4, bands - 1, bands)
    w = 2.0 * np.pi * i / length
    z = np.concatenate([t[:, None], np.cos(-f[None, :] * w[:, None]), np.sin(-f[None, :] * w[:, None])], axis=1)
    z = np.concatenate([z, z[::-1]], axis=0)
    emb_pad = -(-emb // SUBLANES) * SUBLANES
    z = np.pad(z, ((0, 0), (0, emb_pad - emb)))
    return jnp.asarray(z.astype(np.float32))


def _filter_kernel(z_ref, w1_ref, b1_ref, w2_ref, b2_ref, w3_ref, b3_ref, w4_ref, fr_ref, dl_ref,
                   k_ref, nrm_ref, *, d_hy):
    dot = functools.partial(jnp.dot, preferred_element_type=F32, precision=HIGHEST)

    @pl.when(jnp.logical_and(pl.program_id(0) == 0, pl.program_id(1) == 0))
    def _():
        nrm_ref[...] = jnp.zeros(nrm_ref.shape, F32)

    z = z_ref[...]
    fr = fr_ref[...]
    h = jnp.sin(fr * (dot(z, w1_ref[...]) + b1_ref[...]))
    h = jnp.sin(fr * (dot(h, w2_ref[...]) + b2_ref[...]))
    h = jnp.sin(fr * (dot(h, w3_ref[...]) + b3_ref[...]))
    h_hi = h.astype(BF16)
    h_lo = (h - h_hi.astype(F32)).astype(BF16)
    hk = (jnp.dot(h_hi, w4_ref[0], preferred_element_type=F32)
          + jnp.dot(h_lo, w4_ref[0], preferred_element_type=F32)
          + jnp.dot(h_hi, w4_ref[1], preferred_element_type=F32))
    window = jnp.exp(-z[:, 0:1] * dl_ref[...]) + HYENA_SHIFT
    for o in range(HYENA_ORDER):
        k = hk[:, o * d_hy:(o + 1) * d_hy] * window
        k_ref[o] = k.astype(k_ref.dtype)
        nrm_ref[o:o + 1, :] += jnp.sum(jnp.abs(k), axis=0, keepdims=True)


def _hyena_filters(length, w1, b1, w2, b2, w3, b3, w4, freq, d_hy, out_dtype):
    emb, hidden = w1.shape
    z = _filter_features(length, emb)
    emb_pad = z.shape[1]
    w1p = jnp.pad(w1, ((0, emb_pad - emb), (0, 0)))
    max_decay = math.log(HYENA_TARGET) / HYENA_FAST_PCT
    min_decay = math.log(HYENA_TARGET) / HYENA_SLOW_PCT
    deltas = jnp.asarray(np.abs(np.linspace(min_decay, max_decay, d_hy)).astype(np.float32)).reshape(1, d_hy)
    tr = _tile(length, 512, SUBLANES)
    per = length // tr
    vec = lambda a: a.reshape(1, hidden)
    full = lambda shape: pl.BlockSpec(shape, lambda d, i: (0,) * len(shape))
    return pl.pallas_call(
        functools.partial(_filter_kernel, d_hy=d_hy),
        grid=(2, per),
        in_specs=[
            pl.BlockSpec((tr, emb_pad), lambda d, i: (d * per + i, 0)),
            full((emb_pad, hidden)), full((1, hidden)),
            full((hidden, hidden)), full((1, hidden)),
            full((hidden, hidden)), full((1, hidden)),
            pl.BlockSpec((2, hidden, HYENA_ORDER * d_hy), lambda d, i: (0, 0, d)),
            full((1, hidden)), full((1, d_hy)),
        ],
        out_specs=[
            pl.BlockSpec((HYENA_ORDER, tr, d_hy), lambda d, i: (0, d * per + i, 0)),
            pl.BlockSpec((HYENA_ORDER, d_hy), lambda d, i: (0, 0)),
        ],
        out_shape=[
            jax.ShapeDtypeStruct((HYENA_ORDER, 2 * length, d_hy), out_dtype),
            jax.ShapeDtypeStruct((HYENA_ORDER, d_hy), F32),
        ],
        compiler_params=_params(("arbitrary", "arbitrary")),
        name="hyena_filter",
    )(z, w1p, vec(b1), w2, vec(b2), w3, vec(b3), _split_bf16(w4, 2), vec(freq), deltas)


CONV_ROWS = 64
CONV_HALO = 8


def _shortconv_kernel(p_ref, w_ref, b_ref, lat_ref, ctx_ref, pad_ref, *, n_lat, n_ctx):
    t = n_lat + n_ctx
    tc = p_ref.shape[1]
    pad_ref[0:CONV_HALO, :] = jnp.zeros((CONV_HALO, tc), F32)
    pad_ref[t + CONV_HALO:t + 2 * CONV_HALO, :] = jnp.zeros((CONV_HALO, tc), F32)

    def fill(r, carry):
        src = pl.ds(pl.multiple_of(r * CONV_ROWS, CONV_ROWS), CONV_ROWS)
        dst = pl.ds(pl.multiple_of(r * CONV_ROWS + CONV_HALO, CONV_HALO), CONV_ROWS)
        pad_ref[dst, :] = p_ref[src, :].astype(F32)
        return carry

    lax.fori_loop(0, t // CONV_ROWS, fill, 0)
    w0, w1, w2, bias = w_ref[0:1, :], w_ref[1:2, :], w_ref[2:3, :], b_ref[...]
    ext_rows = CONV_ROWS + 2 * CONV_HALO
    inner = slice(CONV_HALO, CONV_HALO + CONV_ROWS)

    def segment(out_ref, row0, n_rows):
        def body(r, carry):
            ext = pad_ref[pl.ds(pl.multiple_of(row0 + r * CONV_ROWS, CONV_HALO), ext_rows), :]
            rid = r * CONV_ROWS + lax.broadcasted_iota(jnp.int32, (CONV_ROWS, 1), 0)
            prev = jnp.where(rid == 0, 0.0, pltpu.roll(ext, 1, 0)[inner])
            nxt = jnp.where(rid == n_rows - 1, 0.0, pltpu.roll(ext, ext_rows - 1, 0)[inner])
            y = prev * w0 + ext[inner] * w1 + nxt * w2 + bias
            out_ref[0, pl.ds(pl.multiple_of(r * CONV_ROWS, CONV_ROWS), CONV_ROWS), :] = y.astype(out_ref.dtype)
            return carry

        lax.fori_loop(0, n_rows // CONV_ROWS, body, 0)

    segment(lat_ref, 0, n_lat)
    segment(ctx_ref, n_lat, n_ctx)


def _shortconv(px, col0, d_hy, conv_w, conv_b, n_lat, n_ctx):
    t = px.shape[0]
    tc = _tile(d_hy, 256, LANES)
    assert col0 % tc == 0 and n_lat % CONV_ROWS == 0 and n_ctx % CONV_ROWS == 0
    per = d_hy // tc
    n_grp = HYENA_ORDER + 1
    return pl.pallas_call(
        functools.partial(_shortconv_kernel, n_lat=n_lat, n_ctx=n_ctx),
        grid=(n_grp, per),
        in_specs=[
            pl.BlockSpec((t, tc), lambda g, j: (0, col0 // tc + g * per + j)),
            pl.BlockSpec((3, tc), lambda g, j: (0, g * per + j)),
            pl.BlockSpec((1, tc), lambda g, j: (0, g * per + j)),
        ],
        out_specs=[
            pl.BlockSpec((1, n_lat, tc), lambda g, j: (g, 0, j)),
            pl.BlockSpec((1, n_ctx, tc), lambda g, j: (g, 0, j)),
        ],
        out_shape=[
            jax.ShapeDtypeStruct((n_grp, n_lat, d_hy), BF16),
            jax.ShapeDtypeStruct((n_grp, n_ctx, d_hy), BF16),
        ],
        scratch_shapes=[pltpu.VMEM((t + 2 * CONV_HALO, tc), F32)],
        compiler_params=_params(("parallel", "parallel")),
        name="shortconv",
    )(px, conv_w, conv_b.reshape(1, -1))


DFT_N2 = 256
K1_ALIGN = 8
STAGE2_K1 = 8
T2_BLOCK = 16
STAGE2_UNROLL = 4


def _dft_constants(length):
    n = 2 * length
    n1 = n // DFT_N2
    assert n1 * DFT_N2 == n and n1 % 2 == 0
    n_t1 = n1 // 2
    k1_used = n1 // 2 + 1
    k1p = -(-k1_used // K1_ALIGN) * K1_ALIGN
    k1 = np.arange(k1p, dtype=np.float64)
    live = (k1 < k1_used).astype(np.float64)
    t1 = np.arange(n1, dtype=np.float64)
    ang1 = -2.0 * np.pi * k1[:, None] * t1[None, :] / n1
    f1 = np.stack([np.cos(ang1), np.sin(ang1)]) * live[None, :, None]
    k2 = np.arange(DFT_N2, dtype=np.float64)
    t2 = np.arange(DFT_N2, dtype=np.float64)
    ang2 = -2.0 * np.pi * t2[None, None, :] * (k1[:, None, None] + n1 * k2[None, :, None]) / n
    gr = np.cos(ang2) * live[:, None, None]
    gi = np.sin(ang2) * live[:, None, None]
    g = np.stack([gr, gi, gr.transpose(0, 2, 1), gi.transpose(0, 2, 1)])
    weight = np.where((k1 == 0) | (k1 == n1 // 2), 1.0, 2.0) * live
    angc = 2.0 * np.pi * np.arange(n_t1, dtype=np.float64)[:, None] * k1[None, :] / n1
    cinv = np.stack([np.cos(angc), np.sin(angc)]) * weight[None, None, :] / n
    as_bf16 = lambda a: jnp.asarray(a.astype(np.float32)).astype(BF16)
    eye = np.eye(T2_BLOCK)
    kron = lambda m: np.stack([np.kron(m[0], eye), np.kron(m[1], eye)])
    return dict(n1=n1, n_t1=n_t1, k1p=k1p, f1=as_bf16(kron(f1)), f1_sig=as_bf16(kron(f1[:, :, :n_t1])),
                g=as_bf16(g), cinv=as_bf16(kron(cinv)))


def _dft_stage1_kernel(f_ref, u_ref, a_ref):
    k, r, tc = u_ref.shape
    u = u_ref[...].reshape(k * r, tc)
    for part in range(2):
        a = jnp.dot(f_ref[part], u, preferred_element_type=F32).astype(a_ref.dtype)
        a_ref[part] = a.reshape(a_ref.shape[1:])


def _dft_stage1(f, u4, group):
    _, k, n2, c = u4.shape
    k1p = f.shape[1] // T2_BLOCK
    tc = _tile(c, 256, LANES)
    return pl.pallas_call(
        _dft_stage1_kernel,
        grid=(n2 // T2_BLOCK, c // tc),
        in_specs=[
            pl.BlockSpec((2, k1p * T2_BLOCK, k * T2_BLOCK), lambda i, j: (0, 0, 0)),
            pl.BlockSpec((None, k, T2_BLOCK, tc), lambda i, j: (group, 0, i, j)),
        ],
        out_specs=pl.BlockSpec((2, k1p, T2_BLOCK, tc), lambda i, j: (0, 0, i, j)),
        out_shape=jax.ShapeDtypeStruct((2, k1p, n2, c), BF16),
        compiler_params=_params(("parallel", "parallel")),
        name="dft_stage1",
    )(f, u4)


def _dft_stage2_filter_kernel(a_ref, g_ref, scale_ref, kf_ref, *, k1c):
    scale = scale_ref[...]

    def body(i, carry):
        ar, ai = a_ref[0, i], a_ref[1, i]
        gr, gi = g_ref[0, i], g_ref[1, i]
        dot = functools.partial(jnp.dot, preferred_element_type=F32)
        kf_ref[0, i] = ((dot(gr, ar) - dot(gi, ai)) * scale).astype(kf_ref.dtype)
        kf_ref[1, i] = ((dot(gr, ai) + dot(gi, ar)) * scale).astype(kf_ref.dtype)
        return carry

    lax.fori_loop(0, k1c, body, 0, unroll=STAGE2_UNROLL)


def _dft_stage2_conv_kernel(a_ref, g_ref, kf_ref, b_ref, *, k1c):
    def body(i, carry):
        ar, ai = a_ref[0, i], a_ref[1, i]
        gr, gi = g_ref[0, i], g_ref[1, i]
        dot = functools.partial(jnp.dot, preferred_element_type=F32)
        xr = dot(gr, ar) - dot(gi, ai)
        xi = dot(gr, ai) + dot(gi, ar)
        kr, ki = kf_ref[0, i].astype(F32), kf_ref[1, i].astype(F32)
        zr =(xr * kr - xi * ki).astype(BF16)
        zi = (xr * ki + xi * kr).astype(BF16)
        grt, git = g_ref[2, i], g_ref[3, i]
        b_ref[0, i] = (dot(grt, zr) + dot(git, zi)).astype(b_ref.dtype)
        b_ref[1, i] = (dot(grt, zi) - dot(git, zr)).astype(b_ref.dtype)
        return carry

    lax.fori_loop(0, k1c, body, 0, unroll=STAGE2_UNROLL)


def _dft_stage2(a4, g, kf=None, scale=None):
    _, k1p, n2, c = a4.shape
    k1c = STAGE2_K1
    tc = _tile(c, 256, LANES)
    a_spec = pl.BlockSpec((2, k1c, n2, tc), lambda i, j: (0, i, 0, j))
    g_spec = pl.BlockSpec((4, k1c, n2, n2), lambda i, j: (0, i, 0, 0))
    if kf is None:
        s_spec = pl.BlockSpec((1, tc), lambda i, j: (0, j))
        body, args, in_specs, dtype = _dft_stage2_filter_kernel, (a4, g, scale), [a_spec, g_spec, s_spec], BF16
    else:
        body, args, in_specs, dtype = _dft_stage2_conv_kernel, (a4, g, kf), [a_spec, g_spec, a_spec], BF16
    return pl.pallas_call(
        functools.partial(body, k1c=k1c),
        grid=(k1p // k1c, c // tc),
        in_specs=in_specs,
        out_specs=a_spec,
        out_shape=jax.ShapeDtypeStruct(a4.shape, dtype),
        compiler_params=_params(("parallel", "arbitrary")),
        name="dft_stage2",
    )(*args)


def _dft_stage3_kernel(c_ref, b_ref, u_ref, m_ref, skip_ref, o_ref):
    _, k1p, r, tc = b_ref.shape
    rows = u_ref.shape[0] * r
    y = (jnp.dot(c_ref[0], b_ref[0].reshape(k1p * r, tc), preferred_element_type=F32)
         - jnp.dot(c_ref[1], b_ref[1].reshape(k1p * r, tc), preferred_element_type=F32))
    u = u_ref[...].reshape(rows, tc).astype(F32)
    out = m_ref[...].reshape(rows, tc).astype(F32) * (y + u * skip_ref[...])
    o_ref[0] = out.astype(o_ref.dtype).reshape(o_ref.shape[1:])


def _dft_stage3(cinv, b4, u4, u_group, m4, m_group, skip):
    _, k1p, n2, c = b4.shape
    n_t1 = cinv.shape[1] // T2_BLOCK
    tc = _tile(c, 256, LANES)
    sig_spec = lambda grp: pl.BlockSpec((None, n_t1, T2_BLOCK, tc), lambda i, j: (grp, 0, i, j))
    return pl.pallas_call(
        _dft_stage3_kernel,
        grid=(n2 // T2_BLOCK, c // tc),
        in_specs=[
            pl.BlockSpec((2, n_t1 * T2_BLOCK, k1p * T2_BLOCK), lambda i, j: (0, 0, 0)),
            pl.BlockSpec((2, k1p, T2_BLOCK, tc), lambda i, j: (0, 0, i, j)),
            sig_spec(u_group),
            sig_spec(m_group),
            pl.BlockSpec((1, tc), lambda i, j: (0, j)),
        ],
        out_specs=pl.BlockSpec((1, n_t1, T2_BLOCK, tc), lambda i, j: (0, 0, i, j)),
        out_shape=jax.ShapeDtypeStruct((1, n_t1, n2, c), BF16),
        compiler_params=_params(("parallel", "parallel")),
        name="dft_stage3",
    )(cinv, b4, u4, m4, skip.reshape(1, c))


def _hyena_latent(p_lat, kern, inv_norm, skip, consts):
    _, length, c = p_lat.shape
    n1, n_t1 = consts["n1"], consts["n_t1"]
    p4 = p_lat.reshape(HYENA_ORDER + 1, n_t1, DFT_N2, c)
    k4 = kern.reshape(HYENA_ORDER, n1, DFT_N2, c)
    u4, u_group = p4, 0
    for order in range(HYENA_ORDER):
        kf = _dft_stage2(_dft_stage1(consts["f1"], k4, order), consts["g"], scale=inv_norm[order:order + 1])
        b = _dft_stage2(_dft_stage1(consts["f1_sig"], u4, u_group), consts["g"], kf)
        u4 = _dft_stage3(consts["cinv"], b, u4, u_group, p4, order + 1, skip[order])
        u_group = 0
    return u4.reshape(length, c)


def _ctx_dft_constants(length):
    n = 2 * length
    k = np.arange(n, dtype=np.float64)
    ang = -2.0 * np.pi * k[:, None] * k[None, :] / n
    fwd = np.stack([np.cos(ang), np.sin(ang)])
    inv = np.stack([np.cos(ang[:length]), -np.sin(ang[:length])]) / n
    as_f32 = lambda a: jnp.asarray(a.astype(np.float32))
    return dict(fwd=as_f32(fwd), fwd_sig=as_f32(fwd[:, :, :length]), inv=as_f32(inv))


def _ctxconv_kernel(ff_ref, fs_ref, fi_ref, k_ref, scale_ref, u_ref, m_ref, skip_ref, o_ref):
    dot = functools.partial(jnp.dot, preferred_element_type=F32, precision=HIGHEST)
    k = k_ref[...] * scale_ref[...]
    u = u_ref[...].astype(F32)
    kr, ki = dot(ff_ref[0], k), dot(ff_ref[1], k)
    xr, xi = dot(fs_ref[0], u), dot(fs_ref[1], u)
    zr = xr * kr - xi * ki
    zi = xr * ki + xi * kr
    y = dot(fi_ref[0], zr) - dot(fi_ref[1], zi)
    o_ref[0] = (m_ref[...].astype(F32) * (y + u * skip_ref[...])).astype(o_ref.dtype)


def _ctxconv(consts, kern, order, inv_norm, u3, u_group, m3, m_group, skip):
    _, length, c = u3.shape
    n = 2 * length
    tc = _tile(c, 256, LANES)
    return pl.pallas_call(
        _ctxconv_kernel,
        grid=(c // tc,),
        in_specs=[
            pl.BlockSpec((2, n, n), lambda j: (0, 0, 0)),
            pl.BlockSpec((2, n, length), lambda j: (0, 0, 0)),
            pl.BlockSpec((2, length, n), lambda j: (0, 0, 0)),
            pl.BlockSpec((None, n, tc), lambda j: (order, 0, j)),
            pl.BlockSpec((1, tc), lambda j: (0, j)),
            pl.BlockSpec((None, length, tc), lambda j: (u_group, 0, j)),
            pl.BlockSpec((None, length, tc), lambda j: (m_group, 0, j)),
            pl.BlockSpec((1, tc), lambda j: (0, j)),
        ],
        out_specs=pl.BlockSpec((1, length, tc), lambda j: (0, 0, j)),
        out_shape=jax.ShapeDtypeStruct((1, length, c), BF16),
        compiler_params=_params(("parallel",)),
        name="ctxconv",
    )(consts["fwd"], consts["fwd_sig"], consts["inv"], kern, inv_norm[order:order + 1], u3, m3, skip.reshape(1, c))


def _hyena_context(p_ctx, kern, inv_norm, skip, consts):
    u3, u_group = p_ctx, 0
    for order in range(HYENA_ORDER):
        u3 = _ctxconv(consts, kern, order, inv_norm, u3, u_group, p_ctx, order + 1, skip[order])
        u_group = 0
    return u3[0]


def kernel(x, c, ctx, c_ctx, norm1_g, norm2_g, mod_a, mod_b, mod_bias, w_in, q_norm_g, k_norm_g, hy_conv_w, hy_conv_b, hy_w1, hy_b1, hy_w2, hy_b2, hy_w3, hy_b3, hy_w4, hy_freq, hy_skip, gm_ln_g, gm_ln_b, gm_ws, gm_bs, gate_up, gate_b, w_br, w_out, rt_group_w, rt_group_b, rt_expert_w, rt_expert_b, ex_w_gate, ex_w_up, ex_w_down, final_norm_g):
    bsz, seq, d = x.shape
    assert bsz == 1 and c.shape[0] == 1
    n_ctx = ctx.shape[1]
    depth = w_in.shape[0]
    t_all = seq + n_ctx
    d_hy = hy_skip.shape[2]
    d_gm = gm_ln_g.shape[1]
    d_at = w_br.shape[1] - d_hy - d_gm
    gate_rank = gate_up.shape[2]
    n_q = d_at // HEAD_DIM
    n_kv = n_q // KV_REP
    q_end = d_at
    v_start = q_end + n_kv * HEAD_DIM
    v_end = v_start + n_kv * HEAD_DIM
    hy_end = v_end + (HYENA_ORDER + 1) * d_hy
    gm_end = hy_end + 2 * d_gm
    n_groups = rt_group_w.shape[2]
    per_group = rt_expert_w.shape[3]
    n_experts = n_groups * per_group
    assert n_groups + n_experts <= ROUTER_COLS
    assert hy_end % (2 * d_gm) == 0 and gm_end % gate_rank == 0 and seq % ROW_CHUNK == 0
    tme = 256

    xs = jnp.concatenate([x[0], ctx[0]], axis=0)
    cond = jnp.stack([c[0], c_ctx], axis=0)
    cos, sin = _rope_tables(seq, n_ctx)

    tq = _tile(seq, 512, 16)
    tk = _tile(t_all, 2816, 256)
    tq_c = _tile(n_ctx, 256, 16)
    dft_lat = _dft_constants(seq)
    dft_ctx = _ctx_dft_constants(n_ctx)
    w_in_b, w_br_b, gate_up_b, w_out_b = (w.astype(BF16) for w in (w_in, w_br, gate_up, w_out))

    for l in range(depth):
        m = (jnp.dot(jnp.dot(jax.nn.silu(cond), mod_a[l], precision=HIGHEST), mod_b[l], precision=HIGHEST)
             + mod_bias[l]).reshape(2, N_MOD, d)

        px = _modproj(xs, norm1_g[l], m[:, 0], m[:, 1], w_in_b, l, seq)
        q, k = _qkprep(px, cos, sin, q_norm_g[l], k_norm_g[l], n_q, n_kv)
        o_at = _attention(q, k, px, v_start // HEAD_DIM, 0, seq, 0, t_all, tq, tk)
        o_at = _attention(q, k, px, v_start // HEAD_DIM, seq, n_ctx, seq, n_ctx, tq_c, n_ctx, out=o_at)

        hy_w = (hy_w1[l], hy_b1[l], hy_w2[l], hy_b2[l], hy_w3[l], hy_b3[l], hy_w4[l], hy_freq[l])
        p_lat, p_ctx = _shortconv(px, v_end, d_hy, hy_conv_w[l], hy_conv_b[l], seq, n_ctx)
        kern_x, norm_x = _hyena_filters(seq, *hy_w, d_hy, BF16)
        kern_c, norm_c = _hyena_filters(n_ctx, *hy_w, d_hy, F32)
        o_hy_x = _hyena_latent(p_lat, kern_x, 1.0 / norm_x, hy_skip[l], dft_lat)
        o_hy_c = _hyena_context(p_ctx, kern_c, 1.0 / norm_c, hy_skip[l], dft_ctx)
        o_hy = jnp.concatenate([o_hy_x, o_hy_c], axis=0)

        o_gm = _gmlp(px, hy_end // (2 * d_gm), d_gm, gm_ln_g[l], gm_ln_b[l], gm_ws[l], gm_bs[l])

        merged = _merge(o_hy, o_at, o_gm, px, gm_end // gate_rank, gate_rank, w_br_b, gate_up_b, gate_b, l)
        xs = _outproj(merged, w_out_b, l, xs, m[:, 2], seq)

        rw = jnp.concatenate([rt_group_w[l]] + [rt_expert_w[l, g] for g in range(n_groups)], axis=1)
        rw = jnp.pad(rw, ((0, 0), (0, ROUTER_COLS - rw.shape[1])))
        rb = jnp.concatenate([rt_group_b[l], rt_expert_b[l].reshape(-1)])
        rb = jnp.pad(rb, (0, ROUTER_COLS - rb.shape[0])).reshape(1, ROUTER_COLS)
        h2, logits = _router(xs, norm2_g[l], m[:, 3], m[:, 4], _split_bf16(rw, 3), rb, seq)
        eid, wts = _route(logits, n_groups, per_group)
        src_tok, roww, pos, tiles = _dispatch(eid, wts, n_experts, tme)
        take_rows = lambda a, idx: a.at[idx].get(mode="promise_in_bounds")
        xg = take_rows(h2, src_tok)
        yo = _experts(xg, roww, tiles, ex_w_gate, ex_w_up, ex_w_down, l, tme)
        xs = _combine(xs, take_rows(yo, pos[:, 0]), take_rows(yo, pos[:, 1]), m[:, 5], seq)

    return _final_norm(xs, final_norm_g, seq).reshape(1, seq, d)
```

```python
import functools
import math

import numpy as np
import jax
import jax.numpy as jnp
from jax import lax
from jax.experimental import pallas as pl
from jax.experimental.pallas import tpu as pltpu

F32 = jnp.float32
BF16 = jnp.bfloat16
HIGHEST = lax.Precision.HIGHEST

HEAD_DIM = 128
KV_REP = 4
GRID_W = 64
ROPE_THETA = 10000.0
ATTN_SCALE = HEAD_DIM ** -0.5
LOG2_E = math.log2(math.e)
HYENA_ORDER = 2
HYENA_SHIFT = 0.05
HYENA_FAST_PCT = 0.3
HYENA_SLOW_PCT = 1.5
HYENA_TARGET = 1e-2
GMLP_CHUNK = 128
GMLP_GROUP_W = 128
N_BRANCH = 3
N_MOD = 6
TOP_K = 2
EPS = 1e-6
LN_EPS = 1e-5

LANES = 128
ROUTER_COLS = LANES
VMEM_LIMIT_MB = 56


def _params(semantics, vmem_mb=VMEM_LIMIT_MB):
    return pltpu.CompilerParams(dimension_semantics=semantics, vmem_limit_bytes=vmem_mb * 1024 * 1024)


def _tile(n, target, mult):
    best = None
    for d in range(mult, min(n, target) + 1, mult):
        if n % d == 0:
            best = d
    assert best is not None, (n, target, mult)
    return best


ROW_CHUNK = 16


def _modulate_rows(x_ref, g_ref, sh_ref, sc_ref, h_ref, row0, n_lat, n_rows):
    n_chunks = n_rows // ROW_CHUNK
    lat_chunks = jnp.clip((n_lat - row0) // ROW_CHUNK, 0, n_chunks)
    unroll = next(u for u in (4, 3, 2, 1) if n_chunks % u == 0)

    def segment(seg, lo, hi, **loop_kw):
        gain = g_ref[...] * (1.0 + sc_ref[seg:seg + 1, :])
        shift = sh_ref[seg:seg + 1, :]

        def body(r, carry):
            rows = pl.ds(pl.multiple_of(r * ROW_CHUNK, ROW_CHUNK), ROW_CHUNK)
            xf = x_ref[rows, :]
            ms = jnp.mean(xf * xf, axis=-1, keepdims=True)
            h_ref[rows, :] = (xf * lax.rsqrt(ms + EPS) * gain + shift).astype(h_ref.dtype)
            return carry

        lax.fori_loop(lo, hi, body, 0, **loop_kw)

    @pl.when(lat_chunks == n_chunks)
    def _():
        segment(0, 0, n_chunks, unroll=unroll)

    @pl.when(lat_chunks != n_chunks)
    def _():
        segment(0, 0, lat_chunks)
        segment(1, lat_chunks, n_chunks)


def _modproj_kernel(x_ref, g_ref, sh_ref, sc_ref, w_ref, o_ref, h_ref, *, n_lat, tm):
    i = pl.program_id(0)

    @pl.when(pl.program_id(1) == 0)
    def _():
        _modulate_rows(x_ref, g_ref, sh_ref, sc_ref, h_ref, i * tm, n_lat, tm)

    o_ref[...] = jnp.dot(h_ref[...], w_ref[...], preferred_element_type=F32).astype(o_ref.dtype)


def _modproj(x, g, sh, sc, w_all, layer, n_lat):
    t, d = x.shape
    n = w_all.shape[2]
    tm = _tile(t, 528, 16)
    tn = _tile(n, 512, LANES)
    return pl.pallas_call(
        functools.partial(_modproj_kernel, n_lat=n_lat, tm=tm),
        grid=(t // tm, n // tn),
        in_specs=[
            pl.BlockSpec((tm, d), lambda i, j: (i, 0)),
            pl.BlockSpec((1, d), lambda i, j: (0, 0)),
            pl.BlockSpec((2, d), lambda i, j: (0, 0)),
            pl.BlockSpec((2, d), lambda i, j: (0, 0)),
            pl.BlockSpec((None, d, tn), lambda i, j: (layer, 0, j)),
        ],
        out_specs=pl.BlockSpec((tm, tn), lambda i, j: (i, j)),
        out_shape=jax.ShapeDtypeStruct((t, n), BF16),
        scratch_shapes=[pltpu.VMEM((tm, d), BF16)],
        compiler_params=_params(("parallel", "arbitrary")),
        name="modproj",
    )(x, g.reshape(1, d), sh, sc, w_all)


def _qkprep_kernel(p_ref, cos_ref, sin_ref, qg_ref, kg_ref, q_ref, k_ref, *, n_q, n_kv):
    cos = cos_ref[...]
    sin = sin_ref[...]
    lane = lax.broadcasted_iota(jnp.int32, (1, HEAD_DIM), 1)
    first_half = (lane % (HEAD_DIM // 2)) < (HEAD_DIM // 4)

    def head(h, gain, scale):
        xh = p_ref[:, h * HEAD_DIM:(h + 1) * HEAD_DIM].astype(F32)
        ms = jnp.mean(xh * xh, axis=-1, keepdims=True)
        y = xh * lax.rsqrt(ms + EPS) * gain
        partner = jnp.where(first_half,
                            pltpu.roll(y, HEAD_DIM - HEAD_DIM // 4, 1),
                            pltpu.roll(y, HEAD_DIM // 4, 1))
        return (y * cos + partner * sin) * scale

    for h in range(n_q):
        q_ref[h] = head(h, qg_ref[...], ATTN_SCALE * LOG2_E).astype(q_ref.dtype)
    for h in range(n_kv):
        k_ref[h] = head(n_q + h, kg_ref[...], 1.0).astype(k_ref.dtype)


def _qkprep(px, cos, sin, qg, kg, n_q, n_kv):
    t = px.shape[0]
    tr = _tile(t, 264, 8)
    width = (n_q + n_kv) * HEAD_DIM
    return pl.pallas_call(
        functools.partial(_qkprep_kernel, n_q=n_q, n_kv=n_kv),
        grid=(t // tr,),
        in_specs=[
            pl.BlockSpec((tr, width), lambda i: (i, 0)),
            pl.BlockSpec((tr, HEAD_DIM), lambda i: (i, 0)),
            pl.BlockSpec((tr, HEAD_DIM), lambda i: (i, 0)),
            pl.BlockSpec((1, HEAD_DIM), lambda i: (0, 0)),
            pl.BlockSpec((1, HEAD_DIM), lambda i: (0, 0)),
        ],
        out_specs=[
            pl.BlockSpec((n_q, tr, HEAD_DIM), lambda i: (0, i, 0)),
            pl.BlockSpec((n_kv, tr, HEAD_DIM), lambda i: (0, i, 0)),
        ],
        out_shape=[
            jax.ShapeDtypeStruct((n_q, t, HEAD_DIM), BF16),
            jax.ShapeDtypeStruct((n_kv, t, HEAD_DIM), BF16),
        ],
        compiler_params=_params(("parallel",)),
        name="qkprep",
    )(px, cos, sin, qg.reshape(1, HEAD_DIM), kg.reshape(1, HEAD_DIM))


def _rope_tables(n_lat, n_ctx):
    n = HEAD_DIM // 4
    inv = ROPE_THETA ** (-np.arange(n, dtype=np.float32) / n)
    t = np.arange(n_lat)
    rows = (t // GRID_W).astype(np.float32)
    cols = (t % GRID_W).astype(np.float32)
    ang = np.concatenate([rows[:, None] * inv[None, :]] * 2 + [cols[:, None] * inv[None, :]] * 2, axis=1)
    ang = jnp.asarray(ang.astype(np.float32))
    sign = np.tile(np.concatenate([-np.ones(n, np.float32), np.ones(n, np.float32)]), 2)
    cos = jnp.concatenate([jnp.cos(ang), jnp.ones((n_ctx, HEAD_DIM), F32)], axis=0)
    sin = jnp.concatenate([jnp.sin(ang) * sign[None, :], jnp.zeros((n_ctx, HEAD_DIM), F32)], axis=0)
    return cos, sin


def _attn_kernel(*refs, tq, tk, n_kv, aliased):
    if aliased:
        q_ref, qn_ref, k_ref, v_ref, _, o_ref, s_ref, m_ref, l_ref, acc_ref = refs
    else:
        q_ref, qn_ref, k_ref, v_ref, o_ref, s_ref, m_ref, l_ref, acc_ref = refs
    n_chunks = n_kv // tk
    assert (n_chunks * KV_REP) % 2 == 0
    m_ref[...] = jnp.full(m_ref.shape, -jnp.inf, F32)
    l_ref[...] = jnp.zeros(l_ref.shape, F32)
    acc_ref[...] = jnp.zeros(acc_ref.shape, F32)

    def scores(q, ks):
        return lax.dot_general(q, k_ref[0, ks, :], (((1,), (1,)), ((), ())), preferred_element_type=F32)

    @pl.when(pl.program_id(1) == 0)
    def _():
        s_ref[0] = scores(q_ref[0], pl.ds(0, tk))

    def body(c, carry):
        ks = pl.ds(pl.multiple_of(c * tk, tk), tk)
        last = c + 1 == n_chunks
        kn = pl.ds(pl.multiple_of(jnp.where(last, 0, c + 1) * tk, tk), tk)
        v = v_ref[ks, :]
        for r in range(KV_REP):
            cur, nxt = r % 2, (r + 1) % 2
            if r + 1 < KV_REP:
                s_ref[nxt] = scores(q_ref[r + 1], ks)
            else:
                s_ref[nxt] = scores(jnp.where(last, qn_ref[0], q_ref[0]), kn)
            s = s_ref[cur]
            m_prev = m_ref[r]
            m_new = jnp.maximum(m_prev, jnp.max(s, axis=-1, keepdims=True))
            alpha = jnp.exp2(m_prev - m_new)
            p = jnp.exp2(s - m_new)
            l_ref[r] = alpha * l_ref[r] + jnp.sum(p, axis=-1, keepdims=True)
            acc_ref[r] = alpha * acc_ref[r] + jnp.dot(p.astype(BF16), v, preferred_element_type=F32)
            m_ref[r] = m_new
        return carry

    lax.fori_loop(0, n_chunks, body, 0)
    for r in range(KV_REP):
        o_ref[:, r * HEAD_DIM:(r + 1) * HEAD_DIM] = (acc_ref[r] * (1.0 / l_ref[r])).astype(o_ref.dtype)


def _attention(q, k, px, v_col_block, q_row0, n_q_rows, kv_row0, n_kv, tq, tk, out=None):
    n_heads, t, _ = q.shape
    n_kv_heads = k.shape[0]
    assert q_row0 % tq == 0 and n_q_rows % tq == 0 and kv_row0 % n_kv == 0 and n_kv % tk == 0
    qb0 = q_row0 // tq
    kb0 = kv_row0 // n_kv
    aliased = out is not None
    n_tiles = n_q_rows // tq
    in_specs = [
        pl.BlockSpec((KV_REP, tq, HEAD_DIM), lambda g, i: (g, qb0 + i, 0)),
        pl.BlockSpec((1, tq, HEAD_DIM), lambda g, i: (g * KV_REP, qb0 + jnp.minimum(i + 1, n_tiles - 1), 0)),
        pl.BlockSpec((1, n_kv, HEAD_DIM), lambda g, i: (g, kb0, 0)),
        pl.BlockSpec((n_kv, HEAD_DIM), lambda g, i: (kb0, v_col_block + g)),
    ]
    args = [q, q, k, px]
    if aliased:
        in_specs.append(pl.BlockSpec(memory_space=pl.ANY))
        args.append(out)
    return pl.pallas_call(
        functools.partial(_attn_kernel, tq=tq, tk=tk, n_kv=n_kv, aliased=aliased),
        grid=(n_kv_heads, n_q_rows // tq),
        in_specs=in_specs,
        out_specs=pl.BlockSpec((tq, KV_REP * HEAD_DIM), lambda g, i: (qb0 + i, g)),
        out_shape=jax.ShapeDtypeStruct((t, n_heads * HEAD_DIM), BF16),
        scratch_shapes=[
            pltpu.VMEM((2, tq, tk), F32),
            pltpu.VMEM((KV_REP, tq, 1), F32),
            pltpu.VMEM((KV_REP, tq, 1), F32),
            pltpu.VMEM((KV_REP, tq, HEAD_DIM), F32),
        ],
        input_output_aliases={4: 0} if aliased else {},
        compiler_params=_params(("parallel", "arbitrary")),
        name="attention",
    )(*args)


def _gmlp_kernel(p_ref, lng_ref, lnb_ref, ws_ref, bst_ref, o_ref, *, n_chunks, d_g):
    n_groups = d_g // GMLP_GROUP_W
    for b in range(n_chunks):
        rows = slice(b * GMLP_CHUNK, (b + 1) * GMLP_CHUNK)
        uv = jax.nn.gelu(p_ref[rows, :].astype(F32))
        u = uv[:, :d_g]
        v = uv[:, d_g:]
        vc = v - jnp.mean(v, axis=-1, keepdims=True)
        var = jnp.mean(vc * vc, axis=-1, keepdims=True)
        vn = (vc * lax.rsqrt(var + LN_EPS) * lng_ref[...] + lnb_ref[...]).astype(BF16)
        for g in range(n_groups):
            cols = slice(g * GMLP_GROUP_W, (g + 1) * GMLP_GROUP_W)
            vm = jnp.dot(ws_ref[g].astype(BF16), vn[:, cols], preferred_element_type=F32) + bst_ref[:, g:g + 1]
            o_ref[rows, cols] = (u[:, cols] * vm).astype(o_ref.dtype)


def _gmlp(px, col_block, d_g, ln_g, ln_b, ws, bs):
    t = px.shape[0]
    n_chunks = 2
    tr = n_chunks * GMLP_CHUNK
    assert t % tr == 0
    n_groups = d_g // GMLP_GROUP_W
    return pl.pallas_call(
        functools.partial(_gmlp_kernel, n_chunks=n_chunks, d_g=d_g),
        grid=(t // tr,),
        in_specs=[
            pl.BlockSpec((tr, 2 * d_g), lambda i: (i, col_block)),
            pl.BlockSpec((1, d_g), lambda i: (0, 0)),
            pl.BlockSpec((1, d_g), lambda i: (0, 0)),
            pl.BlockSpec((n_groups, GMLP_CHUNK, GMLP_CHUNK), lambda i: (0, 0, 0)),
            pl.BlockSpec((GMLP_CHUNK, n_groups), lambda i: (0, 0)),
        ],
        out_specs=pl.BlockSpec((tr, d_g), lambda i: (i, 0)),
        out_shape=jax.ShapeDtypeStruct((t, d_g), BF16),
        compiler_params=_params(("parallel",)),
        name="gmlp",
    )(px, ln_g.reshape(1, d_g), ln_b.reshape(1, d_g), ws, bs.T)


def _merge_kernel(hy_ref, at_ref, gm_ref, gc_ref, wbr_ref, gup_ref, gb_ref, o_ref, *, d_hy, d_at):
    gc = gc_ref[...]
    bounds = (0, d_hy, d_hy + d_at, wbr_ref.shape[0])
    merged = None
    for b, a_ref in enumerate((hy_ref, at_ref, gm_ref)):
        y = jnp.dot(a_ref[...], wbr_ref[bounds[b]:bounds[b + 1], :], preferred_element_type=F32)
        gate = jax.nn.sigmoid(jnp.dot(gc, gup_ref[b], preferred_element_type=F32) + gb_ref[b:b + 1, :])
        merged = gate * y if merged is None else merged + gate * y
    o_ref[...] = merged.astype(o_ref.dtype)


def _merge(o_hy, o_at, o_gm, px, gc_col_block, gate_rank, w_br, gate_up, gate_b, layer):
    t = px.shape[0]
    d_hy, d_at, d_gm = o_hy.shape[1], o_at.shape[1], o_gm.shape[1]
    d = w_br.shape[2]
    tm = _tile(t, 528, 16)
    tn = _tile(d, 1024, LANES)
    return pl.pallas_call(
        functools.partial(_merge_kernel, d_hy=d_hy, d_at=d_at),
        grid=(t // tm, d // tn),
        in_specs=[
            pl.BlockSpec((tm, d_hy), lambda i, j: (i, 0)),
            pl.BlockSpec((tm, d_at), lambda i, j: (i, 0)),
            pl.BlockSpec((tm, d_gm), lambda i, j: (i, 0)),
            pl.BlockSpec((tm, gate_rank), lambda i, j: (i, gc_col_block)),
            pl.BlockSpec((None, d_hy + d_at + d_gm, tn), lambda i, j: (layer, 0, j)),
            pl.BlockSpec((None, N_BRANCH, gate_rank, tn), lambda i, j: (layer, 0, 0, j)),
            pl.BlockSpec((None, N_BRANCH, tn), lambda i, j: (layer, 0, j)),
        ],
        out_specs=pl.BlockSpec((tm, tn), lambda i, j: (i, j)),
        out_shape=jax.ShapeDtypeStruct((t, d), BF16),
        compiler_params=_params(("parallel", "arbitrary")),
        name="merge",
    )(o_hy, o_at, o_gm, px, w_br, gate_up, gate_b)


def _outproj_kernel(a_ref, w_ref, x_ref, gate_ref, o_ref, *, n_lat, tm):
    y = jnp.dot(a_ref[...], w_ref[...], preferred_element_type=F32)
    rid = pl.program_id(0) * tm + lax.broadcasted_iota(jnp.int32, (tm, 1), 0)
    gate = jnp.where(rid >= n_lat, gate_ref[1:2, :], gate_ref[0:1, :])
    o_ref[...] = x_ref[...] + gate * y


def _outproj(a, w_all, layer, x, gate, n_lat):
    t, k = a.shape
    d = w_all.shape[2]
    tm = _tile(t, 528, 16)
    tn = _tile(d, 1024, LANES)
    return pl.pallas_call(
        functools.partial(_outproj_kernel, n_lat=n_lat, tm=tm),
        grid=(t // tm, d // tn),
        in_specs=[
            pl.BlockSpec((tm, k), lambda i, j: (i, 0)),
            pl.BlockSpec((None, k, tn), lambda i, j: (layer, 0, j)),
            pl.BlockSpec((tm, tn), lambda i, j: (i, j)),
            pl.BlockSpec((2, tn), lambda i, j: (0, j)),
        ],
        out_specs=pl.BlockSpec((tm, tn), lambda i, j: (i, j)),
        out_shape=jax.ShapeDtypeStruct((t, d), F32),
        compiler_params=_params(("parallel", "arbitrary")),
        name="outproj",
    )(a, w_all, x, gate)


def _split_bf16(w, parts):
    out = []
    rest = w.astype(F32)
    for _ in range(parts):
        piece = rest.astype(BF16)
        out.append(piece)
        rest = rest - piece.astype(F32)
    return jnp.stack(out)


def _router_kernel(x_ref, g_ref, sh_ref, sc_ref, rw_ref, rb_ref, h_ref, lg_ref, *, n_lat, tm):
    _modulate_rows(x_ref, g_ref, sh_ref, sc_ref, h_ref, pl.program_id(0) * tm, n_lat, tm)
    h = h_ref[...]
    logits = rb_ref[...]
    for part in range(rw_ref.shape[0]):
        logits = logits + jnp.dot(h, rw_ref[part], preferred_element_type=F32)
    lg_ref[...] = logits


def _router(x, g, sh, sc, rw, rb, n_lat):
    t, d = x.shape
    tm = _tile(t, 264, 16)
    return pl.pallas_call(
        functools.partial(_router_kernel, n_lat=n_lat, tm=tm),
        grid=(t // tm,),
        in_specs=[
            pl.BlockSpec((tm, d), lambda i: (i, 0)),
            pl.BlockSpec((1, d), lambda i: (0, 0)),
            pl.BlockSpec((2, d), lambda i: (0, 0)),
            pl.BlockSpec((2, d), lambda i: (0, 0)),
            pl.BlockSpec((rw.shape[0], d, ROUTER_COLS), lambda i: (0, 0, 0)),
            pl.BlockSpec((1, ROUTER_COLS), lambda i: (0, 0)),
        ],
        out_specs=[
            pl.BlockSpec((tm, d), lambda i: (i, 0)),
            pl.BlockSpec((tm, ROUTER_COLS), lambda i: (i, 0)),
        ],
        out_shape=[
            jax.ShapeDtypeStruct((t, d), BF16),
            jax.ShapeDtypeStruct((t, ROUTER_COLS), F32),
        ],
        compiler_params=_params(("parallel",)),
        name="router",
    )(x, g.reshape(1, d), sh, sc, rw, rb)


def _experts_kernel(te_ref, tb_ref, tf_ref, ts_ref, ne_ref, nu_ref, x_ref, rw_ref, wg_hbm, wu_hbm, wd_hbm, o_ref,
                    wg_f, wu_f, wd_f, wg_s, wu_s, wd_s, sem, *, layer):
    i = pl.program_id(0)

    def weight_copies(expert, slot):
        return [pltpu.make_async_copy(hbm.at[layer, expert], stage.at[slot], sem.at[slot, n])
                for n, (hbm, stage) in enumerate(((wg_hbm, wg_f), (wu_hbm, wu_f), (wd_hbm, wd_f)))]

    @pl.when(i < nu_ref[0])
    def _():
        slot = ts_ref[i]

        @pl.when(tf_ref[i] == 1)
        def _():
            @pl.when(i == 0)
            def _():
                for cp in weight_copies(te_ref[0], 0):
                    cp.start()

            for cp in weight_copies(te_ref[i], slot):
                cp.wait()
            wg_s[...] = wg_f[slot].astype(BF16)
            wu_s[...] = wu_f[slot].astype(BF16)
            wd_s[...] = wd_f[slot].astype(BF16)

            @pl.when(ne_ref[i] >= 0)
            def _():
                for cp in weight_copies(ne_ref[i], 1 - slot):
                    cp.start()

        xg = x_ref[...]
        hg = jnp.dot(xg, wg_s[...], preferred_element_type=F32)
        hu = jnp.dot(xg, wu_s[...], preferred_element_type=F32)
        hid = (jax.nn.silu(hg) * hu * rw_ref[...]).astype(BF16)
        o_ref[...] = jnp.dot(hid, wd_s[...], preferred_element_type=F32).astype(o_ref.dtype)

    @pl.when(i >= nu_ref[0])
    def _():
        o_ref[...] = jnp.zeros(o_ref.shape, o_ref.dtype)


def _experts(xg, roww, tiles, w_gate, w_up, w_down, layer, tme):
    p, d = xg.shape
    f = w_gate.shape[3]
    n_tiles = p // tme
    row_block = lambda i, te, tb, tf, ts, ne, nu: (tb[i], 0)
    grid_spec = pltpu.PrefetchScalarGridSpec(
        num_scalar_prefetch=6,
        grid=(n_tiles,),
        in_specs=[
            pl.BlockSpec((tme, d), row_block),
            pl.BlockSpec((tme, 1), row_block),
            pl.BlockSpec(memory_space=pl.ANY),
            pl.BlockSpec(memory_space=pl.ANY),
            pl.BlockSpec(memory_space=pl.ANY),
        ],
        out_specs=pl.BlockSpec((tme, d), lambda i, te, tb, tf, ts, ne, nu: (i, 0)),
        scratch_shapes=[
            pltpu.VMEM((2, d, f), F32),
            pltpu.VMEM((2, d, f), F32),
            pltpu.VMEM((2, f, d), F32),
            pltpu.VMEM((d, f), BF16),
            pltpu.VMEM((d, f), BF16),
            pltpu.VMEM((f, d), BF16),
            pltpu.SemaphoreType.DMA((2, 3)),
        ],
    )
    return pl.pallas_call(
        functools.partial(_experts_kernel, layer=layer),
        grid_spec=grid_spec,
        out_shape=jax.ShapeDtypeStruct((p, d), BF16),
        compiler_params=_params(("arbitrary",)),
        name="experts",
    )(*tiles, xg, roww, w_gate, w_up, w_down)


def _combine_kernel(x_ref, y0_ref, y1_ref, gate_ref, o_ref, *, n_lat, tm):
    rid = pl.program_id(0) * tm + lax.broadcasted_iota(jnp.int32, (tm, 1), 0)
    gate = jnp.where(rid >= n_lat, gate_ref[1:2, :], gate_ref[0:1, :])
    o_ref[...] = x_ref[...] + gate * (y0_ref[...].astype(F32) + y1_ref[...].astype(F32))


def _combine(x, y0, y1, gate, n_lat):
    t, d = x.shape
    tm = _tile(t, 264, 16)
    row = lambda: pl.BlockSpec((tm, d), lambda i: (i, 0))
    return pl.pallas_call(
        functools.partial(_combine_kernel, n_lat=n_lat, tm=tm),
        grid=(t // tm,),
        in_specs=[row(), row(), row(), pl.BlockSpec((2, d), lambda i: (0, 0))],
        out_specs=row(),
        out_shape=jax.ShapeDtypeStruct((t, d), F32),
        compiler_params=_params(("parallel",)),
        name="moe_combine",
    )(x, y0, y1, gate)


def _route(logits, n_groups, per_group):
    lg = logits[:, :n_groups]
    le_all = logits[:, n_groups:n_groups + n_groups * per_group].reshape(-1, n_groups, per_group)
    grp = jnp.argmax(lg, axis=-1)
    p_grp = jnp.take_along_axis(jax.nn.softmax(lg, axis=-1), grp[:, None], axis=-1)
    le = jnp.take_along_axis(le_all, grp[:, None, None], axis=1)[:, 0]
    top_p, top_i = lax.top_k(jax.nn.softmax(le, axis=-1), TOP_K)
    w = top_p / jnp.sum(top_p, axis=-1, keepdims=True) * p_grp
    eid = grp[:, None] * per_group + top_i
    return eid.astype(jnp.int32), w


def _dispatch(eid, w, n_experts, tme):
    t = eid.shape[0]
    n_pairs = t * TOP_K
    p_rows = n_pairs + n_experts * tme
    p_rows = -(-p_rows // tme) * tme
    flat_e = eid.reshape(-1)
    onehot = (flat_e[:, None] == jnp.arange(n_experts, dtype=jnp.int32)[None, :]).astype(jnp.int32)
    csum = jnp.cumsum(onehot, axis=0)
    rank = jnp.take_along_axis(csum, flat_e[:, None], axis=1)[:, 0] - 1
    counts = csum[-1]
    padded = -(-counts // tme) * tme
    ends = jnp.cumsum(padded)
    starts = ends - padded
    dest = starts[flat_e] + rank
    assert t < 2 ** 24
    tok = (jnp.arange(n_pairs, dtype=jnp.int32) // TOP_K).astype(F32)
    slots = jnp.full((p_rows, 2), -1.0, F32).at[dest].set(jnp.stack([tok, w.reshape(-1)], axis=1))
    is_pad = slots[:, 0] < 0
    src_tok = jnp.where(is_pad, jnp.arange(p_rows, dtype=jnp.int32) % t, slots[:, 0].astype(jnp.int32))
    roww = jnp.where(is_pad, 0.0, slots[:, 1])
    n_used = (ends[-1] // tme).astype(jnp.int32)
    tile_ids = jnp.arange(p_rows // tme, dtype=jnp.int32)
    tile_block = jnp.minimum(tile_ids, n_used - 1)
    tile_expert = jnp.minimum(
        jnp.searchsorted(ends, tile_block * tme, side="right").astype(jnp.int32), n_experts - 1)
    tile_first = jnp.logical_and(tile_block * tme == starts[tile_expert], tile_ids < n_used).astype(jnp.int32)
    ordinal = jnp.maximum(jnp.cumsum(tile_first) - 1, 0)
    used = counts > 0
    used_experts = jnp.argsort(jnp.logical_not(used), stable=True).astype(jnp.int32)
    next_expert = jnp.where(ordinal + 1 < jnp.sum(used),
                            used_experts[jnp.minimum(ordinal + 1, n_experts - 1)], -1).astype(jnp.int32)
    tiles = (tile_expert, tile_block, tile_first, (ordinal % 2).astype(jnp.int32), next_expert, n_used.reshape(1))
    return src_tok, roww.reshape(-1, 1), dest.reshape(t, TOP_K), tiles


def _rmsnorm_kernel(x_ref, g_ref, o_ref):
    xf = x_ref[...]
    ms = jnp.mean(xf * xf, axis=-1, keepdims=True)
    o_ref[...] = xf * lax.rsqrt(ms + EPS) * g_ref[...]


def _final_norm(x, g, n_rows):
    d = x.shape[1]
    tr = _tile(n_rows, 256, 8)
    return pl.pallas_call(
        _rmsnorm_kernel,
        grid=(n_rows // tr,),
        in_specs=[pl.BlockSpec((tr, d), lambda i: (i, 0)), pl.BlockSpec((1, d), lambda i: (0, 0))],
        out_specs=pl.BlockSpec((tr, d), lambda i: (i, 0)),
        out_shape=jax.ShapeDtypeStruct((n_rows, d), F32),
        compiler_params=_params(("parallel",)),
        name="final_norm",
    )(x, g.reshape(1, d))


SUBLANES = 8


def _filter_features(length, emb):
    bands = (emb - 1) // 2
    i = np.arange(length, dtype=np.float64)
    t = i / (length - 1)
    f = np.linspace(1e-4, bands - 1, bands)
    w = 2.0 * np.pi * i / length
    z = np.concatenate([t[:, None], np.cos(-f[None, :] * w[:, None]), np.sin(-f[None, :] * w[:, None])], axis=1)
    z = np.concatenate([z, z[::-1]], axis=0)
    emb_pad = -(-emb // SUBLANES) * SUBLANES
    z = np.pad(z, ((0, 0), (0, emb_pad - emb)))
    return jnp.asarray(z.astype(np.float32))


def _filter_kernel(z_ref, w1_ref, b1_ref, w2_ref, b2_ref, w3_ref, b3_ref, w4_ref, fr_ref, dl_ref,
                   k_ref, nrm_ref, *, d_hy):
    dot = functools.partial(jnp.dot, preferred_element_type=F32, precision=HIGHEST)

    @pl.when(jnp.logical_and(pl.program_id(0) == 0, pl.program_id(1) == 0))
    def _():
        nrm_ref[...] = jnp.zeros(nrm_ref.shape, F32)

    z = z_ref[...]
    fr = fr_ref[...]
    h = jnp.sin(fr * (dot(z, w1_ref[...]) + b1_ref[...]))
    h = jnp.sin(fr * (dot(h, w2_ref[...]) + b2_ref[...]))
    h = jnp.sin(fr * (dot(h, w3_ref[...]) + b3_ref[...]))
    h_hi = h.astype(BF16)
    h_lo = (h - h_hi.astype(F32)).astype(BF16)
    hk = (jnp.dot(h_hi, w4_ref[0], preferred_element_type=F32)
          + jnp.dot(h_lo, w4_ref[0], preferred_element_type=F32)
          + jnp.dot(h_hi, w4_ref[1], preferred_element_type=F32))
    window = jnp.exp(-z[:, 0:1] * dl_ref[...]) + HYENA_SHIFT
    for o in range(HYENA_ORDER):
        k = hk[:, o * d_hy:(o + 1) * d_hy] * window
        k_ref[o] = k.astype(k_ref.dtype)
        nrm_ref[o:o + 1, :] += jnp.sum(jnp.abs(k), axis=0, keepdims=True)


def _hyena_filters(length, w1, b1, w2, b2, w3, b3, w4, freq, d_hy, out_dtype):
    emb, hidden = w1.shape
    z = _filter_features(length, emb)
    emb_pad = z.shape[1]
    w1p = jnp.pad(w1, ((0, emb_pad - emb), (0, 0)))
    max_decay = math.log(HYENA_TARGET) / HYENA_FAST_PCT
    min_decay = math.log(HYENA_TARGET) / HYENA_SLOW_PCT
    deltas = jnp.asarray(np.abs(np.linspace(min_decay, max_decay, d_hy)).astype(np.float32)).reshape(1, d_hy)
    tr = _tile(length, 512, SUBLANES)
    per = length // tr
    vec = lambda a: a.reshape(1, hidden)
    full = lambda shape: pl.BlockSpec(shape, lambda d, i: (0,) * len(shape))
    return pl.pallas_call(
        functools.partial(_filter_kernel, d_hy=d_hy),
        grid=(2, per),
        in_specs=[
            pl.BlockSpec((tr, emb_pad), lambda d, i: (d * per + i, 0)),
            full((emb_pad, hidden)), full((1, hidden)),
            full((hidden, hidden)), full((1, hidden)),
            full((hidden, hidden)), full((1, hidden)),
            pl.BlockSpec((2, hidden, HYENA_ORDER * d_hy), lambda d, i: (0, 0, d)),
            full((1, hidden)), full((1, d_hy)),
        ],
        out_specs=[
            pl.BlockSpec((HYENA_ORDER, tr, d_hy), lambda d, i: (0, d * per + i, 0)),
            pl.BlockSpec((HYENA_ORDER, d_hy), lambda d, i: (0, 0)),
        ],
        out_shape=[
            jax.ShapeDtypeStruct((HYENA_ORDER, 2 * length, d_hy), out_dtype),
            jax.ShapeDtypeStruct((HYENA_ORDER, d_hy), F32),
        ],
        compiler_params=_params(("arbitrary", "arbitrary")),
        name="hyena_filter",
    )(z, w1p, vec(b1), w2, vec(b2), w3, vec(b3), _split_bf16(w4, 2), vec(freq), deltas)


CONV_ROWS = 64
CONV_HALO = 8


def _shortconv_kernel(p_ref, w_ref, b_ref, lat_ref, ctx_ref, pad_ref, *, n_lat, n_ctx):
    t = n_lat + n_ctx
    tc = p_ref.shape[1]
    pad_ref[0:CONV_HALO, :] = jnp.zeros((CONV_HALO, tc), F32)
    pad_ref[t + CONV_HALO:t + 2 * CONV_HALO, :] = jnp.zeros((CONV_HALO, tc), F32)

    def fill(r, carry):
        src = pl.ds(pl.multiple_of(r * CONV_ROWS, CONV_ROWS), CONV_ROWS)
        dst = pl.ds(pl.multiple_of(r * CONV_ROWS + CONV_HALO, CONV_HALO), CONV_ROWS)
        pad_ref[dst, :] = p_ref[src, :].astype(F32)
        return carry

    lax.fori_loop(0, t // CONV_ROWS, fill, 0)
    w0, w1, w2, bias = w_ref[0:1, :], w_ref[1:2, :], w_ref[2:3, :], b_ref[...]
    ext_rows = CONV_ROWS + 2 * CONV_HALO
    inner = slice(CONV_HALO, CONV_HALO + CONV_ROWS)

    def segment(out_ref, row0, n_rows):
        def body(r, carry):
            ext = pad_ref[pl.ds(pl.multiple_of(row0 + r * CONV_ROWS, CONV_HALO), ext_rows), :]
            rid = r * CONV_ROWS + lax.broadcasted_iota(jnp.int32, (CONV_ROWS, 1), 0)
            prev = jnp.where(rid == 0, 0.0, pltpu.roll(ext, 1, 0)[inner])
            nxt = jnp.where(rid == n_rows - 1, 0.0, pltpu.roll(ext, ext_rows - 1, 0)[inner])
            y = prev * w0 + ext[inner] * w1 + nxt * w2 + bias
            out_ref[0, pl.ds(pl.multiple_of(r * CONV_ROWS, CONV_ROWS), CONV_ROWS), :] = y.astype(out_ref.dtype)
            return carry

        lax.fori_loop(0, n_rows // CONV_ROWS, body, 0)

    segment(lat_ref, 0, n_lat)
    segment(ctx_ref, n_lat, n_ctx)


def _shortconv(px, col0, d_hy, conv_w, conv_b, n_lat, n_ctx):
    t = px.shape[0]
    tc = _tile(d_hy, 256, LANES)
    assert col0 % tc == 0 and n_lat % CONV_ROWS == 0 and n_ctx % CONV_ROWS == 0
    per = d_hy // tc
    n_grp = HYENA_ORDER + 1
    return pl.pallas_call(
        functools.partial(_shortconv_kernel, n_lat=n_lat, n_ctx=n_ctx),
        grid=(n_grp, per),
        in_specs=[
            pl.BlockSpec((t, tc), lambda g, j: (0, col0 // tc + g * per + j)),
            pl.BlockSpec((3, tc), lambda g, j: (0, g * per + j)),
            pl.BlockSpec((1, tc), lambda g, j: (0, g * per + j)),
        ],
        out_specs=[
            pl.BlockSpec((1, n_lat, tc), lambda g, j: (g, 0, j)),
            pl.BlockSpec((1, n_ctx, tc), lambda g, j: (g, 0, j)),
        ],
        out_shape=[
            jax.ShapeDtypeStruct((n_grp, n_lat, d_hy), BF16),
            jax.ShapeDtypeStruct((n_grp, n_ctx, d_hy), BF16),
        ],
        scratch_shapes=[pltpu.VMEM((t + 2 * CONV_HALO, tc), F32)],
        compiler_params=_params(("parallel", "parallel")),
        name="shortconv",
    )(px, conv_w, conv_b.reshape(1, -1))


DFT_N2 = 256
K1_ALIGN = 8
STAGE2_K1 = 8
T2_BLOCK = 16
STAGE2_UNROLL = 8


def _dft_constants(length):
    n = 2 * length
    n1 = n // DFT_N2
    assert n1 * DFT_N2 == n and n1 % 2 == 0
    n_t1 = n1 // 2
    k1_used = n1 // 2 + 1
    k1p = -(-k1_used // K1_ALIGN) * K1_ALIGN
    k1 = np.arange(k1p, dtype=np.float64)
    live = (k1 < k1_used).astype(np.float64)
    t1 = np.arange(n1, dtype=np.float64)
    ang1 = -2.0 * np.pi * k1[:, None] * t1[None, :] / n1
    f1 = np.stack([np.cos(ang1), np.sin(ang1)]) * live[None, :, None]
    k2 = np.arange(DFT_N2, dtype=np.float64)
    t2 = np.arange(DFT_N2, dtype=np.float64)
    ang2 = -2.0 * np.pi * t2[None, None, :] * (k1[:, None, None] + n1 * k2[None, :, None]) / n
    gr = np.cos(ang2) * live[:, None, None]
    gi = np.sin(ang2) * live[:, None, None]
    g = np.stack([gr, gi, gr.transpose(0, 2, 1), gi.transpose(0, 2, 1)])
    weight = np.where((k1 == 0) | (k1 == n1 // 2), 1.0, 2.0) * live
    angc = 2.0 * np.pi * np.arange(n_t1, dtype=np.float64)[:, None] * k1[None, :] / n1
    cinv = np.stack([np.cos(angc), np.sin(angc)]) * weight[None, None, :] / n
    as_bf16 = lambda a: jnp.asarray(a.astype(np.float32)).astype(BF16)
    eye = np.eye(T2_BLOCK)
    kron = lambda m: np.stack([np.kron(m[0], eye), np.kron(m[1], eye)])
    return dict(n1=n1, n_t1=n_t1, k1p=k1p, f1=as_bf16(kron(f1)), f1_sig=as_bf16(kron(f1[:, :, :n_t1])),
                g=as_bf16(g), cinv=as_bf16(kron(cinv)))


def _dft_stage1_kernel(f_ref, u_ref, a_ref):
    k, r, tc = u_ref.shape
    u = u_ref[...].reshape(k * r, tc)
    for part in range(2):
        a = jnp.dot(f_ref[part], u, preferred_element_type=F32).astype(a_ref.dtype)
        a_ref[part] = a.reshape(a_ref.shape[1:])


def _dft_stage1(f, u4, group):
    _, k, n2, c = u4.shape
    k1p = f.shape[1] // T2_BLOCK
    tc = _tile(c, 256, LANES)
    return pl.pallas_call(
        _dft_stage1_kernel,
        grid=(n2 // T2_BLOCK, c // tc),
        in_specs=[
            pl.BlockSpec((2, k1p * T2_BLOCK, k * T2_BLOCK), lambda i, j: (0, 0, 0)),
            pl.BlockSpec((None, k, T2_BLOCK, tc), lambda i, j: (group, 0, i, j)),
        ],
        out_specs=pl.BlockSpec((2, k1p, T2_BLOCK, tc), lambda i, j: (0, 0, i, j)),
        out_shape=jax.ShapeDtypeStruct((2, k1p, n2, c), BF16),
        compiler_params=_params(("parallel", "parallel")),
        name="dft_stage1",
    )(f, u4)


def _dft_stage2_filter_kernel(a_ref, g_ref, scale_ref, kf_ref, *, k1c):
    scale = scale_ref[...]

    def body(i, carry):
        ar, ai = a_ref[0, i], a_ref[1, i]
        gr, gi = g_ref[0, i], g_ref[1, i]
        dot = functools.partial(jnp.dot, preferred_element_type=F32)
        kf_ref[0, i] = ((dot(gr, ar) - dot(gi, ai)) * scale).astype(kf_ref.dtype)
        kf_ref[1, i] = ((dot(gr, ai) + dot(gi, ar)) * scale).astype(kf_ref.dtype)
        return carry

    lax.fori_loop(0, k1c, body, 0, unroll=STAGE2_UNROLL)


def _dft_stage2_conv_kernel(a_ref, g_ref, kf_ref, b_ref, *, k1c):
    def body(i, carry):
        ar, ai = a_ref[0, i], a_ref[1, i]
        gr, gi = g_ref[0, i], g_ref[1, i]
        dot = functools.partial(jnp.dot, preferred_element_type=F32)
        xr = dot(gr, ar) - dot(gi, ai)
        xi = dot(gr, ai) + dot(gi, ar)
        kr, ki = kf_ref[0, i].astype(F32), kf_ref[1, i].astype(F32)
        zr =(xr * kr - xi * ki).astype(BF16)
        zi = (xr * ki + xi * kr).astype(BF16)
        grt, git = g_ref[2, i], g_ref[3, i]
        b_ref[0, i] = (dot(grt, zr) + dot(git, zi)).astype(b_ref.dtype)
        b_ref[1, i] = (dot(grt, zi) - dot(git, zr)).astype(b_ref.dtype)
        return carry

    lax.fori_loop(0, k1c, body, 0, unroll=STAGE2_UNROLL)


def _dft_stage2(a4, g, kf=None, scale=None):
    _, k1p, n2, c = a4.shape
    k1c = STAGE2_K1
    tc = _tile(c, 256, LANES)
    a_spec = pl.BlockSpec((2, k1c, n2, tc), lambda i, j: (0, i, 0, j))
    g_spec = pl.BlockSpec((4, k1c, n2, n2), lambda i, j: (0, i, 0, 0))
    if kf is None:
        s_spec = pl.BlockSpec((1, tc), lambda i, j: (0, j))
        body, args, in_specs, dtype = _dft_stage2_filter_kernel, (a4, g, scale), [a_spec, g_spec, s_spec], BF16
    else:
        body, args, in_specs, dtype = _dft_stage2_conv_kernel, (a4, g, kf), [a_spec, g_spec, a_spec], BF16
    return pl.pallas_call(
        functools.partial(body, k1c=k1c),
        grid=(k1p // k1c, c // tc),
        in_specs=in_specs,
        out_specs=a_spec,
        out_shape=jax.ShapeDtypeStruct(a4.shape, dtype),
        compiler_params=_params(("parallel", "arbitrary")),
        name="dft_stage2",
    )(*args)


def _dft_stage3_kernel(c_ref, b_ref, u_ref, m_ref, skip_ref, o_ref):
    _, k1p, r, tc = b_ref.shape
    rows = u_ref.shape[0] * r
    y = (jnp.dot(c_ref[0], b_ref[0].reshape(k1p * r, tc), preferred_element_type=F32)
         - jnp.dot(c_ref[1], b_ref[1].reshape(k1p * r, tc), preferred_element_type=F32))
    u = u_ref[...].reshape(rows, tc).astype(F32)
    out = m_ref[...].reshape(rows, tc).astype(F32) * (y + u * skip_ref[...])
    o_ref[0] = out.astype(o_ref.dtype).reshape(o_ref.shape[1:])


def _dft_stage3(cinv, b4, u4, u_group, m4, m_group, skip):
    _, k1p, n2, c = b4.shape
    n_t1 = cinv.shape[1] // T2_BLOCK
    tc = _tile(c, 256, LANES)
    sig_spec = lambda grp: pl.BlockSpec((None, n_t1, T2_BLOCK, tc), lambda i, j: (grp, 0, i, j))
    return pl.pallas_call(
        _dft_stage3_kernel,
        grid=(n2 // T2_BLOCK, c // tc),
        in_specs=[
            pl.BlockSpec((2, n_t1 * T2_BLOCK, k1p * T2_BLOCK), lambda i, j: (0, 0, 0)),
            pl.BlockSpec((2, k1p, T2_BLOCK, tc), lambda i, j: (0, 0, i, j)),
            sig_spec(u_group),
            sig_spec(m_group),
            pl.BlockSpec((1, tc), lambda i, j: (0, j)),
        ],
        out_specs=pl.BlockSpec((1, n_t1, T2_BLOCK, tc), lambda i, j: (0, 0, i, j)),
        out_shape=jax.ShapeDtypeStruct((1, n_t1, n2, c), BF16),
        compiler_params=_params(("parallel", "parallel")),
        name="dft_stage3",
    )(cinv, b4, u4, m4, skip.reshape(1, c))


def _hyena_latent(p_lat, kern, inv_norm, skip, consts):
    _, length, c = p_lat.shape
    n1, n_t1 = consts["n1"], consts["n_t1"]
    p4 = p_lat.reshape(HYENA_ORDER + 1, n_t1, DFT_N2, c)
    k4 = kern.reshape(HYENA_ORDER, n1, DFT_N2, c)
    u4, u_group = p4, 0
    for order in range(HYENA_ORDER):
        kf = _dft_stage2(_dft_stage1(consts["f1"], k4, order), consts["g"], scale=inv_norm[order:order + 1])
        b = _dft_stage2(_dft_stage1(consts["f1_sig"], u4, u_group), consts["g"], kf)
        u4 = _dft_stage3(consts["cinv"], b, u4, u_group, p4, order + 1, skip[order])
        u_group = 0
    return u4.reshape(length, c)


def _ctx_dft_constants(length):
    n = 2 * length
    k = np.arange(n, dtype=np.float64)
    ang = -2.0 * np.pi * k[:, None] * k[None, :] / n
    fwd = np.stack([np.cos(ang), np.sin(ang)])
    inv = np.stack([np.cos(ang[:length]), -np.sin(ang[:length])]) / n
    as_f32 = lambda a: jnp.asarray(a.astype(np.float32))
    return dict(fwd=as_f32(fwd), fwd_sig=as_f32(fwd[:, :, :length]), inv=as_f32(inv))


def _ctxconv_kernel(ff_ref, fs_ref, fi_ref, k_ref, scale_ref, u_ref, m_ref, skip_ref, o_ref):
    dot = functools.partial(jnp.dot, preferred_element_type=F32, precision=HIGHEST)
    k = k_ref[...] * scale_ref[...]
    u = u_ref[...].astype(F32)
    kr, ki = dot(ff_ref[0], k), dot(ff_ref[1], k)
    xr, xi = dot(fs_ref[0], u), dot(fs_ref[1], u)
    zr = xr * kr - xi * ki
    zi = xr * ki + xi * kr
    y = dot(fi_ref[0], zr) - dot(fi_ref[1], zi)
    o_ref[0] = (m_ref[...].astype(F32) * (y + u * skip_ref[...])).astype(o_ref.dtype)


def _ctxconv(consts, kern, order, inv_norm, u3, u_group, m3, m_group, skip):
    _, length, c = u3.shape
    n = 2 * length
    tc = _tile(c, 256, LANES)
    return pl.pallas_call(
        _ctxconv_kernel,
        grid=(c // tc,),
        in_specs=[
            pl.BlockSpec((2, n, n), lambda j: (0, 0, 0)),
            pl.BlockSpec((2, n, length), lambda j: (0, 0, 0)),
            pl.BlockSpec((2, length, n), lambda j: (0, 0, 0)),
            pl.BlockSpec((None, n, tc), lambda j: (order, 0, j)),
            pl.BlockSpec((1, tc), lambda j: (0, j)),
            pl.BlockSpec((None, length, tc), lambda j: (u_group, 0, j)),
            pl.BlockSpec((None, length, tc), lambda j: (m_group, 0, j)),
            pl.BlockSpec((1, tc), lambda j: (0, j)),
        ],
        out_specs=pl.BlockSpec((1, length, tc), lambda j: (0, 0, j)),
        out_shape=jax.ShapeDtypeStruct((1, length, c), BF16),
        compiler_params=_params(("parallel",)),
        name="ctxconv",
    )(consts["fwd"], consts["fwd_sig"], consts["inv"], kern, inv_norm[order:order + 1], u3, m3, skip.reshape(1, c))


def _hyena_context(p_ctx, kern, inv_norm, skip, consts):
    u3, u_group = p_ctx, 0
    for order in range(HYENA_ORDER):
        u3 = _ctxconv(consts, kern, order, inv_norm, u3, u_group, p_ctx, order + 1, skip[order])
        u_group = 0
    return u3[0]


def kernel(x, c, ctx, c_ctx, norm1_g, norm2_g, mod_a, mod_b, mod_bias, w_in, q_norm_g, k_norm_g, hy_conv_w, hy_conv_b, hy_w1, hy_b1, hy_w2, hy_b2, hy_w3, hy_b3, hy_w4, hy_freq, hy_skip, gm_ln_g, gm_ln_b, gm_ws, gm_bs, gate_up, gate_b, w_br, w_out, rt_group_w, rt_group_b, rt_expert_w, rt_expert_b, ex_w_gate, ex_w_up, ex_w_down, final_norm_g):
    bsz, seq, d = x.shape
    assert bsz == 1 and c.shape[0] == 1
    n_ctx = ctx.shape[1]
    depth = w_in.shape[0]
    t_all = seq + n_ctx
    d_hy = hy_skip.shape[2]
    d_gm = gm_ln_g.shape[1]
    d_at = w_br.shape[1] - d_hy - d_gm
    gate_rank = gate_up.shape[2]
    n_q = d_at // HEAD_DIM
    n_kv = n_q // KV_REP
    q_end = d_at
    v_start = q_end + n_kv * HEAD_DIM
    v_end = v_start + n_kv * HEAD_DIM
    hy_end = v_end + (HYENA_ORDER + 1) * d_hy
    gm_end = hy_end + 2 * d_gm
    n_groups = rt_group_w.shape[2]
    per_group = rt_expert_w.shape[3]
    n_experts = n_groups * per_group
    assert n_groups + n_experts <= ROUTER_COLS
    assert hy_end % (2 * d_gm) == 0 and gm_end % gate_rank == 0 and seq % ROW_CHUNK == 0
    tme = 256

    xs = jnp.concatenate([x[0], ctx[0]], axis=0)
    cond = jnp.stack([c[0], c_ctx], axis=0)
    cos, sin = _rope_tables(seq, n_ctx)

    tq = _tile(seq, 512, 16)
    tk = _tile(t_all, 2816, 256)
    tq_c = _tile(n_ctx, 256, 16)
    dft_lat = _dft_constants(seq)
    dft_ctx = _ctx_dft_constants(n_ctx)
    w_in_b, w_br_b, gate_up_b, w_out_b = (w.astype(BF16) for w in (w_in, w_br, gate_up, w_out))

    for l in range(depth):
        m = (jnp.dot(jnp.dot(jax.nn.silu(cond), mod_a[l], precision=HIGHEST), mod_b[l], precision=HIGHEST)
             + mod_bias[l]).reshape(2, N_MOD, d)

        px = _modproj(xs, norm1_g[l], m[:, 0], m[:, 1], w_in_b, l, seq)
        q, k = _qkprep(px, cos, sin, q_norm_g[l], k_norm_g[l], n_q, n_kv)
        o_at = _attention(q, k, px, v_start // HEAD_DIM, 0, seq, 0, t_all, tq, tk)
        o_at = _attention(q, k, px, v_start // HEAD_DIM, seq, n_ctx, seq, n_ctx, tq_c, n_ctx, out=o_at)

        hy_w = (hy_w1[l], hy_b1[l], hy_w2[l], hy_b2[l], hy_w3[l], hy_b3[l], hy_w4[l], hy_freq[l])
        p_lat, p_ctx = _shortconv(px, v_end, d_hy, hy_conv_w[l], hy_conv_b[l], seq, n_ctx)
        kern_x, norm_x = _hyena_filters(seq, *hy_w, d_hy, BF16)
        kern_c, norm_c = _hyena_filters(n_ctx, *hy_w, d_hy, F32)
        o_hy_x = _hyena_latent(p_lat, kern_x, 1.0 / norm_x, hy_skip[l], dft_lat)
        o_hy_c = _hyena_context(p_ctx, kern_c, 1.0 / norm_c, hy_skip[l], dft_ctx)
        o_hy = jnp.concatenate([o_hy_x, o_hy_c], axis=0)

        o_gm = _gmlp(px, hy_end // (2 * d_gm), d_gm, gm_ln_g[l], gm_ln_b[l], gm_ws[l], gm_bs[l])

        merged = _merge(o_hy, o_at, o_gm, px, gm_end // gate_rank, gate_rank, w_br_b, gate_up_b, gate_b, l)
        xs = _outproj(merged, w_out_b, l, xs, m[:, 2], seq)

        rw = jnp.concatenate([rt_group_w[l]] + [rt_expert_w[l, g] for g in range(n_groups)], axis=1)
        rw = jnp.pad(rw, ((0, 0), (0, ROUTER_COLS - rw.shape[1])))
        rb = jnp.concatenate([rt_group_b[l], rt_expert_b[l].reshape(-1)])
        rb = jnp.pad(rb, (0, ROUTER_COLS - rb.shape[0])).reshape(1, ROUTER_COLS)
        h2, logits = _router(xs, norm2_g[l], m[:, 3], m[:, 4], _split_bf16(rw, 3), rb, seq)
        eid, wts = _route(logits, n_groups, per_group)
        src_tok, roww, pos, tiles = _dispatch(eid, wts, n_experts, tme)
        take_rows = lambda a, idx: a.at[idx].get(mode="promise_in_bounds")
        xg = take_rows(h2, src_tok)
        yo = _experts(xg, roww, tiles, ex_w_gate, ex_w_up, ex_w_down, l, tme)
        xs = _combine(xs, take_rows(yo, pos[:, 0]), take_rows(yo, pos[:, 1]), m[:, 5], seq)

    return _final_norm(xs, final_norm_g, seq).reshape(1, seq, d)
```

```python
import functools
import math

import numpy as np
import jax
import jax.numpy as jnp
from jax import lax
from jax.experimental import pallas as pl
from jax.experimental.pallas import tpu as pltpu

F32 = jnp.float32
BF16 = jnp.bfloat16
HIGHEST = lax.Precision.HIGHEST

HEAD_DIM = 128
KV_REP = 4
GRID_W = 64
ROPE_THETA = 10000.0
ATTN_SCALE = HEAD_DIM ** -0.5
LOG2_E = math.log2(math.e)
HYENA_ORDER = 2
HYENA_SHIFT = 0.05
HYENA_FAST_PCT = 0.3
HYENA_SLOW_PCT = 1.5
HYENA_TARGET = 1e-2
GMLP_CHUNK = 128
GMLP_GROUP_W = 128
N_BRANCH = 3
N_MOD = 6
TOP_K = 2
EPS = 1e-6
LN_EPS = 1e-5

LANES = 128
ROUTER_COLS = LANES
VMEM_LIMIT_MB = 56


def _params(semantics, vmem_mb=VMEM_LIMIT_MB):
    return pltpu.CompilerParams(dimension_semantics=semantics, vmem_limit_bytes=vmem_mb * 1024 * 1024)


def _tile(n, target, mult):
    best = None
    for d in range(mult, min(n, target) + 1, mult):
        if n % d == 0:
            best = d
    assert best is not None, (n, target, mult)
    return best


ROW_CHUNK = 16


def _modulate_rows(x_ref, g_ref, sh_ref, sc_ref, h_ref, row0, n_lat, n_rows):
    n_chunks = n_rows // ROW_CHUNK
    lat_chunks = jnp.clip((n_lat - row0) // ROW_CHUNK, 0, n_chunks)
    unroll = next(u for u in (4, 3, 2, 1) if n_chunks % u == 0)

    def segment(seg, lo, hi, **loop_kw):
        gain = g_ref[...] * (1.0 + sc_ref[seg:seg + 1, :])
        shift = sh_ref[seg:seg + 1, :]

        def body(r, carry):
            rows = pl.ds(pl.multiple_of(r * ROW_CHUNK, ROW_CHUNK), ROW_CHUNK)
            xf = x_ref[rows, :]
            ms = jnp.mean(xf * xf, axis=-1, keepdims=True)
            h_ref[rows, :] = (xf * lax.rsqrt(ms + EPS) * gain + shift).astype(h_ref.dtype)
            return carry

        lax.fori_loop(lo, hi, body, 0, **loop_kw)

    @pl.when(lat_chunks == n_chunks)
    def _():
        segment(0, 0, n_chunks, unroll=unroll)

    @pl.when(lat_chunks != n_chunks)
    def _():
        segment(0, 0, lat_chunks)
        segment(1, lat_chunks, n_chunks)


def _modproj_kernel(x_ref, g_ref, sh_ref, sc_ref, w_ref, o_ref, h_ref, *, n_lat, tm):
    i = pl.program_id(0)

    @pl.when(pl.program_id(1) == 0)
    def _():
        _modulate_rows(x_ref, g_ref, sh_ref, sc_ref, h_ref, i * tm, n_lat, tm)

    o_ref[...] = jnp.dot(h_ref[...], w_ref[...], preferred_element_type=F32).astype(o_ref.dtype)


def _modproj(x, g, sh, sc, w_all, layer, n_lat):
    t, d = x.shape
    n = w_all.shape[2]
    tm = _tile(t, 528, 16)
    tn = _tile(n, 512, LANES)
    return pl.pallas_call(
        functools.partial(_modproj_kernel, n_lat=n_lat, tm=tm),
        grid=(t // tm, n // tn),
        in_specs=[
            pl.BlockSpec((tm, d), lambda i, j: (i, 0)),
            pl.BlockSpec((1, d), lambda i, j: (0, 0)),
            pl.BlockSpec((2, d), lambda i, j: (0, 0)),
            pl.BlockSpec((2, d), lambda i, j: (0, 0)),
            pl.BlockSpec((None, d, tn), lambda i, j: (layer, 0, j)),
        ],
        out_specs=pl.BlockSpec((tm, tn), lambda i, j: (i, j)),
        out_shape=jax.ShapeDtypeStruct((t, n), BF16),
        scratch_shapes=[pltpu.VMEM((tm, d), BF16)],
        compiler_params=_params(("parallel", "arbitrary")),
        name="modproj",
    )(x, g.reshape(1, d), sh, sc, w_all)


def _qkprep_kernel(p_ref, cos_ref, sin_ref, qg_ref, kg_ref, q_ref, k_ref, *, n_q, n_kv):
    cos = cos_ref[...]
    sin = sin_ref[...]
    lane = lax.broadcasted_iota(jnp.int32, (1, HEAD_DIM), 1)
    first_half = (lane % (HEAD_DIM // 2)) < (HEAD_DIM // 4)

    def head(h, gain, scale):
        xh = p_ref[:, h * HEAD_DIM:(h + 1) * HEAD_DIM].astype(F32)
        ms = jnp.mean(xh * xh, axis=-1, keepdims=True)
        y = xh * lax.rsqrt(ms + EPS) * gain
        partner = jnp.where(first_half,
                            pltpu.roll(y, HEAD_DIM - HEAD_DIM // 4, 1),
                            pltpu.roll(y, HEAD_DIM // 4, 1))
        return (y * cos + partner * sin) * scale

    for h in range(n_q):
        q_ref[h] = head(h, qg_ref[...], ATTN_SCALE * LOG2_E).astype(q_ref.dtype)
    for h in range(n_kv):
        k_ref[h] = head(n_q + h, kg_ref[...], 1.0).astype(k_ref.dtype)


def _qkprep(px, cos, sin, qg, kg, n_q, n_kv):
    t = px.shape[0]
    tr = _tile(t, 264, 8)
    width = (n_q + n_kv) * HEAD_DIM
    return pl.pallas_call(
        functools.partial(_qkprep_kernel, n_q=n_q, n_kv=n_kv),
        grid=(t // tr,),
        in_specs=[
            pl.BlockSpec((tr, width), lambda i: (i, 0)),
            pl.BlockSpec((tr, HEAD_DIM), lambda i: (i, 0)),
            pl.BlockSpec((tr, HEAD_DIM), lambda i: (i, 0)),
            pl.BlockSpec((1, HEAD_DIM), lambda i: (0, 0)),
            pl.BlockSpec((1, HEAD_DIM), lambda i: (0, 0)),
        ],
        out_specs=[
            pl.BlockSpec((n_q, tr, HEAD_DIM), lambda i: (0, i, 0)),
            pl.BlockSpec((n_kv, tr, HEAD_DIM), lambda i: (0, i, 0)),
        ],
        out_shape=[
            jax.ShapeDtypeStruct((n_q, t, HEAD_DIM), BF16),
            jax.ShapeDtypeStruct((n_kv, t, HEAD_DIM), BF16),
        ],
        compiler_params=_params(("parallel",)),
        name="qkprep",
    )(px, cos, sin, qg.reshape(1, HEAD_DIM), kg.reshape(1, HEAD_DIM))


def _rope_tables(n_lat, n_ctx):
    n = HEAD_DIM // 4
    inv = ROPE_THETA ** (-np.arange(n, dtype=np.float32) / n)
    t = np.arange(n_lat)
    rows = (t // GRID_W).astype(np.float32)
    cols = (t % GRID_W).astype(np.float32)
    ang = np.concatenate([rows[:, None] * inv[None, :]] * 2 + [cols[:, None] * inv[None, :]] * 2, axis=1)
    ang = jnp.asarray(ang.astype(np.float32))
    sign = np.tile(np.concatenate([-np.ones(n, np.float32), np.ones(n, np.float32)]), 2)
    cos = jnp.concatenate([jnp.cos(ang), jnp.ones((n_ctx, HEAD_DIM), F32)], axis=0)
    sin = jnp.concatenate([jnp.sin(ang) * sign[None, :], jnp.zeros((n_ctx, HEAD_DIM), F32)], axis=0)
    return cos, sin


def _attn_kernel(*refs, tq, tk, n_kv, aliased):
    if aliased:
        q_ref, qn_ref, k_ref, v_ref, _, o_ref, s_ref, m_ref, l_ref, acc_ref = refs
    else:
        q_ref, qn_ref, k_ref, v_ref, o_ref, s_ref, m_ref, l_ref, acc_ref = refs
    n_chunks = n_kv // tk
    assert (n_chunks * KV_REP) % 2 == 0
    m_ref[...] = jnp.full(m_ref.shape, -jnp.inf, F32)
    l_ref[...] = jnp.zeros(l_ref.shape, F32)
    acc_ref[...] = jnp.zeros(acc_ref.shape, F32)

    def scores(q, ks):
        return lax.dot_general(q, k_ref[0, ks, :], (((1,), (1,)), ((), ())), preferred_element_type=F32)

    @pl.when(pl.program_id(1) == 0)
    def _():
        s_ref[0] = scores(q_ref[0], pl.ds(0, tk))

    def body(c, carry):
        ks = pl.ds(pl.multiple_of(c * tk, tk), tk)
        last = c + 1 == n_chunks
        kn = pl.ds(pl.multiple_of(jnp.where(last, 0, c + 1) * tk, tk), tk)
        v = v_ref[ks, :]
        for r in range(KV_REP):
            cur, nxt = r % 2, (r + 1) % 2
            if r + 1 < KV_REP:
                s_ref[nxt] = scores(q_ref[r + 1], ks)
            else:
                s_ref[nxt] = scores(jnp.where(last, qn_ref[0], q_ref[0]), kn)
            s = s_ref[cur]
            m_prev = m_ref[r]
            m_new = jnp.maximum(m_prev, jnp.max(s, axis=-1, keepdims=True))
            alpha = jnp.exp2(m_prev - m_new)
            p = jnp.exp2(s - m_new)
            l_ref[r] = alpha * l_ref[r] + jnp.sum(p, axis=-1, keepdims=True)
            acc_ref[r] = alpha * acc_ref[r] + jnp.dot(p.astype(BF16), v, preferred_element_type=F32)
            m_ref[r] = m_new
        return carry

    lax.fori_loop(0, n_chunks, body, 0)
    for r in range(KV_REP):
        o_ref[:, r * HEAD_DIM:(r + 1) * HEAD_DIM] = (acc_ref[r] * (1.0 / l_ref[r])).astype(o_ref.dtype)


def _attention(q, k, px, v_col_block, q_row0, n_q_rows, kv_row0, n_kv, tq, tk, out=None):
    n_heads, t, _ = q.shape
    n_kv_heads = k.shape[0]
    assert q_row0 % tq == 0 and n_q_rows % tq == 0 and kv_row0 % n_kv == 0 and n_kv % tk == 0
    qb0 = q_row0 // tq
    kb0 = kv_row0 // n_kv
    aliased = out is not None
    n_tiles = n_q_rows // tq
    in_specs = [
        pl.BlockSpec((KV_REP, tq, HEAD_DIM), lambda g, i: (g, qb0 + i, 0)),
        pl.BlockSpec((1, tq, HEAD_DIM), lambda g, i: (g * KV_REP, qb0 + jnp.minimum(i + 1, n_tiles - 1), 0)),
        pl.BlockSpec((1, n_kv, HEAD_DIM), lambda g, i: (g, kb0, 0)),
        pl.BlockSpec((n_kv, HEAD_DIM), lambda g, i: (kb0, v_col_block + g)),
    ]
    args = [q, q, k, px]
    if aliased:
        in_specs.append(pl.BlockSpec(memory_space=pl.ANY))
        args.append(out)
    return pl.pallas_call(
        functools.partial(_attn_kernel, tq=tq, tk=tk, n_kv=n_kv, aliased=aliased),
        grid=(n_kv_heads, n_q_rows // tq),
        in_specs=in_specs,
        out_specs=pl.BlockSpec((tq, KV_REP * HEAD_DIM), lambda g, i: (qb0 + i, g)),
        out_shape=jax.ShapeDtypeStruct((t, n_heads * HEAD_DIM), BF16),
        scratch_shapes=[
            pltpu.VMEM((2, tq, tk), F32),
            pltpu.VMEM((KV_REP, tq, 1), F32),
            pltpu.VMEM((KV_REP, tq, 1), F32),
            pltpu.VMEM((KV_REP, tq, HEAD_DIM), F32),
        ],
        input_output_aliases={4: 0} if aliased else {},
        compiler_params=_params(("parallel", "arbitrary")),
        name="attention",
    )(*args)


def _gmlp_kernel(p_ref, lng_ref, lnb_ref, ws_ref, bst_ref, o_ref, *, n_chunks, d_g):
    n_groups = d_g // GMLP_GROUP_W
    for b in range(n_chunks):
        rows = slice(b * GMLP_CHUNK, (b + 1) * GMLP_CHUNK)
        uv = jax.nn.gelu(p_ref[rows, :].astype(F32))
        u = uv[:, :d_g]
        v = uv[:, d_g:]
        vc = v - jnp.mean(v, axis=-1, keepdims=True)
        var = jnp.mean(vc * vc, axis=-1, keepdims=True)
        vn = (vc * lax.rsqrt(var + LN_EPS) * lng_ref[...] + lnb_ref[...]).astype(BF16)
        for g in range(n_groups):
            cols = slice(g * GMLP_GROUP_W, (g + 1) * GMLP_GROUP_W)
            vm = jnp.dot(ws_ref[g].astype(BF16), vn[:, cols], preferred_element_type=F32) + bst_ref[:, g:g + 1]
            o_ref[rows, cols] = (u[:, cols] * vm).astype(o_ref.dtype)


def _gmlp(px, col_block, d_g, ln_g, ln_b, ws, bs):
    t = px.shape[0]
    n_chunks = 2
    tr = n_chunks * GMLP_CHUNK
    assert t % tr == 0
    n_groups = d_g // GMLP_GROUP_W
    return pl.pallas_call(
        functools.partial(_gmlp_kernel, n_chunks=n_chunks, d_g=d_g),
        grid=(t // tr,),
        in_specs=[
            pl.BlockSpec((tr, 2 * d_g), lambda i: (i, col_block)),
            pl.BlockSpec((1, d_g), lambda i: (0, 0)),
            pl.BlockSpec((1, d_g), lambda i: (0, 0)),
            pl.BlockSpec((n_groups, GMLP_CHUNK, GMLP_CHUNK), lambda i: (0, 0, 0)),
            pl.BlockSpec((GMLP_CHUNK, n_groups), lambda i: (0, 0)),
        ],
        out_specs=pl.BlockSpec((tr, d_g), lambda i: (i, 0)),
        out_shape=jax.ShapeDtypeStruct((t, d_g), BF16),
        compiler_params=_params(("parallel",)),
        name="gmlp",
    )(px, ln_g.reshape(1, d_g), ln_b.reshape(1, d_g), ws, bs.T)


def _merge_kernel(hy_ref, at_ref, gm_ref, gc_ref, wbr_ref, gup_ref, gb_ref, o_ref, *, d_hy, d_at):
    gc = gc_ref[...]
    bounds = (0, d_hy, d_hy + d_at, wbr_ref.shape[0])
    merged = None
    for b, a_ref in enumerate((hy_ref, at_ref, gm_ref)):
        y = jnp.dot(a_ref[...], wbr_ref[bounds[b]:bounds[b + 1], :], preferred_element_type=F32)
        gate = jax.nn.sigmoid(jnp.dot(gc, gup_ref[b], preferred_element_type=F32) + gb_ref[b:b + 1, :])
        merged = gate * y if merged is None else merged + gate * y
    o_ref[...] = merged.astype(o_ref.dtype)


def _merge(o_hy, o_at, o_gm, px, gc_col_block, gate_rank, w_br, gate_up, gate_b, layer):
    t = px.shape[0]
    d_hy, d_at, d_gm = o_hy.shape[1], o_at.shape[1], o_gm.shape[1]
    d = w_br.shape[2]
    tm = _tile(t, 528, 16)
    tn = _tile(d, 1024, LANES)
    return pl.pallas_call(
        functools.partial(_merge_kernel, d_hy=d_hy, d_at=d_at),
        grid=(t // tm, d // tn),
        in_specs=[
            pl.BlockSpec((tm, d_hy), lambda i, j: (i, 0)),
            pl.BlockSpec((tm, d_at), lambda i, j: (i, 0)),
            pl.BlockSpec((tm, d_gm), lambda i, j: (i, 0)),
            pl.BlockSpec((tm, gate_rank), lambda i, j: (i, gc_col_block)),
            pl.BlockSpec((None, d_hy + d_at + d_gm, tn), lambda i, j: (layer, 0, j)),
            pl.BlockSpec((None, N_BRANCH, gate_rank, tn), lambda i, j: (layer, 0, 0, j)),
            pl.BlockSpec((None, N_BRANCH, tn), lambda i, j: (layer, 0, j)),
        ],
        out_specs=pl.BlockSpec((tm, tn), lambda i, j: (i, j)),
        out_shape=jax.ShapeDtypeStruct((t, d), BF16),
        compiler_params=_params(("parallel", "arbitrary")),
        name="merge",
    )(o_hy, o_at, o_gm, px, w_br, gate_up, gate_b)


def _outproj_kernel(a_ref, w_ref, x_ref, gate_ref, o_ref, *, n_lat, tm):
    y = jnp.dot(a_ref[...], w_ref[...], preferred_element_type=F32)
    rid = pl.program_id(0) * tm + lax.broadcasted_iota(jnp.int32, (tm, 1), 0)
    gate = jnp.where(rid >= n_lat, gate_ref[1:2, :], gate_ref[0:1, :])
    o_ref[...] = x_ref[...] + gate * y


def _outproj(a, w_all, layer, x, gate, n_lat):
    t, k = a.shape
    d = w_all.shape[2]
    tm = _tile(t, 528, 16)
    tn = _tile(d, 1024, LANES)
    return pl.pallas_call(
        functools.partial(_outproj_kernel, n_lat=n_lat, tm=tm),
        grid=(t // tm, d // tn),
        in_specs=[
            pl.BlockSpec((tm, k), lambda i, j: (i, 0)),
            pl.BlockSpec((None, k, tn), lambda i, j: (layer, 0, j)),
            pl.BlockSpec((tm, tn), lambda i, j: (i, j)),
            pl.BlockSpec((2, tn), lambda i, j: (0, j)),
        ],
        out_specs=pl.BlockSpec((tm, tn), lambda i, j: (i, j)),
        out_shape=jax.ShapeDtypeStruct((t, d), F32),
        compiler_params=_params(("parallel", "arbitrary")),
        name="outproj",
    )(a, w_all, x, gate)


def _split_bf16(w, parts):
    out = []
    rest = w.astype(F32)
    for _ in range(parts):
        piece = rest.astype(BF16)
        out.append(piece)
        rest = rest - piece.astype(F32)
    return jnp.stack(out)


def _router_kernel(x_ref, g_ref, sh_ref, sc_ref, rw_ref, rb_ref, h_ref, lg_ref, *, n_lat, tm):
    _modulate_rows(x_ref, g_ref, sh_ref, sc_ref, h_ref, pl.program_id(0) * tm, n_lat, tm)
    h = h_ref[...]
    logits = rb_ref[...]
    for part in range(rw_ref.shape[0]):
        logits = logits + jnp.dot(h, rw_ref[part], preferred_element_type=F32)
    lg_ref[...] = logits


def _router(x, g, sh, sc, rw, rb, n_lat):
    t, d = x.shape
    tm = _tile(t, 264, 16)
    return pl.pallas_call(
        functools.partial(_router_kernel, n_lat=n_lat, tm=tm),
        grid=(t // tm,),
        in_specs=[
            pl.BlockSpec((tm, d), lambda i: (i, 0)),
            pl.BlockSpec((1, d), lambda i: (0, 0)),
            pl.BlockSpec((2, d), lambda i: (0, 0)),
            pl.BlockSpec((2, d), lambda i: (0, 0)),
            pl.BlockSpec((rw.shape[0], d, ROUTER_COLS), lambda i: (0, 0, 0)),
            pl.BlockSpec((1, ROUTER_COLS), lambda i: (0, 0)),
        ],
        out_specs=[
            pl.BlockSpec((tm, d), lambda i: (i, 0)),
            pl.BlockSpec((tm, ROUTER_COLS), lambda i: (i, 0)),
        ],
        out_shape=[
            jax.ShapeDtypeStruct((t, d), BF16),
            jax.ShapeDtypeStruct((t, ROUTER_COLS), F32),
        ],
        compiler_params=_params(("parallel",)),
        name="router",
    )(x, g.reshape(1, d), sh, sc, rw, rb)


def _experts_kernel(te_ref, tb_ref, tf_ref, ts_ref, ne_ref, nu_ref, x_ref, rw_ref, wg_hbm, wu_hbm, wd_hbm, o_ref,
                    wg_f, wu_f, wd_f, wg_s, wu_s, wd_s, sem, *, layer):
    i = pl.program_id(0)

    def weight_copies(expert, slot):
        return [pltpu.make_async_copy(hbm.at[layer, expert], stage.at[slot], sem.at[slot, n])
                for n, (hbm, stage) in enumerate(((wg_hbm, wg_f), (wu_hbm, wu_f), (wd_hbm, wd_f)))]

    @pl.when(i < nu_ref[0])
    def _():
        slot = ts_ref[i]

        @pl.when(tf_ref[i] == 1)
        def _():
            @pl.when(i == 0)
            def _():
                for cp in weight_copies(te_ref[0], 0):
                    cp.start()

            for cp in weight_copies(te_ref[i], slot):
                cp.wait()
            wg_s[...] = wg_f[slot].astype(BF16)
            wu_s[...] = wu_f[slot].astype(BF16)
            wd_s[...] = wd_f[slot].astype(BF16)

            @pl.when(ne_ref[i] >= 0)
            def _():
                for cp in weight_copies(ne_ref[i], 1 - slot):
                    cp.start()

        xg = x_ref[...]
        hg = jnp.dot(xg, wg_s[...], preferred_element_type=F32)
        hu = jnp.dot(xg, wu_s[...], preferred_element_type=F32)
        hid = (jax.nn.silu(hg) * hu * rw_ref[...]).astype(BF16)
        o_ref[...] = jnp.dot(hid, wd_s[...], preferred_element_type=F32).astype(o_ref.dtype)

    @pl.when(i >= nu_ref[0])
    def _():
        o_ref[...] = jnp.zeros(o_ref.shape, o_ref.dtype)


def _experts(xg, roww, tiles, w_gate, w_up, w_down, layer, tme):
    p, d = xg.shape
    f = w_gate.shape[3]
    n_tiles = p // tme
    row_block = lambda i, te, tb, tf, ts, ne, nu: (tb[i], 0)
    grid_spec = pltpu.PrefetchScalarGridSpec(
        num_scalar_prefetch=6,
        grid=(n_tiles,),
        in_specs=[
            pl.BlockSpec((tme, d), row_block),
            pl.BlockSpec((tme, 1), row_block),
            pl.BlockSpec(memory_space=pl.ANY),
            pl.BlockSpec(memory_space=pl.ANY),
            pl.BlockSpec(memory_space=pl.ANY),
        ],
        out_specs=pl.BlockSpec((tme, d), lambda i, te, tb, tf, ts, ne, nu: (i, 0)),
        scratch_shapes=[
            pltpu.VMEM((2, d, f), F32),
            pltpu.VMEM((2, d, f), F32),
            pltpu.VMEM((2, f, d), F32),
            pltpu.VMEM((d, f), BF16),
            pltpu.VMEM((d, f), BF16),
            pltpu.VMEM((f, d), BF16),
            pltpu.SemaphoreType.DMA((2, 3)),
        ],
    )
    return pl.pallas_call(
        functools.partial(_experts_kernel, layer=layer),
        grid_spec=grid_spec,
        out_shape=jax.ShapeDtypeStruct((p, d), BF16),
        compiler_params=_params(("arbitrary",)),
        name="experts",
    )(*tiles, xg, roww, w_gate, w_up, w_down)


def _combine_kernel(x_ref, y0_ref, y1_ref, gate_ref, o_ref, *, n_lat, tm):
    rid = pl.program_id(0) * tm + lax.broadcasted_iota(jnp.int32, (tm, 1), 0)
    gate = jnp.where(rid >= n_lat, gate_ref[1:2, :], gate_ref[0:1, :])
    o_ref[...] = x_ref[...] + gate * (y0_ref[...].astype(F32) + y1_ref[...].astype(F32))


def _combine(x, y0, y1, gate, n_lat):
    t, d = x.shape
    tm = _tile(t, 264, 16)
    row = lambda: pl.BlockSpec((tm, d), lambda i: (i, 0))
    return pl.pallas_call(
        functools.partial(_combine_kernel, n_lat=n_lat, tm=tm),
        grid=(t // tm,),
        in_specs=[row(), row(), row(), pl.BlockSpec((2, d), lambda i: (0, 0))],
        out_specs=row(),
        out_shape=jax.ShapeDtypeStruct((t, d), F32),
        compiler_params=_params(("parallel",)),
        name="moe_combine",
    )(x, y0, y1, gate)


def _route(logits, n_groups, per_group):
    lg = logits[:, :n_groups]
    le_all = logits[:, n_groups:n_groups + n_groups * per_group].reshape(-1, n_groups, per_group)
    grp = jnp.argmax(lg, axis=-1)
    p_grp = jnp.take_along_axis(jax.nn.softmax(lg, axis=-1), grp[:, None], axis=-1)
    le = jnp.take_along_axis(le_all, grp[:, None, None], axis=1)[:, 0]
    top_p, top_i = lax.top_k(jax.nn.softmax(le, axis=-1), TOP_K)
    w = top_p / jnp.sum(top_p, axis=-1, keepdims=True) * p_grp
    eid = grp[:, None] * per_group + top_i
    return eid.astype(jnp.int32), w


def _dispatch(eid, w, n_experts, tme):
    t = eid.shape[0]
    n_pairs = t * TOP_K
    p_rows = n_pairs + n_experts * tme
    p_rows = -(-p_rows // tme) * tme
    flat_e = eid.reshape(-1)
    onehot = (flat_e[:, None] == jnp.arange(n_experts, dtype=jnp.int32)[None, :]).astype(jnp.int32)
    csum = jnp.cumsum(onehot, axis=0)
    rank = jnp.take_along_axis(csum, flat_e[:, None], axis=1)[:, 0] - 1
    counts = csum[-1]
    padded = -(-counts // tme) * tme
    ends = jnp.cumsum(padded)
    starts = ends - padded
    dest = starts[flat_e] + rank
    assert t < 2 ** 24
    tok = (jnp.arange(n_pairs, dtype=jnp.int32) // TOP_K).astype(F32)
    slots = jnp.full((p_rows, 2), -1.0, F32).at[dest].set(jnp.stack([tok, w.reshape(-1)], axis=1))
    is_pad = slots[:, 0] < 0
    src_tok = jnp.where(is_pad, jnp.arange(p_rows, dtype=jnp.int32) % t, slots[:, 0].astype(jnp.int32))
    roww = jnp.where(is_pad, 0.0, slots[:, 1])
    n_used = (ends[-1] // tme).astype(jnp.int32)
    tile_ids = jnp.arange(p_rows // tme, dtype=jnp.int32)
    tile_block = jnp.minimum(tile_ids, n_used - 1)
    tile_expert = jnp.minimum(
        jnp.searchsorted(ends, tile_block * tme, side="right").astype(jnp.int32), n_experts - 1)
    tile_first = jnp.logical_and(tile_block * tme == starts[tile_expert], tile_ids < n_used).astype(jnp.int32)
    ordinal = jnp.maximum(jnp.cumsum(tile_first) - 1, 0)
    used = counts > 0
    used_experts = jnp.argsort(jnp.logical_not(used), stable=True).astype(jnp.int32)
    next_expert = jnp.where(ordinal + 1 < jnp.sum(used),
                            used_experts[jnp.minimum(ordinal + 1, n_experts - 1)], -1).astype(jnp.int32)
    tiles = (tile_expert, tile_block, tile_first, (ordinal % 2).astype(jnp.int32), next_expert, n_used.reshape(1))
    return src_tok, roww.reshape(-1, 1), dest.reshape(t, TOP_K), tiles


def _rmsnorm_kernel(x_ref, g_ref, o_ref):
    xf = x_ref[...]
    ms = jnp.mean(xf * xf, axis=-1, keepdims=True)
    o_ref[...] = xf * lax.rsqrt(ms + EPS) * g_ref[...]


def _final_norm(x, g, n_rows):
    d = x.shape[1]
    tr = _tile(n_rows, 256, 8)
    return pl.pallas_call(
        _rmsnorm_kernel,
        grid=(n_rows // tr,),
        in_specs=[pl.BlockSpec((tr, d), lambda i: (i, 0)), pl.BlockSpec((1, d), lambda i: (0, 0))],
        out_specs=pl.BlockSpec((tr, d), lambda i: (i, 0)),
        out_shape=jax.ShapeDtypeStruct((n_rows, d), F32),
        compiler_params=_params(("parallel",)),
        name="final_norm",
    )(x, g.reshape(1, d))


SUBLANES = 8


def _filter_features(length, emb):
    bands = (emb - 1) // 2
    i = np.arange(length, dtype=np.float64)
    t = i / (length - 1)
    f = np.linspace(1e-4, bands - 1, bands)
    w = 2.0 * np.pi * i / length
    z = np.concatenate([t[:, None], np.cos(-f[None, :] * w[:, None]), np.sin(-f[None, :] * w[:, None])], axis=1)
    z = np.concatenate([z, z[::-1]], axis=0)
    emb_pad = -(-emb // SUBLANES) * SUBLANES
    z = np.pad(z, ((0, 0), (0, emb_pad - emb)))
    return jnp.asarray(z.astype(np.float32))


def _filter_kernel(zt_ref, t_ref, w1_ref, b1_ref, w2_ref, b2_ref, w3_ref, b3_ref, w4_ref, fr_ref, dl_ref,
                   k_ref, nrm_ref, *, d_hy):
    dot = functools.partial(jnp.dot, preferred_element_type=F32, precision=HIGHEST)

    @pl.when(jnp.logical_and(pl.program_id(0) == 0, pl.program_id(1) == 0))
    def _():
        nrm_ref[...] = jnp.zeros(nrm_ref.shape, F32)

    fr = fr_ref[...]
    h = jnp.sin(fr * (dot(w1_ref[...], zt_ref[...]) + b1_ref[...]))
    h = jnp.sin(fr * (dot(w2_ref[...], h) + b2_ref[...]))
    h = jnp.sin(fr * (dot(w3_ref[...], h) + b3_ref[...]))
    h_hi = h.astype(BF16)
    h_lo = (h - h_hi.astype(F32)).astype(BF16)
    tdot = lambda a, b: lax.dot_general(a, b, (((0,), (0,)), ((), ())), preferred_element_type=F32)
    hk = tdot(h_hi, w4_ref[0]) + tdot(h_lo, w4_ref[0]) + tdot(h_hi, w4_ref[1])
    window = jnp.exp(-t_ref[...] * dl_ref[...]) + HYENA_SHIFT
    for o in range(HYENA_ORDER):
        k = hk[:, o * d_hy:(o + 1) * d_hy] * window
        k_ref[o] = k.astype(k_ref.dtype)
        nrm_ref[o:o + 1, :] += jnp.sum(jnp.abs(k), axis=0, keepdims=True)


def _hyena_filters(length, w1, b1, w2, b2, w3, b3, w4, freq, d_hy, out_dtype):
    emb, hidden = w1.shape
    z = _filter_features(length, emb)
    emb_pad = z.shape[1]
    w1p = jnp.pad(w1, ((0, emb_pad - emb), (0, 0)))
    max_decay = math.log(HYENA_TARGET) / HYENA_FAST_PCT
    min_decay = math.log(HYENA_TARGET) / HYENA_SLOW_PCT
    deltas = jnp.asarray(np.abs(np.linspace(min_decay, max_decay, d_hy)).astype(np.float32)).reshape(1, d_hy)
    tr = _tile(length, 512, SUBLANES)
    per = length // tr
    vec = lambda a: a.reshape(hidden, 1)
    full = lambda shape: pl.BlockSpec(shape, lambda d, i: (0,) * len(shape))
    return pl.pallas_call(
        functools.partial(_filter_kernel, d_hy=d_hy),
        grid=(2, per),
        in_specs=[
            pl.BlockSpec((emb_pad, tr), lambda d, i: (0, d * per + i)),
            pl.BlockSpec((tr, 1), lambda d, i: (d * per + i, 0)),
            full((hidden, emb_pad)), full((hidden, 1)),
            full((hidden, hidden)), full((hidden, 1)),
            full((hidden, hidden)), full((hidden, 1)),
            pl.BlockSpec((2, hidden, HYENA_ORDER * d_hy), lambda d, i: (0, 0, d)),
            full((hidden, 1)), full((1, d_hy)),
        ],
        out_specs=[
            pl.BlockSpec((HYENA_ORDER, tr, d_hy), lambda d, i: (0, d * per + i, 0)),
            pl.BlockSpec((HYENA_ORDER, d_hy), lambda d, i: (0, 0)),
        ],
        out_shape=[
            jax.ShapeDtypeStruct((HYENA_ORDER, 2 * length, d_hy), out_dtype),
            jax.ShapeDtypeStruct((HYENA_ORDER, d_hy), F32),
        ],
        compiler_params=_params(("arbitrary", "arbitrary")),
        name="hyena_filter",
    )(z.T, z[:, 0:1], w1p.T, vec(b1), w2.T, vec(b2), w3.T, vec(b3), _split_bf16(w4, 2), vec(freq), deltas)


CONV_ROWS = 64
CONV_HALO = 8


def _shortconv_kernel(p_ref, w_ref, b_ref, lat_ref, ctx_ref, pad_ref, *, n_lat, n_ctx):
    t = n_lat + n_ctx
    tc = p_ref.shape[1]
    pad_ref[0:CONV_HALO, :] = jnp.zeros((CONV_HALO, tc), F32)
    pad_ref[t + CONV_HALO:t + 2 * CONV_HALO, :] = jnp.zeros((CONV_HALO, tc), F32)

    def fill(r, carry):
        src = pl.ds(pl.multiple_of(r * CONV_ROWS, CONV_ROWS), CONV_ROWS)
        dst = pl.ds(pl.multiple_of(r * CONV_ROWS + CONV_HALO, CONV_HALO), CONV_ROWS)
        pad_ref[dst, :] = p_ref[src, :].astype(F32)
        return carry

    lax.fori_loop(0, t // CONV_ROWS, fill, 0)
    w0, w1, w2, bias = w_ref[0:1, :], w_ref[1:2, :], w_ref[2:3, :], b_ref[...]
    ext_rows = CONV_ROWS + 2 * CONV_HALO
    inner = slice(CONV_HALO, CONV_HALO + CONV_ROWS)

    def segment(out_ref, row0, n_rows):
        def body(r, carry):
            ext = pad_ref[pl.ds(pl.multiple_of(row0 + r * CONV_ROWS, CONV_HALO), ext_rows), :]
            rid = r * CONV_ROWS + lax.broadcasted_iota(jnp.int32, (CONV_ROWS, 1), 0)
            prev = jnp.where(rid == 0, 0.0, pltpu.roll(ext, 1, 0)[inner])
            nxt = jnp.where(rid == n_rows - 1, 0.0, pltpu.roll(ext, ext_rows - 1, 0)[inner])
            y = prev * w0 + ext[inner] * w1 + nxt * w2 + bias
            out_ref[0, pl.ds(pl.multiple_of(r * CONV_ROWS, CONV_ROWS), CONV_ROWS), :] = y.astype(out_ref.dtype)
            return carry

        lax.fori_loop(0, n_rows // CONV_ROWS, body, 0)

    segment(lat_ref, 0, n_lat)
    segment(ctx_ref, n_lat, n_ctx)


def _shortconv(px, col0, d_hy, conv_w, conv_b, n_lat, n_ctx):
    t = px.shape[0]
    tc = _tile(d_hy, 256, LANES)
    assert col0 % tc == 0 and n_lat % CONV_ROWS == 0 and n_ctx % CONV_ROWS == 0
    per = d_hy // tc
    n_grp = HYENA_ORDER + 1
    return pl.pallas_call(
        functools.partial(_shortconv_kernel, n_lat=n_lat, n_ctx=n_ctx),
        grid=(n_grp, per),
        in_specs=[
            pl.BlockSpec((t, tc), lambda g, j: (0, col0 // tc + g * per + j)),
            pl.BlockSpec((3, tc), lambda g, j: (0, g * per + j)),
            pl.BlockSpec((1, tc), lambda g, j: (0, g * per + j)),
        ],
        out_specs=[
            pl.BlockSpec((1, n_lat, tc), lambda g, j: (g, 0, j)),
            pl.BlockSpec((1, n_ctx, tc), lambda g, j: (g, 0, j)),
        ],
        out_shape=[
            jax.ShapeDtypeStruct((n_grp, n_lat, d_hy), BF16),
            jax.ShapeDtypeStruct((n_grp, n_ctx, d_hy), BF16),
        ],
        scratch_shapes=[pltpu.VMEM((t + 2 * CONV_HALO, tc), F32)],
        compiler_params=_params(("parallel", "parallel")),
        name="shortconv",
    )(px, conv_w, conv_b.reshape(1, -1))


DFT_N2 = 256
K1_ALIGN = 8
STAGE2_K1 = 8
T2_BLOCK = 16
STAGE2_UNROLL = 8


def _dft_constants(length):
    n = 2 * length
    n1 = n // DFT_N2
    assert n1 * DFT_N2 == n and n1 % 2 == 0
    n_t1 = n1 // 2
    k1_used = n1 // 2 + 1
    k1p = -(-k1_used // K1_ALIGN) * K1_ALIGN
    k1 = np.arange(k1p, dtype=np.float64)
    live = (k1 < k1_used).astype(np.float64)
    t1 = np.arange(n1, dtype=np.float64)
    ang1 = -2.0 * np.pi * k1[:, None] * t1[None, :] / n1
    f1 = np.stack([np.cos(ang1), np.sin(ang1)]) * live[None, :, None]
    k2 = np.arange(DFT_N2, dtype=np.float64)
    t2 = np.arange(DFT_N2, dtype=np.float64)
    ang2 = -2.0 * np.pi * t2[None, None, :] * (k1[:, None, None] + n1 * k2[None, :, None]) / n
    gr = np.cos(ang2) * live[:, None, None]
    gi = np.sin(ang2) * live[:, None, None]
    g = np.stack([gr, gi, gr.transpose(0, 2, 1), gi.transpose(0, 2, 1)])
    weight = np.where((k1 == 0) | (k1 == n1 // 2), 1.0, 2.0) * live
    angc = 2.0 * np.pi * np.arange(n_t1, dtype=np.float64)[:, None] * k1[None, :] / n1
    cinv = np.stack([np.cos(angc), np.sin(angc)]) * weight[None, None, :] / n
    as_bf16 = lambda a: jnp.asarray(a.astype(np.float32)).astype(BF16)
    eye = np.eye(T2_BLOCK)
    kron = lambda m: np.stack([np.kron(m[0], eye), np.kron(m[1], eye)])
    return dict(n1=n1, n_t1=n_t1, k1p=k1p, f1=as_bf16(kron(f1)), f1_sig=as_bf16(kron(f1[:, :, :n_t1])),
                g=as_bf16(g), cinv=as_bf16(kron(cinv)))


def _dft_stage1_kernel(f_ref, u_ref, a_ref):
    k, r, tc = u_ref.shape
    u = u_ref[...].reshape(k * r, tc)
    for part in range(2):
        a = jnp.dot(f_ref[part], u, preferred_element_type=F32).astype(a_ref.dtype)
        a_ref[part] = a.reshape(a_ref.shape[1:])


def _dft_stage1(f, u4, group):
    _, k, n2, c = u4.shape
    k1p = f.shape[1] // T2_BLOCK
    tc = _tile(c, 256, LANES)
    return pl.pallas_call(
        _dft_stage1_kernel,
        grid=(n2 // T2_BLOCK, c // tc),
        in_specs=[
            pl.BlockSpec((2, k1p * T2_BLOCK, k * T2_BLOCK), lambda i, j: (0, 0, 0)),
            pl.BlockSpec((None, k, T2_BLOCK, tc), lambda i, j: (group, 0, i, j)),
        ],
        out_specs=pl.BlockSpec((2, k1p, T2_BLOCK, tc), lambda i, j: (0, 0, i, j)),
        out_shape=jax.ShapeDtypeStruct((2, k1p, n2, c), BF16),
        compiler_params=_params(("parallel", "parallel")),
        name="dft_stage1",
    )(f, u4)


def _dft_stage2_filter_kernel(a_ref, g_ref, scale_ref, kf_ref, *, k1c):
    scale = scale_ref[...]

    def body(i, carry):
        ar, ai = a_ref[0, i], a_ref[1, i]
        gr, gi = g_ref[0, i], g_ref[1, i]
        dot = functools.partial(jnp.dot, preferred_element_type=F32)
        kf_ref[0, i] = ((dot(gr, ar) - dot(gi, ai)) * scale).astype(kf_ref.dtype)
        kf_ref[1, i] = ((dot(gr, ai) + dot(gi, ar)) * scale).astype(kf_ref.dtype)
        return carry

    lax.fori_loop(0, k1c, body, 0, unroll=STAGE2_UNROLL)


def _dft_stage2_conv_kernel(a_ref, g_ref, kf_ref, b_ref, *, k1c):
    def body(i, carry):
        ar, ai = a_ref[0, i], a_ref[1, i]
        gr, gi = g_ref[0, i], g_ref[1, i]
        dot = functools.partial(jnp.dot, preferred_element_type=F32)
        xr = dot(gr, ar) - dot(gi, ai)
        xi = dot(gr, ai) + dot(gi, ar)
        kr, ki = kf_ref[0, i].astype(F32), kf_ref[1, i].astype(F32)
        zr =(xr * kr - xi * ki).astype(BF16)
        zi = (xr * ki + xi * kr).astype(BF16)
        grt, git = g_ref[2, i], g_ref[3, i]
        b_ref[0, i] = (dot(grt, zr) + dot(git, zi)).astype(b_ref.dtype)
        b_ref[1, i] = (dot(grt, zi) - dot(git, zr)).astype(b_ref.dtype)
        return carry

    lax.fori_loop(0, k1c, body, 0, unroll=STAGE2_UNROLL)


def _dft_stage2(a4, g, kf=None, scale=None):
    _, k1p, n2, c = a4.shape
    k1c = STAGE2_K1
    tc = _tile(c, 256, LANES)
    a_spec = pl.BlockSpec((2, k1c, n2, tc), lambda i, j: (0, i, 0, j))
    g_spec = pl.BlockSpec((4, k1c, n2, n2), lambda i, j: (0, i, 0, 0))
    if kf is None:
        s_spec = pl.BlockSpec((1, tc), lambda i, j: (0, j))
        body, args, in_specs, dtype = _dft_stage2_filter_kernel, (a4, g, scale), [a_spec, g_spec, s_spec], BF16
    else:
        body, args, in_specs, dtype = _dft_stage2_conv_kernel, (a4, g, kf), [a_spec, g_spec, a_spec], BF16
    return pl.pallas_call(
        functools.partial(body, k1c=k1c),
        grid=(k1p // k1c, c // tc),
        in_specs=in_specs,
        out_specs=a_spec,
        out_shape=jax.ShapeDtypeStruct(a4.shape, dtype),
        compiler_params=_params(("parallel", "arbitrary")),
        name="dft_stage2",
    )(*args)


def _dft_stage3_kernel(c_ref, b_ref, u_ref, m_ref, skip_ref, o_ref):
    _, k1p, r, tc = b_ref.shape
    rows = u_ref.shape[0] * r
    y = (jnp.dot(c_ref[0], b_ref[0].reshape(k1p * r, tc), preferred_element_type=F32)
         - jnp.dot(c_ref[1], b_ref[1].reshape(k1p * r, tc), preferred_element_type=F32))
    u = u_ref[...].reshape(rows, tc).astype(F32)
    out = m_ref[...].reshape(rows, tc).astype(F32) * (y + u * skip_ref[...])
    o_ref[0] = out.astype(o_ref.dtype).reshape(o_ref.shape[1:])


def _dft_stage3(cinv, b4, u4, u_group, m4, m_group, skip):
    _, k1p, n2, c = b4.shape
    n_t1 = cinv.shape[1] // T2_BLOCK
    tc = _tile(c, 256, LANES)
    sig_spec = lambda grp: pl.BlockSpec((None, n_t1, T2_BLOCK, tc), lambda i, j: (grp, 0, i, j))
    return pl.pallas_call(
        _dft_stage3_kernel,
        grid=(n2 // T2_BLOCK, c // tc),
        in_specs=[
            pl.BlockSpec((2, n_t1 * T2_BLOCK, k1p * T2_BLOCK), lambda i, j: (0, 0, 0)),
            pl.BlockSpec((2, k1p, T2_BLOCK, tc), lambda i, j: (0, 0, i, j)),
            sig_spec(u_group),
            sig_spec(m_group),
            pl.BlockSpec((1, tc), lambda i, j: (0, j)),
        ],
        out_specs=pl.BlockSpec((1, n_t1, T2_BLOCK, tc), lambda i, j: (0, 0, i, j)),
        out_shape=jax.ShapeDtypeStruct((1, n_t1, n2, c), BF16),
        compiler_params=_params(("parallel", "parallel")),
        name="dft_stage3",
    )(cinv, b4, u4, m4, skip.reshape(1, c))


def _hyena_latent(p_lat, kern, inv_norm, skip, consts):
    _, length, c = p_lat.shape
    n1, n_t1 = consts["n1"], consts["n_t1"]
    p4 = p_lat.reshape(HYENA_ORDER + 1, n_t1, DFT_N2, c)
    k4 = kern.reshape(HYENA_ORDER, n1, DFT_N2, c)
    u4, u_group = p4, 0
    for order in range(HYENA_ORDER):
        kf = _dft_stage2(_dft_stage1(consts["f1"], k4, order), consts["g"], scale=inv_norm[order:order + 1])
        b = _dft_stage2(_dft_stage1(consts["f1_sig"], u4, u_group), consts["g"], kf)
        u4 = _dft_stage3(consts["cinv"], b, u4, u_group, p4, order + 1, skip[order])
        u_group = 0
    return u4.reshape(length, c)


def _ctx_dft_constants(length):
    n = 2 * length
    k = np.arange(n, dtype=np.float64)
    ang = -2.0 * np.pi * k[:, None] * k[None, :] / n
    fwd = np.stack([np.cos(ang), np.sin(ang)])
    inv = np.stack([np.cos(ang[:length]), -np.sin(ang[:length])]) / n
    as_f32 = lambda a: jnp.asarray(a.astype(np.float32))
    return dict(fwd=as_f32(fwd), fwd_sig=as_f32(fwd[:, :, :length]), inv=as_f32(inv))


def _ctxconv_kernel(ff_ref, fs_ref, fi_ref, k_ref, scale_ref, u_ref, m_ref, skip_ref, o_ref):
    dot = functools.partial(jnp.dot, preferred_element_type=F32, precision=HIGHEST)
    k = k_ref[...] * scale_ref[...]
    u = u_ref[...].astype(F32)
    kr, ki = dot(ff_ref[0], k), dot(ff_ref[1], k)
    xr, xi = dot(fs_ref[0], u), dot(fs_ref[1], u)
    zr = xr * kr - xi * ki
    zi = xr * ki + xi * kr
    y = dot(fi_ref[0], zr) - dot(fi_ref[1], zi)
    o_ref[0] = (m_ref[...].astype(F32) * (y + u * skip_ref[...])).astype(o_ref.dtype)


def _ctxconv(consts, kern, order, inv_norm, u3, u_group, m3, m_group, skip):
    _, length, c = u3.shape
    n = 2 * length
    tc = _tile(c, 256, LANES)
    return pl.pallas_call(
        _ctxconv_kernel,
        grid=(c // tc,),
        in_specs=[
            pl.BlockSpec((2, n, n), lambda j: (0, 0, 0)),
            pl.BlockSpec((2, n, length), lambda j: (0, 0, 0)),
            pl.BlockSpec((2, length, n), lambda j: (0, 0, 0)),
            pl.BlockSpec((None, n, tc), lambda j: (order, 0, j)),
            pl.BlockSpec((1, tc), lambda j: (0, j)),
            pl.BlockSpec((None, length, tc), lambda j: (u_group, 0, j)),
            pl.BlockSpec((None, length, tc), lambda j: (m_group, 0, j)),
            pl.BlockSpec((1, tc), lambda j: (0, j)),
        ],
        out_specs=pl.BlockSpec((1, length, tc), lambda j: (0, 0, j)),
        out_shape=jax.ShapeDtypeStruct((1, length, c), BF16),
        compiler_params=_params(("parallel",)),
        name="ctxconv",
    )(consts["fwd"], consts["fwd_sig"], consts["inv"], kern, inv_norm[order:order + 1], u3, m3, skip.reshape(1, c))


def _hyena_context(p_ctx, kern, inv_norm, skip, consts):
    u3, u_group = p_ctx, 0
    for order in range(HYENA_ORDER):
        u3 = _ctxconv(consts, kern, order, inv_norm, u3, u_group, p_ctx, order + 1, skip[order])
        u_group = 0
    return u3[0]


def kernel(x, c, ctx, c_ctx, norm1_g, norm2_g, mod_a, mod_b, mod_bias, w_in, q_norm_g, k_norm_g, hy_conv_w, hy_conv_b, hy_w1, hy_b1, hy_w2, hy_b2, hy_w3, hy_b3, hy_w4, hy_freq, hy_skip, gm_ln_g, gm_ln_b, gm_ws, gm_bs, gate_up, gate_b, w_br, w_out, rt_group_w, rt_group_b, rt_expert_w, rt_expert_b, ex_w_gate, ex_w_up, ex_w_down, final_norm_g):
    bsz, seq, d = x.shape
    assert bsz == 1 and c.shape[0] == 1
    n_ctx = ctx.shape[1]
    depth = w_in.shape[0]
    t_all = seq + n_ctx
    d_hy = hy_skip.shape[2]
    d_gm = gm_ln_g.shape[1]
    d_at = w_br.shape[1] - d_hy - d_gm
    gate_rank = gate_up.shape[2]
    n_q = d_at // HEAD_DIM
    n_kv = n_q // KV_REP
    q_end = d_at
    v_start = q_end + n_kv * HEAD_DIM
    v_end = v_start + n_kv * HEAD_DIM
    hy_end = v_end + (HYENA_ORDER + 1) * d_hy
    gm_end = hy_end + 2 * d_gm
    n_groups = rt_group_w.shape[2]
    per_group = rt_expert_w.shape[3]
    n_experts = n_groups * per_group
    assert n_groups + n_experts <= ROUTER_COLS
    assert hy_end % (2 * d_gm) == 0 and gm_end % gate_rank == 0 and seq % ROW_CHUNK == 0
    tme = 256

    xs = jnp.concatenate([x[0], ctx[0]], axis=0)
    cond = jnp.stack([c[0], c_ctx], axis=0)
    cos, sin = _rope_tables(seq, n_ctx)

    tq = _tile(seq, 512, 16)
    tk = _tile(t_all, 2816, 256)
    tq_c = _tile(n_ctx, 256, 16)
    dft_lat = _dft_constants(seq)
    dft_ctx = _ctx_dft_constants(n_ctx)
    w_in_b, w_br_b, gate_up_b, w_out_b = (w.astype(BF16) for w in (w_in, w_br, gate_up, w_out))

    for l in range(depth):
        m = (jnp.dot(jnp.dot(jax.nn.silu(cond), mod_a[l], precision=HIGHEST), mod_b[l], precision=HIGHEST)
             + mod_bias[l]).reshape(2, N_MOD, d)

        px = _modproj(xs, norm1_g[l], m[:, 0], m[:, 1], w_in_b, l, seq)
        q, k = _qkprep(px, cos, sin, q_norm_g[l], k_norm_g[l], n_q, n_kv)
        o_at = _attention(q, k, px, v_start // HEAD_DIM, 0, seq, 0, t_all, tq, tk)
        o_at = _attention(q, k, px, v_start // HEAD_DIM, seq, n_ctx, seq, n_ctx, tq_c, n_ctx, out=o_at)

        hy_w = (hy_w1[l], hy_b1[l], hy_w2[l], hy_b2[l], hy_w3[l], hy_b3[l], hy_w4[l], hy_freq[l])
        p_lat, p_ctx = _shortconv(px, v_end, d_hy, hy_conv_w[l], hy_conv_b[l], seq, n_ctx)
        kern_x, norm_x = _hyena_filters(seq, *hy_w, d_hy, BF16)
        kern_c, norm_c = _hyena_filters(n_ctx, *hy_w, d_hy, F32)
        o_hy_x = _hyena_latent(p_lat, kern_x, 1.0 / norm_x, hy_skip[l], dft_lat)
        o_hy_c = _hyena_context(p_ctx, kern_c, 1.0 / norm_c, hy_skip[l], dft_ctx)
        o_hy = jnp.concatenate([o_hy_x, o_hy_c], axis=0)

        o_gm = _gmlp(px, hy_end // (2 * d_gm), d_gm, gm_ln_g[l], gm_ln_b[l], gm_ws[l], gm_bs[l])

        merged = _merge(o_hy, o_at, o_gm, px, gm_end // gate_rank, gate_rank, w_br_b, gate_up_b, gate_b, l)
        xs = _outproj(merged, w_out_b, l, xs, m[:, 2], seq)

        rw = jnp.concatenate([rt_group_w[l]] + [rt_expert_w[l, g] for g in range(n_groups)], axis=1)
        rw = jnp.pad(rw, ((0, 0), (0, ROUTER_COLS - rw.shape[1])))
        rb = jnp.concatenate([rt_group_b[l], rt_expert_b[l].reshape(-1)])
        rb = jnp.pad(rb, (0, ROUTER_COLS - rb.shape[0])).reshape(1, ROUTER_COLS)
        h2, logits = _router(xs, norm2_g[l], m[:, 3], m[:, 4], _split_bf16(rw, 3), rb, seq)
        eid, wts = _route(logits, n_groups, per_group)
        src_tok, roww, pos, tiles = _dispatch(eid, wts, n_experts, tme)
        take_rows = lambda a, idx: a.at[idx].get(mode="promise_in_bounds")
        xg = take_rows(h2, src_tok)
        yo = _experts(xg, roww, tiles, ex_w_gate, ex_w_up, ex_w_down, l, tme)
        xs = _combine(xs, take_rows(yo, pos[:, 0]), take_rows(yo, pos[:, 1]), m[:, 5], seq)

    return _final_norm(xs, final_norm_g, seq).reshape(1, seq, d)
```
